```python
import math
import jax
import jax.numpy as jnp
from jax import lax
import numpy as np

D_MODEL = 2048
BATCH = 16
SEQ = 2048
DEPTH = 2

BRANCH_WIDTH = 1024
N_BRANCH = 3
M_HEADS = 4
M_HEAD_DIM = BRANCH_WIDTH // M_HEADS
M_WIDTH = M_HEADS * M_HEAD_DIM
M_CONV = 4
M_CHUNK = 64
DA_HEADS = 4
DA_HEAD_DIM = 128
DA_V_DIM = 2 * DA_HEAD_DIM
DA_QK_WIDTH = DA_HEADS * 2 * DA_HEAD_DIM
DA_WIDTH = DA_HEADS * DA_V_DIM
Q_BLOCK = 128
REL_BUCKETS = 32
REL_MAX_DIST = 128
S5_WIDTH = BRANCH_WIDTH
S5_GROUP = 16
S5_GROUPS = S5_WIDTH // S5_GROUP
S5_STATE = 64
D_FF = 5632
FFN_CONV = 3
EPS = 1e-6
IN_SIZES = (M_WIDTH, M_WIDTH, M_WIDTH, M_WIDTH, M_HEADS, M_HEADS,
            DA_QK_WIDTH, DA_QK_WIDTH, DA_WIDTH, S5_WIDTH, N_BRANCH * D_MODEL)
N_IN = sum(IN_SIZES)

kernel_name = 'hybrid_gated_mlstm_diffattn_s5'


def rmsnorm(x, gain):
    xf = x.astype(jnp.float32)
    var = jnp.mean(xf * xf, axis=-1, keepdims=True)
    return (xf * lax.rsqrt(var + EPS) * gain.astype(jnp.float32)).astype(x.dtype)


def causal_dwconv(x, w):
    K = w.shape[0]
    L = x.shape[1]
    xp = jnp.pad(x, ((0, 0), (K - 1, 0), (0, 0)))
    return sum(w[j] * xp[:, j:j + L] for j in range(K))


def t5_causal_bucket(dist):
    n = jnp.maximum(dist, 0)
    max_exact = REL_BUCKETS // 2
    nf = jnp.maximum(n, 1).astype(jnp.float32)
    large = max_exact + (jnp.log(nf / max_exact) / math.log(REL_MAX_DIST / max_exact)
                         * (REL_BUCKETS - max_exact)).astype(jnp.int32)
    large = jnp.minimum(large, REL_BUCKETS - 1)
    return jnp.where(n < max_exact, n, large)


def mlstm_branch(q, k, v, o, i_pre, f_pre, conv_w, norm_g):
    dtype = v.dtype
    B, L, _ = q.shape
    nc = L // M_CHUNK
    qk = jax.nn.silu(causal_dwconv(jnp.concatenate([q, k], axis=-1), conv_w))
    q, k = jnp.split(qk, 2, axis=-1)

    def to_chunks(t):
        return t.astype(jnp.float32).reshape(B, nc, M_CHUNK, M_HEADS, -1).transpose(1, 0, 3, 2, 4)

    def gate_chunks(t):
        return t.astype(jnp.float32).reshape(B, nc, M_CHUNK, M_HEADS).transpose(1, 0, 3, 2)

    qc = to_chunks(q)
    kc = to_chunks(k) * (M_HEAD_DIM ** -0.5)
    vc = to_chunks(v)
    log_i = gate_chunks(i_pre)
    log_f = jax.nn.log_sigmoid(gate_chunks(f_pre))
    causal = jnp.tril(jnp.ones((M_CHUNK, M_CHUNK), dtype=bool))

    def step(carry, inp):
        C, n, m = carry
        qt, kt, vt, li, lf = inp
        b = jnp.cumsum(lf, axis=-1)
        g = b[..., -1]
        dmat = b[..., :, None] - b[..., None, :] + li[..., None, :]
        dmat = jnp.where(causal, dmat, -jnp.inf)
        inter = b + m[..., None]
        m_t = jnp.maximum(inter, jnp.max(dmat, axis=-1))
        s = jnp.einsum('bhtd,bhsd->bhts', qt, kt) * jnp.exp(dmat - m_t[..., None])
        w_inter = jnp.exp(inter - m_t)
        num = (jnp.einsum('bhts,bhsv->bhtv', s, vt)
               + w_inter[..., None] * jnp.einsum('bhtd,bhdv->bhtv', qt, C))
        den = jnp.sum(s, axis=-1) + w_inter * jnp.einsum('bhtd,bhd->bht', qt, n)
        h = num / jnp.maximum(jnp.abs(den), jnp.exp(-m_t))[..., None]
        a_s = g[..., None] - b + li
        m_new = jnp.maximum(g + m, jnp.max(a_s, axis=-1))
        ws = jnp.exp(a_s - m_new[..., None])
        decay = jnp.exp(g + m - m_new)
        C_new = decay[..., None, None] * C + jnp.einsum('bhs,bhsd,bhsv->bhdv', ws, kt, vt)
        n_new = decay[..., None] * n + jnp.einsum('bhs,bhsd->bhd', ws, kt)
        return (C_new, n_new, m_new), h

    init = (jnp.zeros((B, M_HEADS, M_HEAD_DIM, M_HEAD_DIM), jnp.float32),
            jnp.zeros((B, M_HEADS, M_HEAD_DIM), jnp.float32),
            jnp.zeros((B, M_HEADS), jnp.float32))
    _, h = lax.scan(step, init, (qc, kc, vc, log_i, log_f))
    h = h.transpose(1, 0, 3, 2, 4).reshape(B, L, M_HEADS, M_HEAD_DIM)
    h = rmsnorm(h, norm_g.reshape(M_HEADS, M_HEAD_DIM)).reshape(B, L, M_WIDTH)
    return (jax.nn.sigmoid(o.astype(jnp.float32)) * h).astype(dtype)


def diff_attention_branch(q, k, v, lam, norm_g, rel_bias, lambda_init):
    dtype = v.dtype
    B, L, _ = q.shape
    nb = L // Q_BLOCK
    q = q.astype(jnp.float32).reshape(B, L, DA_HEADS, 2, DA_HEAD_DIM) * (DA_HEAD_DIM ** -0.5)
    k = k.astype(jnp.float32).reshape(B, L, DA_HEADS, 2, DA_HEAD_DIM)
    v = v.astype(jnp.float32).reshape(B, L, DA_HEADS, DA_V_DIM)
    lam = lam.astype(jnp.float32)
    lam_full = jnp.exp(jnp.sum(lam[0] * lam[1])) - jnp.exp(jnp.sum(lam[2] * lam[3])) + lambda_init
    bias_table = rel_bias.astype(jnp.float32)
    qb = q.reshape(B, nb, Q_BLOCK, DA_HEADS, 2, DA_HEAD_DIM).transpose(1, 0, 2, 3, 4, 5)
    k_pos = jnp.arange(L)

    def block(args):
        qblk, start = args
        q_pos = start + jnp.arange(Q_BLOCK)
        dist = q_pos[:, None] - k_pos[None, :]
        bias = jnp.transpose(bias_table[t5_causal_bucket(dist)], (2, 0, 1))
        logits = jnp.einsum('bqhmd,bkhmd->bhmqk', qblk, k) + bias[None, :, None]
        logits = jnp.where(dist >= 0, logits, -jnp.inf)
        p = jax.nn.softmax(logits, axis=-1)
        attn = p[:, :, 0] - lam_full * p[:, :, 1]
        return jnp.einsum('bhqk,bkhv->bqhv', attn, v)

    starts = jnp.arange(nb, dtype=jnp.int32) * Q_BLOCK
    out = lax.map(block, (qb, starts))
    out = out.transpose(1, 0, 2, 3, 4).reshape(B, L, DA_HEADS, DA_V_DIM)
    out = rmsnorm(out, norm_g.reshape(DA_HEADS, DA_V_DIM)) * (1.0 - lambda_init)
    return out.reshape(B, L, DA_WIDTH).astype(dtype)


def s5_branch(u, lam_re, lam_im, log_dt, b_re, b_im, c_re, c_im, d_skip, w_glu):
    dtype = u.dtype
    B, L, _ = u.shape
    uf = u.astype(jnp.float32)
    ug = uf.reshape(B, L, S5_GROUPS, S5_GROUP)
    lam_re = lam_re.astype(jnp.float32)
    lam_im = lam_im.astype(jnp.float32)
    dt = jnp.exp(log_dt.astype(jnp.float32))[:, None]
    mag = jnp.exp(lam_re * dt)
    a_re = mag * jnp.cos(lam_im * dt)
    a_im = mag * jnp.sin(lam_im * dt)
    den = lam_re * lam_re + lam_im * lam_im
    z_re = ((a_re - 1.0) * lam_re + a_im * lam_im) / den
    z_im = (a_im * lam_re - (a_re - 1.0) * lam_im) / den
    b_re = b_re.astype(jnp.float32)
    b_im = b_im.astype(jnp.float32)
    bb_re = z_re[..., None] * b_re - z_im[..., None] * b_im
    bb_im = z_re[..., None] * b_im + z_im[..., None] * b_re
    bu_re = jnp.einsum('gpc,blgc->blgp', bb_re, ug)
    bu_im = jnp.einsum('gpc,blgc->blgp', bb_im, ug)
    a_re_t = jnp.broadcast_to(a_re, (1, L, S5_GROUPS, S5_STATE))
    a_im_t = jnp.broadcast_to(a_im, (1, L, S5_GROUPS, S5_STATE))

    def combine(e1, e2):
        a1r, a1i, b1r, b1i = e1
        a2r, a2i, b2r, b2i = e2
        return (a2r * a1r - a2i * a1i,
                a2r * a1i + a2i * a1r,
                a2r * b1r - a2i * b1i + b2r,
                a2r * b1i + a2i * b1r + b2i)

    _, _, x_re, x_im = lax.associative_scan(combine, (a_re_t, a_im_t, bu_re, bu_im), axis=1)
    y = (jnp.einsum('gcp,blgp->blgc', c_re.astype(jnp.float32), x_re)
         - jnp.einsum('gcp,blgp->blgc', c_im.astype(jnp.float32), x_im))
    y = y.reshape(B, L, S5_WIDTH) + d_skip.astype(jnp.float32) * uf
    y = jax.nn.gelu(y)
    ab = jnp.einsum('blw,wv->blv', y, w_glu.astype(jnp.float32))
    a, g = jnp.split(ab, 2, axis=-1)
    return (a * jax.nn.sigmoid(g)).astype(dtype)


def setup_inputs(seed: int = 0) -> dict:
    key = jax.random.key(seed)
    ks = jax.random.split(key, 32)
    f32 = jnp.float32

    def nrm(k, shape, scale):
        return jax.random.normal(k, shape, f32) * scale

    def gain(k, shape):
        return 1.0 + 0.05 * jax.random.normal(k, shape, f32)

    x = nrm(ks[0], (BATCH, SEQ, D_MODEL), 1.0)
    norm_mix_pre = gain(ks[1], (DEPTH, D_MODEL))
    norm_mix_post = gain(ks[2], (DEPTH, D_MODEL))
    norm_ffn_pre = gain(ks[3], (DEPTH, D_MODEL))
    norm_ffn_post = gain(ks[4], (DEPTH, D_MODEL))
    w_in = nrm(ks[5], (DEPTH, D_MODEL, N_IN), D_MODEL ** -0.5)
    b_i = nrm(ks[6], (DEPTH, M_HEADS), 0.1)
    b_f = jnp.linspace(3.0, 6.0, M_HEADS, dtype=f32) + nrm(ks[7], (DEPTH, M_HEADS), 0.1)
    mlstm_b_if = jnp.stack([b_i, b_f], axis=1)
    mlstm_conv = nrm(ks[8], (DEPTH, M_CONV, 2 * M_WIDTH), M_CONV ** -0.5)
    mlstm_norm = gain(ks[9], (DEPTH, M_WIDTH))
    diff_lambda = nrm(ks[10], (DEPTH, 4, DA_HEAD_DIM), 0.1)
    diff_norm = gain(ks[11], (DEPTH, DA_WIDTH))
    rel_bias = nrm(ks[12], (REL_BUCKETS, DA_HEADS), 0.5)
    n_idx = jnp.arange(S5_STATE, dtype=f32)
    s5_lambda_re = -0.5 + nrm(ks[13], (DEPTH, S5_GROUPS, S5_STATE), 1e-3)
    s5_lambda_im = math.pi * n_idx + nrm(ks[14], (DEPTH, S5_GROUPS, S5_STATE), 1e-3)
    s5_log_dt = jax.random.uniform(ks[15], (DEPTH, S5_GROUPS), f32, math.log(1e-3), math.log(1e-1))
    s5_b_re = nrm(ks[16], (DEPTH, S5_GROUPS, S5_STATE, S5_GROUP), (2 * S5_GROUP) ** -0.5)
    s5_b_im = nrm(ks[17], (DEPTH, S5_GROUPS, S5_STATE, S5_GROUP), (2 * S5_GROUP) ** -0.5)
    s5_c_re = nrm(ks[18], (DEPTH, S5_GROUPS, S5_GROUP, S5_STATE), (2 * S5_STATE) ** -0.5)
    s5_c_im = nrm(ks[19], (DEPTH, S5_GROUPS, S5_GROUP, S5_STATE), (2 * S5_STATE) ** -0.5)
    s5_d = nrm(ks[20], (DEPTH, S5_WIDTH), 0.5)
    s5_w_glu = nrm(ks[21], (DEPTH, S5_WIDTH, 2 * S5_WIDTH), S5_WIDTH ** -0.5)
    w_branch = nrm(ks[22], (DEPTH, N_BRANCH, BRANCH_WIDTH, D_MODEL), BRANCH_WIDTH ** -0.5)
    w_out = nrm(ks[23], (DEPTH, D_MODEL, D_MODEL), D_MODEL ** -0.5)
    w_up = nrm(ks[24], (DEPTH, D_MODEL, 2 * D_FF), D_MODEL ** -0.5)
    ffn_conv = nrm(ks[25], (DEPTH, FFN_CONV, 2 * D_FF), FFN_CONV ** -0.5)
    ffn_conv_b = nrm(ks[26], (DEPTH, 2 * D_FF), 0.02)
    w_down = nrm(ks[27], (DEPTH, D_FF, D_MODEL), D_FF ** -0.5)
    return {'x': x, 'norm_mix_pre': norm_mix_pre, 'norm_mix_post': norm_mix_post,
            'norm_ffn_pre': norm_ffn_pre, 'norm_ffn_post': norm_ffn_post, 'w_in': w_in,
            'mlstm_b_if': mlstm_b_if, 'mlstm_conv': mlstm_conv, 'mlstm_norm': mlstm_norm,
            'diff_lambda': diff_lambda, 'diff_norm': diff_norm, 'rel_bias': rel_bias,
            's5_lambda_re': s5_lambda_re, 's5_lambda_im': s5_lambda_im, 's5_log_dt': s5_log_dt,
            's5_b_re': s5_b_re, 's5_b_im': s5_b_im, 's5_c_re': s5_c_re, 's5_c_im': s5_c_im,
            's5_d': s5_d, 's5_w_glu': s5_w_glu, 'w_branch': w_branch, 'w_out': w_out,
            'w_up': w_up, 'ffn_conv': ffn_conv, 'ffn_conv_b': ffn_conv_b, 'w_down': w_down}


def reference(x, norm_mix_pre, norm_mix_post, norm_ffn_pre, norm_ffn_post, w_in,
              mlstm_b_if, mlstm_conv, mlstm_norm, diff_lambda, diff_norm, rel_bias,
              s5_lambda_re, s5_lambda_im, s5_log_dt, s5_b_re, s5_b_im, s5_c_re, s5_c_im,
              s5_d, s5_w_glu, w_branch, w_out, w_up, ffn_conv, ffn_conv_b, w_down):
    B, L, _ = x.shape
    split_points = [int(s) for s in np.cumsum(IN_SIZES)[:-1]]
    for layer in range(DEPTH):
        h = rmsnorm(x, norm_mix_pre[layer])
        proj = jnp.einsum('bld,dn->bln', h, w_in[layer])
        (q_m, k_m, v_m, o_m, i_m, f_m, q_d, k_d, v_d, u_s,
         gate_pre) = jnp.split(proj, split_points, axis=-1)
        y_a = mlstm_branch(q_m, k_m, v_m, o_m, i_m + mlstm_b_if[layer, 0],
                           f_m + mlstm_b_if[layer, 1], mlstm_conv[layer], mlstm_norm[layer])
        lambda_init = 0.8 - 0.6 * math.exp(-0.3 * layer)
        y_b = diff_attention_branch(q_d, k_d, v_d, diff_lambda[layer], diff_norm[layer],
                                    rel_bias, lambda_init)
        y_c = s5_branch(u_s, s5_lambda_re[layer], s5_lambda_im[layer], s5_log_dt[layer],
                        s5_b_re[layer], s5_b_im[layer], s5_c_re[layer], s5_c_im[layer],
                        s5_d[layer], s5_w_glu[layer])
        ys = jnp.stack([y_a, y_b, y_c], axis=2)
        z = jnp.einsum('blnw,nwd->blnd', ys, w_branch[layer])
        gates = jax.nn.sigmoid(gate_pre.reshape(B, L, N_BRANCH, D_MODEL))
        merged = jnp.sum(gates * z, axis=2)
        mix = jnp.einsum('bld,de->ble', merged, w_out[layer])
        x = x + rmsnorm(mix, norm_mix_post[layer])
        h = rmsnorm(x, norm_ffn_pre[layer])
        up = causal_dwconv(jnp.einsum('bld,df->blf', h, w_up[layer]), ffn_conv[layer]) + ffn_conv_b[layer]
        a, v = jnp.split(up, 2, axis=-1)
        ffn = jnp.einsum('blf,fd->bld', jax.nn.gelu(a) * v, w_down[layer])
        x = x + rmsnorm(ffn, norm_ffn_post[layer])
    return x
```

```python
import functools
import math

import jax
import jax.numpy as jnp
from jax import lax
from jax.experimental import pallas as pl
from jax.experimental.pallas import tpu as pltpu

F32 = jnp.float32
BF16 = jnp.bfloat16
HIGHEST = lax.Precision.HIGHEST

D_MODEL = 2048
DEPTH = 2
BRANCH_WIDTH = 1024
N_BRANCH = 3
M_HEADS = 4
M_HEAD_DIM = 256
M_CONV = 4
DA_HEADS = 4
DA_HEAD_DIM = 128
DA_V_DIM = 256
REL_BUCKETS = 32
REL_MAX_DIST = 128
S5_GROUP = 16
S5_GROUPS = 64
S5_STATE = 64
D_FF = 5632
FFN_CONV = 3
EPS = 1e-6

LANES = 128
SUBLANES = 8
VMEM_LIMIT = 56 * 1024 * 1024

N_MAIN = 14336
COL_QM, COL_KM, COL_VM, COL_OM = 0, 1, 2, 3
COL_QD, COL_KD, COL_VD, COL_US = 4, 5, 6, 7
COL_GATE = 8
N_GATE_PAD = LANES

S5_SLABS = 8
S5_SLAB_STATES = 512

NEG_BIG = -1e30


def _cparams(sem):
    return pltpu.CompilerParams(dimension_semantics=sem, vmem_limit_bytes=VMEM_LIMIT)


def _sigmoid(x):
    return 1.0 / (1.0 + jnp.exp(-x))


def _gelu_tanh(x):
    c = math.sqrt(2.0 / math.pi)
    return 0.5 * x * (1.0 + jnp.tanh(c * (x + 0.044715 * (x * x * x))))


def _rms(x, gain):
    var = jnp.mean(x * x, axis=-1, keepdims=True)
    return x * lax.rsqrt(var + EPS) * gain


def _in_proj_kernel(x_ref, g_ref, wg_ref, w_ref, o_ref, og_ref, h_scr):
    @pl.when(pl.program_id(1) == 0)
    def _():
        hf = _rms(x_ref[...], g_ref[...])
        og_ref[...] = jnp.dot(hf, wg_ref[...], preferred_element_type=F32,
                              precision=HIGHEST)
        h_scr[...] = hf.astype(BF16)

    o_ref[...] = jnp.dot(h_scr[...], w_ref[...],
                         preferred_element_type=F32).astype(BF16)


def _in_proj(x2d, gain, w_gate, w_main, *, tm, tn):
    m = x2d.shape[0]
    return pl.pallas_call(
        _in_proj_kernel,
        out_shape=(jax.ShapeDtypeStruct((m, N_MAIN), BF16),
                   jax.ShapeDtypeStruct((m, N_GATE_PAD), F32)),
        grid=(m // tm, N_MAIN // tn),
        in_specs=[pl.BlockSpec((tm, D_MODEL), lambda i, n: (i, 0)),
                  pl.BlockSpec((1, D_MODEL), lambda i, n: (0, 0)),
                  pl.BlockSpec((D_MODEL, N_GATE_PAD), lambda i, n: (0, 0)),
                  pl.BlockSpec((D_MODEL, tn), lambda i, n: (0, n))],
        out_specs=(pl.BlockSpec((tm, tn), lambda i, n: (i, n)),
                   pl.BlockSpec((tm, N_GATE_PAD), lambda i, n: (i, 0))),
        scratch_shapes=[pltpu.VMEM((tm, D_MODEL), BF16)],
        compiler_params=_cparams(("parallel", "arbitrary")),
        name="in_proj",
    )(x2d, gain, w_gate, w_main)


def _mlstm_kernel(q_ref, k_ref, v_ref, o_ref, gt_ref, gb_ref, cw_ref, ng_ref, y_ref,
                  c_scr, n_scr, m_scr, qe_scr, ke_scr, *, chunk):
    t = chunk
    hd = M_HEAD_DIM
    width = M_HEADS * hd

    @pl.when(pl.program_id(1) == 0)
    def _():
        c_scr[...] = jnp.zeros_like(c_scr)
        n_scr[...] = jnp.zeros_like(n_scr)
        m_scr[...] = jnp.zeros_like(m_scr)
        qe_scr[0:SUBLANES, :] = jnp.zeros((SUBLANES, width), F32)
        ke_scr[0:SUBLANES, :] = jnp.zeros((SUBLANES, width), F32)

    qe_scr[SUBLANES:SUBLANES + t, :] = q_ref[...].astype(F32)
    ke_scr[SUBLANES:SUBLANES + t, :] = k_ref[...].astype(F32)

    gates = gt_ref[...] + gb_ref[...]
    log_f = jnp.minimum(gates, 0.0) - jnp.log1p(jnp.exp(-jnp.abs(gates)))
    row = lax.broadcasted_iota(jnp.int32, (t, t), 0)
    col = lax.broadcasted_iota(jnp.int32, (t, t), 1)
    causal = col <= row
    cum = jnp.dot(causal.astype(F32), log_f, preferred_element_type=F32,
                  precision=HIGHEST)
    gates_t = gates.T
    cum_t = cum.T

    for h in range(M_HEADS):
        sl = slice(h * hd, (h + 1) * hd)
        ksl = slice(width + h * hd, width + (h + 1) * hd)
        qc = jnp.zeros((t, hd), F32)
        kc = jnp.zeros((t, hd), F32)
        for j in range(M_CONV):
            off = SUBLANES - (M_CONV - 1) + j
            qc = qc + cw_ref[j:j + 1, sl] * qe_scr[off:off + t, sl]
            kc = kc + cw_ref[j:j + 1, ksl] * ke_scr[off:off + t, sl]
        qc = qc * _sigmoid(qc)
        kc = kc * _sigmoid(kc) * (hd ** -0.5)
        qb = qc.astype(BF16)
        kb = kc.astype(BF16)
        vb = v_ref[:, sl]

        li_row = gates_t[h:h + 1, :]
        b_row = cum_t[M_HEADS + h:M_HEADS + h + 1, :]
        li_col = gates[:, h:h + 1]
        b_col = cum[:, M_HEADS + h:M_HEADS + h + 1]
        m_prev = m_scr[h:h + 1, 0:1]
        c_prev = c_scr[h]
        n_prev = n_scr[h:h + 1, :]

        dmat = jnp.where(causal, b_col - b_row + li_row, -jnp.inf)
        inter = b_col + m_prev
        m_t = jnp.maximum(inter, jnp.max(dmat, axis=-1, keepdims=True))
        s = lax.dot_general(qb, kb, (((1,), (1,)), ((), ())),
                            preferred_element_type=F32) * jnp.exp(dmat - m_t)
        w_inter = jnp.exp(inter - m_t)
        num = (jnp.dot(s.astype(BF16), vb, preferred_element_type=F32)
               + w_inter * jnp.dot(qb, c_prev.astype(BF16), preferred_element_type=F32))
        den = (jnp.sum(s, axis=-1, keepdims=True)
               + w_inter * jnp.sum(qc * n_prev, axis=-1, keepdims=True))
        hh = num / jnp.maximum(jnp.abs(den), jnp.exp(-m_t))

        g = cum[t - 1:t, M_HEADS + h:M_HEADS + h + 1]
        a_col = g - b_col + li_col
        m_new = jnp.maximum(g + m_prev, jnp.max(a_col, axis=0, keepdims=True))
        ws = jnp.exp(a_col - m_new)
        decay = jnp.exp(g + m_prev - m_new)
        kw = ws * kc
        c_scr[h] = decay * c_prev + lax.dot_general(
            kw.astype(BF16), vb, (((0,), (0,)), ((), ())), preferred_element_type=F32)
        n_scr[h:h + 1, :] = decay * n_prev + jnp.sum(kw, axis=0, keepdims=True)
        m_scr[h:h + 1, :] = jnp.broadcast_to(m_new, (1, LANES))

        hn = _rms(hh, ng_ref[:, sl])
        y_ref[:, sl] = (_sigmoid(o_ref[:, sl].astype(F32)) * hn).astype(BF16)

    qe_scr[0:SUBLANES, :] = qe_scr[t:t + SUBLANES, :]
    ke_scr[0:SUBLANES, :] = ke_scr[t:t + SUBLANES, :]


def _mlstm(proj, gates, gate_bias, conv_w, norm_g, *, batch, seq, chunk):
    nc = seq // chunk
    width = M_HEADS * M_HEAD_DIM

    def col_spec(cb):
        return pl.BlockSpec((chunk, width), lambda b, c: (b * nc + c, cb))

    return pl.pallas_call(
        functools.partial(_mlstm_kernel, chunk=chunk),
        out_shape=jax.ShapeDtypeStruct((batch * seq, width), BF16),
        grid=(batch, nc),
        in_specs=[col_spec(COL_QM), col_spec(COL_KM), col_spec(COL_VM), col_spec(COL_OM),
                  pl.BlockSpec((chunk, N_GATE_PAD), lambda b, c: (b * nc + c, 0)),
                  pl.BlockSpec((1, N_GATE_PAD), lambda b, c: (0, 0)),
                  pl.BlockSpec((M_CONV, 2 * width), lambda b, c: (0, 0)),
                  pl.BlockSpec((1, width), lambda b, c: (0, 0))],
        out_specs=pl.BlockSpec((chunk, width), lambda b, c: (b * nc + c, 0)),
        scratch_shapes=[pltpu.VMEM((M_HEADS, M_HEAD_DIM, M_HEAD_DIM), F32),
                        pltpu.VMEM((M_HEADS, M_HEAD_DIM), F32),
                        pltpu.VMEM((M_HEADS, LANES), F32),
                        pltpu.VMEM((chunk + SUBLANES, width), F32),
                        pltpu.VMEM((chunk + SUBLANES, width), F32)],
        compiler_params=_cparams(("parallel", "arbitrary")),
        name="mlstm",
    )(proj, proj, proj, proj, gates, gate_bias, conv_w, norm_g)


def _attn_kernel(sc_ref, q_ref, k_ref, v_ref, bias_ref, ng_ref, y_ref,
                 m_scr, l_scr, acc_scr, *, blk, out_scale):
    t = blk
    d = DA_HEAD_DIM
    h = pl.program_id(1)
    i = pl.program_id(2)
    lam = sc_ref[0]
    far_bias = sc_ref[1 + h]
    qk_scale = d ** -0.5

    m_scr[...] = jnp.full(m_scr.shape, NEG_BIG, F32)
    l_scr[...] = jnp.zeros_like(l_scr)
    acc_scr[...] = jnp.zeros_like(acc_scr)
    q = q_ref[...]

    def block_step(j, bias):
        start = pl.multiple_of(j * t, t)
        kb = k_ref[pl.ds(start, t), :]
        vb = v_ref[pl.ds(start, t), :]
        for c in range(2):
            s = lax.dot_general(q[:, c * d:(c + 1) * d], kb[:, c * d:(c + 1) * d],
                                (((1,), (1,)), ((), ())), preferred_element_type=F32)
            s = s * qk_scale + bias
            m_old = m_scr[c]
            m_new = jnp.maximum(m_old, jnp.max(s, axis=-1, keepdims=True))
            p = jnp.exp(s - m_new)
            alpha = jnp.exp(m_old - m_new)
            l_scr[c] = alpha * l_scr[c] + jnp.sum(p, axis=-1, keepdims=True)
            acc_scr[c] = alpha * acc_scr[c] + jnp.dot(
                p.astype(BF16), vb, preferred_element_type=F32)
            m_scr[c] = m_new

    def far_body(j, carry):
        block_step(j, far_bias)
        return carry

    lax.fori_loop(0, jnp.maximum(i - 1, 0), far_body, 0)

    @pl.when(i >= 1)
    def _():
        block_step(i - 1, bias_ref[0, 1])

    block_step(i, bias_ref[0, 0])

    out = acc_scr[0] / l_scr[0] - lam * (acc_scr[1] / l_scr[1])
    y_ref[...] = (_rms(out, ng_ref[...]) * out_scale).astype(BF16)


def _diff_attn(scalars, proj, bias_tiles, norm_g, *, batch, seq, blk, out_scale):
    nq = seq // blk
    kvw = DA_V_DIM
    return pl.pallas_call(
        functools.partial(_attn_kernel, blk=blk, out_scale=out_scale),
        out_shape=jax.ShapeDtypeStruct((batch * seq, DA_HEADS * DA_V_DIM), BF16),
        grid=(batch, DA_HEADS, nq),
        in_specs=[pl.BlockSpec(memory_space=pltpu.SMEM),
                  pl.BlockSpec((blk, kvw), lambda b, h, i: (b * nq + i, COL_QD * 4 + h)),
                  pl.BlockSpec((seq, kvw), lambda b, h, i: (b, COL_KD * 4 + h)),
                  pl.BlockSpec((seq, kvw), lambda b, h, i: (b, COL_VD * 4 + h)),
                  pl.BlockSpec((1, 2, blk, blk), lambda b, h, i: (h, 0, 0, 0)),
                  pl.BlockSpec((1, kvw), lambda b, h, i: (0, h))],
        out_specs=pl.BlockSpec((blk, kvw), lambda b, h, i: (b * nq + i, h)),
        scratch_shapes=[pltpu.VMEM((2, blk, 1), F32),
                        pltpu.VMEM((2, blk, 1), F32),
                        pltpu.VMEM((2, blk, DA_V_DIM), F32)],
        compiler_params=_cparams(("parallel", "parallel", "arbitrary")),
        name="diff_attn",
    )(scalars, proj, proj, proj, bias_tiles, norm_g)


def _s5_kernel(u_ref, wb_ref, wc_ref, are_ref, aim_ref, d_ref, wg_ref, y_ref,
               xre_scr, xim_scr, bu_scr, ys_scr, *, steps, batch):
    rows = steps * batch
    ns = S5_SLAB_STATES

    @pl.when(pl.program_id(0) == 0)
    def _():
        xre_scr[...] = jnp.zeros_like(xre_scr)
        xim_scr[...] = jnp.zeros_like(xim_scr)

    for s in range(S5_SLABS):
        lsl = slice(s * LANES, (s + 1) * LANES)
        bu_scr[...] = jnp.dot(u_ref[:, lsl], wb_ref[s], preferred_element_type=F32)
        a_re = jnp.broadcast_to(are_ref[s], (batch, ns))
        a_im = jnp.broadcast_to(aim_ref[s], (batch, ns))

        def step(tt, carry):
            x_re, x_im = carry
            r0 = pl.multiple_of(tt * batch, batch)
            b_re = bu_scr[pl.ds(r0, batch), 0:ns]
            b_im = bu_scr[pl.ds(r0, batch), ns:2 * ns]
            n_re = a_re * x_re - a_im * x_im + b_re
            n_im = a_re * x_im + a_im * x_re + b_im
            bu_scr[pl.ds(r0, batch), 0:ns] = n_re
            bu_scr[pl.ds(r0, batch), ns:2 * ns] = n_im
            return n_re, n_im

        x_re, x_im = lax.fori_loop(0, steps, step, (xre_scr[s], xim_scr[s]))
        xre_scr[s] = x_re
        xim_scr[s] = x_im
        ys_scr[:, lsl] = jnp.dot(bu_scr[...].astype(BF16), wc_ref[s],
                                 preferred_element_type=F32)

    y = ys_scr[...] + d_ref[...] * u_ref[...].astype(F32)
    yb = _gelu_tanh(y).astype(BF16)
    half = BRANCH_WIDTH
    a = jnp.dot(yb, wg_ref[:, 0:half], preferred_element_type=F32)
    g = jnp.dot(yb, wg_ref[:, half:2 * half], preferred_element_type=F32)
    y_ref[...] = (a * _sigmoid(g)).astype(BF16)


def _s5(u_tb, wb, wc, a_re, a_im, d_skip, w_glu, *, batch, seq, steps):
    rows = steps * batch
    ns = S5_SLAB_STATES
    const3 = lambda i: (0, 0, 0)
    return pl.pallas_call(
        functools.partial(_s5_kernel, steps=steps, batch=batch),
        out_shape=jax.ShapeDtypeStruct((seq * batch, BRANCH_WIDTH), BF16),
        grid=(seq // steps,),
        in_specs=[pl.BlockSpec((rows, BRANCH_WIDTH), lambda i: (i, 0)),
                  pl.BlockSpec((S5_SLABS, LANES, 2 * ns), const3),
                  pl.BlockSpec((S5_SLABS, 2 * ns, LANES), const3),
                  pl.BlockSpec((S5_SLABS, 1, ns), const3),
                  pl.BlockSpec((S5_SLABS, 1, ns), const3),
                  pl.BlockSpec((1, BRANCH_WIDTH), lambda i: (0, 0)),
                  pl.BlockSpec((BRANCH_WIDTH, 2 * BRANCH_WIDTH), lambda i: (0, 0))],
        out_specs=pl.BlockSpec((rows, BRANCH_WIDTH), lambda i: (i, 0)),
        scratch_shapes=[pltpu.VMEM((S5_SLABS, batch, ns), F32),
                        pltpu.VMEM((S5_SLABS, batch, ns), F32),
                        pltpu.VMEM((rows, 2 * ns), F32),
                        pltpu.VMEM((rows, BRANCH_WIDTH), F32)],
        compiler_params=_cparams(("arbitrary",)),
        name="s5",
    )(u_tb, wb, wc, a_re, a_im, d_skip, w_glu)


def _merge_kernel(ya_ref, yb_ref, yc_ref, g0_ref, g1_ref, g2_ref, x_ref, wbr_ref,
                  wo_ref, ng_ref, o_ref):
    merged = None
    for n, (y_ref, g_ref) in enumerate(((ya_ref, g0_ref), (yb_ref, g1_ref),
                                        (yc_ref, g2_ref))):
        z = jnp.dot(y_ref[...], wbr_ref[n], preferred_element_type=F32)
        term = _sigmoid(g_ref[...].astype(F32)) * z
        merged = term if merged is None else merged + term
    mix = jnp.dot(merged.astype(BF16), wo_ref[...], preferred_element_type=F32)
    o_ref[...] = x_ref[...] + _rms(mix, ng_ref[...])


def _merge(y_a, y_b, y_c, proj, x2d, w_branch, w_out, norm_g, *, tm):
    m = x2d.shape[0]
    row = lambda i: (i, 0)

    def gate_spec(n):
        return pl.BlockSpec((tm, D_MODEL), lambda i: (i, COL_GATE // 2 + n))

    return pl.pallas_call(
        _merge_kernel,
        out_shape=jax.ShapeDtypeStruct((m, D_MODEL), F32),
        grid=(m // tm,),
        in_specs=[pl.BlockSpec((tm, BRANCH_WIDTH), row),
                  pl.BlockSpec((tm, BRANCH_WIDTH), row),
                  pl.BlockSpec((tm, BRANCH_WIDTH), row),
                  gate_spec(0), gate_spec(1), gate_spec(2),
                  pl.BlockSpec((tm, D_MODEL), row),
                  pl.BlockSpec((N_BRANCH, BRANCH_WIDTH, D_MODEL), lambda i: (0, 0, 0),
                               pipeline_mode=pl.Buffered(1)),
                  pl.BlockSpec((D_MODEL, D_MODEL), lambda i: (0, 0),
                               pipeline_mode=pl.Buffered(1)),
                  pl.BlockSpec((1, D_MODEL), lambda i: (0, 0))],
        out_specs=pl.BlockSpec((tm, D_MODEL), row),
        compiler_params=_cparams(("parallel",)),
        name="merge",
    )(y_a, y_b, y_c, proj, proj, proj, x2d, w_branch, w_out, norm_g)


def _ffn_kernel(x_ref, xh_ref, g_ref, wa_ref, wv_ref, cwa_ref, cwv_ref, cba_ref,
                cbv_ref, wd_ref, ng_ref, o_ref, h_scr, *, tl):
    i = pl.program_id(1)
    f = pl.program_id(2)
    nf = pl.num_programs(2)
    hal = SUBLANES

    @pl.when(f == 0)
    def _():
        halo = _rms(xh_ref[0], g_ref[...])
        halo = jnp.where(i == 0, 0.0, halo)
        h_scr[0:hal, :] = halo.astype(BF16)
        h_scr[hal:hal + tl, :] = _rms(x_ref[0], g_ref[...]).astype(BF16)
        o_ref[0] = jnp.zeros((tl, D_MODEL), F32)

    hb = h_scr[...]

    def conv(w_ref, cw_ref, cb_ref):
        up = jnp.dot(hb, w_ref[...], preferred_element_type=F32)
        out = cb_ref[...] + cw_ref[FFN_CONV - 1:FFN_CONV, :] * up[hal:hal + tl, :]
        for j in range(FFN_CONV - 1):
            off = hal - (FFN_CONV - 1) + j
            out = out + cw_ref[j:j + 1, :] * up[off:off + tl, :]
        return out

    a = conv(wa_ref, cwa_ref, cba_ref)
    v = conv(wv_ref, cwv_ref, cbv_ref)
    act = (_gelu_tanh(a) * v).astype(BF16)
    o_ref[0] += jnp.dot(act, wd_ref[...], preferred_element_type=F32)

    @pl.when(f == nf - 1)
    def _():
        o_ref[0] = x_ref[0] + _rms(o_ref[0], ng_ref[...])


def _ffn(x3d, gain, w_up, conv_w, conv_b, w_down, norm_g, *, tl, tf):
    batch, seq, _ = x3d.shape
    nfb = D_FF // tf
    hal = SUBLANES
    return pl.pallas_call(
        functools.partial(_ffn_kernel, tl=tl),
        out_shape=jax.ShapeDtypeStruct((batch, seq, D_MODEL), F32),
        grid=(batch, seq // tl, nfb),
        in_specs=[pl.BlockSpec((1, tl, D_MODEL), lambda b, i, f: (b, i, 0)),
                  pl.BlockSpec((1, hal, D_MODEL),
                               lambda b, i, f: (b, jnp.maximum(i * (tl // hal) - 1, 0), 0)),
                  pl.BlockSpec((1, D_MODEL), lambda b, i, f: (0, 0)),
                  pl.BlockSpec((D_MODEL, tf), lambda b, i, f: (0, f)),
                  pl.BlockSpec((D_MODEL, tf), lambda b, i, f: (0, nfb + f)),
                  pl.BlockSpec((FFN_CONV, tf), lambda b, i, f: (0, f)),
                  pl.BlockSpec((FFN_CONV, tf), lambda b, i, f: (0, nfb + f)),
                  pl.BlockSpec((1, tf), lambda b, i, f: (0, f)),
                  pl.BlockSpec((1, tf), lambda b, i, f: (0, nfb + f)),
                  pl.BlockSpec((tf, D_MODEL), lambda b, i, f: (f, 0)),
                  pl.BlockSpec((1, D_MODEL), lambda b, i, f: (0, 0))],
        out_specs=pl.BlockSpec((1, tl, D_MODEL), lambda b, i, f: (b, i, 0)),
        scratch_shapes=[pltpu.VMEM((tl + hal, D_MODEL), BF16)],
        compiler_params=_cparams(("parallel", "parallel", "arbitrary")),
        name="conv_ffn",
    )(x3d, x3d, gain, w_up, w_up, conv_w, conv_w, conv_b, conv_b, w_down, norm_g)


def _t5_bucket(dist):
    n = jnp.maximum(dist, 0)
    max_exact = REL_BUCKETS // 2
    nf = jnp.maximum(n, 1).astype(F32)
    large = max_exact + (jnp.log(nf / max_exact) / math.log(REL_MAX_DIST / max_exact)
                         * (REL_BUCKETS - max_exact)).astype(jnp.int32)
    large = jnp.minimum(large, REL_BUCKETS - 1)
    return jnp.where(n < max_exact, n, large)


def _attn_bias_tiles(rel_bias, blk):
    qi = jnp.arange(blk, dtype=jnp.int32)[:, None]
    kj = jnp.arange(blk, dtype=jnp.int32)[None, :]
    table = rel_bias.astype(F32)
    tiles = []
    for off in (0, blk):
        dist = qi - kj + off
        bias = jnp.transpose(table[_t5_bucket(dist)], (2, 0, 1))
        tiles.append(jnp.where(dist >= 0, bias, NEG_BIG))
    return jnp.stack(tiles, axis=1)


def _s5_params(lam_re, lam_im, log_dt, b_re, b_im, c_re, c_im):
    dt = jnp.exp(log_dt)[:, None]
    mag = jnp.exp(lam_re * dt)
    a_re = mag * jnp.cos(lam_im * dt)
    a_im = mag * jnp.sin(lam_im * dt)
    den = lam_re * lam_re + lam_im * lam_im
    z_re = ((a_re - 1.0) * lam_re + a_im * lam_im) / den
    z_im = (a_im * lam_re - (a_re - 1.0) * lam_im) / den
    bb_re = z_re[..., None] * b_re - z_im[..., None] * b_im
    bb_im = z_re[..., None] * b_im + z_im[..., None] * b_re
    gs = S5_GROUPS // S5_SLABS
    eye = jnp.eye(gs, dtype=F32)

    def in_blocks(bb):
        bb = bb.reshape(S5_SLABS, gs, S5_STATE, S5_GROUP)
        w = jnp.einsum('sgpc,gh->sgchp', bb, eye)
        return w.reshape(S5_SLABS, gs * S5_GROUP, gs * S5_STATE)

    def out_blocks(cc):
        cc = cc.reshape(S5_SLABS, gs, S5_GROUP, S5_STATE)
        w = jnp.einsum('sgcp,gh->sgphc', cc, eye)
        return w.reshape(S5_SLABS, gs * S5_STATE, gs * S5_GROUP)

    wb = jnp.concatenate([in_blocks(bb_re), in_blocks(bb_im)], axis=-1).astype(BF16)
    wc = jnp.concatenate([out_blocks(c_re), out_blocks(-c_im)], axis=-2).astype(BF16)
    a_re = a_re.reshape(S5_SLABS, 1, S5_SLAB_STATES)
    a_im = a_im.reshape(S5_SLABS, 1, S5_SLAB_STATES)
    return wb, wc, a_re, a_im


IN_PROJ_TM, IN_PROJ_TN = 1024, 1024
MLSTM_CHUNK = 128
ATTN_BLOCK = 128
S5_STEPS = 32
MERGE_TM = 256
FFN_TL, FFN_TF = 512, 512


def _layer(x2d, batch, seq, layer, p):
    w_in = p['w_in'][layer]
    n_if = 2 * M_HEADS
    split = 4 * M_HEADS * M_HEAD_DIM
    w_main = jnp.concatenate([w_in[:, :split], w_in[:, split + n_if:]], axis=1).astype(BF16)
    w_gate = jnp.pad(w_in[:, split:split + n_if], ((0, 0), (0, N_GATE_PAD - n_if)))
    proj, gates = _in_proj(x2d, p['norm_mix_pre'][layer][None, :], w_gate, w_main,
                           tm=min(IN_PROJ_TM, batch * seq), tn=IN_PROJ_TN)

    gate_bias = jnp.pad(p['mlstm_b_if'][layer].reshape(1, n_if),
                        ((0, 0), (0, N_GATE_PAD - n_if)))
    y_a = _mlstm(proj, gates, gate_bias, p['mlstm_conv'][layer],
                 p['mlstm_norm'][layer][None, :], batch=batch, seq=seq,
                 chunk=min(MLSTM_CHUNK, seq))

    lambda_init = 0.8 - 0.6 * math.exp(-0.3 * layer)
    lam = p['diff_lambda'][layer]
    lam_full = (jnp.exp(jnp.sum(lam[0] * lam[1])) - jnp.exp(jnp.sum(lam[2] * lam[3]))
                + lambda_init)
    blk = min(ATTN_BLOCK, seq)
    scalars = jnp.concatenate([lam_full[None], p['rel_bias'][REL_BUCKETS - 1, :]]).astype(F32)
    y_b = _diff_attn(scalars, proj, _attn_bias_tiles(p['rel_bias'], blk),
                     p['diff_norm'][layer][None, :], batch=batch, seq=seq, blk=blk,
                     out_scale=1.0 - lambda_init)

    wb, wc, a_re, a_im = _s5_params(
        p['s5_lambda_re'][layer], p['s5_lambda_im'][layer], p['s5_log_dt'][layer],
        p['s5_b_re'][layer], p['s5_b_im'][layer], p['s5_c_re'][layer], p['s5_c_im'][layer])
    u = proj[:, COL_US * BRANCH_WIDTH:(COL_US + 1) * BRANCH_WIDTH]
    u_tb = u.reshape(batch, seq, BRANCH_WIDTH).transpose(1, 0, 2).reshape(seq * batch, -1)
    y_c_tb = _s5(u_tb, wb, wc, a_re, a_im, p['s5_d'][layer][None, :],
                 p['s5_w_glu'][layer].astype(BF16), batch=batch, seq=seq,
                 steps=min(S5_STEPS, seq))
    y_c = y_c_tb.reshape(seq, batch, BRANCH_WIDTH).transpose(1, 0, 2).reshape(batch * seq, -1)

    x2d = _merge(y_a, y_b, y_c, proj, x2d, p['w_branch'][layer].astype(BF16),
                 p['w_out'][layer].astype(BF16), p['norm_mix_post'][layer][None, :],
                 tm=MERGE_TM)

    x3d = _ffn(x2d.reshape(batch, seq, D_MODEL), p['norm_ffn_pre'][layer][None, :],
               p['w_up'][layer].astype(BF16), p['ffn_conv'][layer],
               p['ffn_conv_b'][layer][None, :], p['w_down'][layer].astype(BF16),
               p['norm_ffn_post'][layer][None, :], tl=min(FFN_TL, seq), tf=FFN_TF)
    return x3d.reshape(batch * seq, D_MODEL)


def kernel(x, norm_mix_pre, norm_mix_post, norm_ffn_pre, norm_ffn_post, w_in, mlstm_b_if, mlstm_conv, mlstm_norm, diff_lambda, diff_norm, rel_bias, s5_lambda_re, s5_lambda_im, s5_log_dt, s5_b_re, s5_b_im, s5_c_re, s5_c_im, s5_d, s5_w_glu, w_branch, w_out, w_up, ffn_conv, ffn_conv_b, w_down):
    p = dict(norm_mix_pre=norm_mix_pre, norm_mix_post=norm_mix_post,
             norm_ffn_pre=norm_ffn_pre, norm_ffn_post=norm_ffn_post, w_in=w_in,
             mlstm_b_if=mlstm_b_if, mlstm_conv=mlstm_conv, mlstm_norm=mlstm_norm,
             diff_lambda=diff_lambda, diff_norm=diff_norm, rel_bias=rel_bias,
             s5_lambda_re=s5_lambda_re, s5_lambda_im=s5_lambda_im, s5_log_dt=s5_log_dt,
             s5_b_re=s5_b_re, s5_b_im=s5_b_im, s5_c_re=s5_c_re, s5_c_im=s5_c_im,
             s5_d=s5_d, s5_w_glu=s5_w_glu, w_branch=w_branch, w_out=w_out, w_up=w_up,
             ffn_conv=ffn_conv, ffn_conv_b=ffn_conv_b, w_down=w_down)
    batch, seq, _ = x.shape
    x2d = x.reshape(batch * seq, D_MODEL)
    for layer in range(DEPTH):
        x2d = _layer(x2d, batch, seq, layer, p)
    return x2d.reshape(batch, seq, D_MODEL)
```

```python
import functools
import math

import jax
import jax.numpy as jnp
from jax import lax
from jax.experimental import pallas as pl
from jax.experimental.pallas import tpu as pltpu

F32 = jnp.float32
BF16 = jnp.bfloat16
HIGHEST = lax.Precision.HIGHEST

D_MODEL = 2048
DEPTH = 2
BRANCH_WIDTH = 1024
N_BRANCH = 3
M_HEADS = 4
M_HEAD_DIM = 256
M_CONV = 4
DA_HEADS = 4
DA_HEAD_DIM = 128
DA_V_DIM = 256
REL_BUCKETS = 32
REL_MAX_DIST = 128
S5_GROUP = 16
S5_GROUPS = 64
S5_STATE = 64
D_FF = 5632
FFN_CONV = 3
EPS = 1e-6

LANES = 128
SUBLANES = 8
VMEM_LIMIT = 56 * 1024 * 1024

N_MAIN = 14336
COL_QM, COL_KM, COL_VM, COL_OM = 0, 1, 2, 3
COL_QD, COL_KD, COL_VD, COL_US = 4, 5, 6, 7
COL_GATE = 8
N_GATE_PAD = LANES

S5_SLABS = 8
S5_SLAB_STATES = 512

NEG_BIG = -1e30
LOG2E = math.log2(math.e)


def _cparams(sem):
    return pltpu.CompilerParams(dimension_semantics=sem, vmem_limit_bytes=VMEM_LIMIT)


def _sigmoid(x):
    return 1.0 / (1.0 + jnp.exp(-x))


def _gelu_tanh(x):
    c = math.sqrt(2.0 / math.pi)
    return 0.5 * x * (1.0 + jnp.tanh(c * (x + 0.044715 * (x * x * x))))


def _rms(x, gain):
    var = jnp.mean(x * x, axis=-1, keepdims=True)
    return x * lax.rsqrt(var + EPS) * gain


def _in_proj_kernel(x_ref, g_ref, wg_ref, w_ref, o_ref, og_ref, h_scr):
    @pl.when(pl.program_id(1) == 0)
    def _():
        hf = _rms(x_ref[...], g_ref[...])
        og_ref[...] = jnp.dot(hf, wg_ref[...], preferred_element_type=F32,
                              precision=HIGHEST)
        h_scr[...] = hf.astype(BF16)

    o_ref[...] = jnp.dot(h_scr[...], w_ref[...],
                         preferred_element_type=F32).astype(BF16)


def _in_proj(x2d, gain, w_gate, w_main, *, tm, tn):
    m = x2d.shape[0]
    return pl.pallas_call(
        _in_proj_kernel,
        out_shape=(jax.ShapeDtypeStruct((m, N_MAIN), BF16),
                   jax.ShapeDtypeStruct((m, N_GATE_PAD), F32)),
        grid=(m // tm, N_MAIN // tn),
        in_specs=[pl.BlockSpec((tm, D_MODEL), lambda i, n: (i, 0)),
                  pl.BlockSpec((1, D_MODEL), lambda i, n: (0, 0)),
                  pl.BlockSpec((D_MODEL, N_GATE_PAD), lambda i, n: (0, 0)),
                  pl.BlockSpec((D_MODEL, tn), lambda i, n: (0, n))],
        out_specs=(pl.BlockSpec((tm, tn), lambda i, n: (i, n)),
                   pl.BlockSpec((tm, N_GATE_PAD), lambda i, n: (i, 0))),
        scratch_shapes=[pltpu.VMEM((tm, D_MODEL), BF16)],
        compiler_params=_cparams(("parallel", "arbitrary")),
        name="in_proj",
    )(x2d, gain, w_gate, w_main)


def _mlstm_kernel(q_ref, k_ref, v_ref, o_ref, gt_ref, gb_ref, cw_ref, ng_ref, y_ref,
                  c_scr, n_scr, m_scr, qe_scr, ke_scr, *, chunk):
    t = chunk
    hd = M_HEAD_DIM
    width = M_HEADS * hd

    @pl.when(pl.program_id(1) == 0)
    def _():
        c_scr[...] = jnp.zeros_like(c_scr)
        n_scr[...] = jnp.zeros_like(n_scr)
        m_scr[...] = jnp.zeros_like(m_scr)
        qe_scr[0:SUBLANES, :] = jnp.zeros((SUBLANES, width), F32)
        ke_scr[0:SUBLANES, :] = jnp.zeros((SUBLANES, width), F32)

    qe_scr[SUBLANES:SUBLANES + t, :] = q_ref[...].astype(F32)
    ke_scr[SUBLANES:SUBLANES + t, :] = k_ref[...].astype(F32)

    gates = gt_ref[...] + gb_ref[...]
    log_f = jnp.minimum(gates, 0.0) - jnp.log1p(jnp.exp(-jnp.abs(gates)))
    row = lax.broadcasted_iota(jnp.int32, (t, t), 0)
    col = lax.broadcasted_iota(jnp.int32, (t, t), 1)
    causal = col <= row
    cum = jnp.dot(causal.astype(F32), log_f, preferred_element_type=F32,
                  precision=HIGHEST)
    gates_t = gates.T
    cum_t = cum.T

    for h in range(M_HEADS):
        sl = slice(h * hd, (h + 1) * hd)
        ksl = slice(width + h * hd, width + (h + 1) * hd)
        qc = jnp.zeros((t, hd), F32)
        kc = jnp.zeros((t, hd), F32)
        for j in range(M_CONV):
            off = SUBLANES - (M_CONV - 1) + j
            qc = qc + cw_ref[j:j + 1, sl] * qe_scr[off:off + t, sl]
            kc = kc + cw_ref[j:j + 1, ksl] * ke_scr[off:off + t, sl]
        qc = qc * _sigmoid(qc)
        kc = kc * _sigmoid(kc) * (hd ** -0.5)
        qb = qc.astype(BF16)
        kb = kc.astype(BF16)
        vb = v_ref[:, sl]

        li_row = gates_t[h:h + 1, :]
        b_row = cum_t[M_HEADS + h:M_HEADS + h + 1, :]
        li_col = gates[:, h:h + 1]
        b_col = cum[:, M_HEADS + h:M_HEADS + h + 1]
        m_prev = m_scr[h:h + 1, 0:1]
        c_prev = c_scr[h]
        n_prev = n_scr[h:h + 1, :]

        dmat = jnp.where(causal, b_col - b_row + li_row, -jnp.inf)
        inter = b_col + m_prev
        m_t = jnp.maximum(inter, jnp.max(dmat, axis=-1, keepdims=True))
        s = lax.dot_general(qb, kb, (((1,), (1,)), ((), ())),
                            preferred_element_type=F32) * jnp.exp(dmat - m_t)
        w_inter = jnp.exp(inter - m_t)
        num = (jnp.dot(s.astype(BF16), vb, preferred_element_type=F32)
               + w_inter * jnp.dot(qb, c_prev.astype(BF16), preferred_element_type=F32))
        den = (jnp.sum(s, axis=-1, keepdims=True)
               + w_inter * jnp.sum(qc * n_prev, axis=-1, keepdims=True))
        hh = num / jnp.maximum(jnp.abs(den), jnp.exp(-m_t))

        g = cum[t - 1:t, M_HEADS + h:M_HEADS + h + 1]
        a_col = g - b_col + li_col
        m_new = jnp.maximum(g + m_prev, jnp.max(a_col, axis=0, keepdims=True))
        ws = jnp.exp(a_col - m_new)
        decay = jnp.exp(g + m_prev - m_new)
        kw = ws * kc
        c_scr[h] = decay * c_prev + lax.dot_general(
            kw.astype(BF16), vb, (((0,), (0,)), ((), ())), preferred_element_type=F32)
        n_scr[h:h + 1, :] = decay * n_prev + jnp.sum(kw, axis=0, keepdims=True)
        m_scr[h:h + 1, :] = jnp.broadcast_to(m_new, (1, LANES))

        hn = _rms(hh, ng_ref[:, sl])
        y_ref[:, sl] = (_sigmoid(o_ref[:, sl].astype(F32)) * hn).astype(BF16)

    qe_scr[0:SUBLANES, :] = qe_scr[t:t + SUBLANES, :]
    ke_scr[0:SUBLANES, :] = ke_scr[t:t + SUBLANES, :]


def _mlstm(proj, gates, gate_bias, conv_w, norm_g, *, batch, seq, chunk):
    nc = seq // chunk
    width = M_HEADS * M_HEAD_DIM

    def col_spec(cb):
        return pl.BlockSpec((chunk, width), lambda b, c: (b * nc + c, cb))

    return pl.pallas_call(
        functools.partial(_mlstm_kernel, chunk=chunk),
        out_shape=jax.ShapeDtypeStruct((batch * seq, width), BF16),
        grid=(batch, nc),
        in_specs=[col_spec(COL_QM), col_spec(COL_KM), col_spec(COL_VM), col_spec(COL_OM),
                  pl.BlockSpec((chunk, N_GATE_PAD), lambda b, c: (b * nc + c, 0)),
                  pl.BlockSpec((1, N_GATE_PAD), lambda b, c: (0, 0)),
                  pl.BlockSpec((M_CONV, 2 * width), lambda b, c: (0, 0)),
                  pl.BlockSpec((1, width), lambda b, c: (0, 0))],
        out_specs=pl.BlockSpec((chunk, width), lambda b, c: (b * nc + c, 0)),
        scratch_shapes=[pltpu.VMEM((M_HEADS, M_HEAD_DIM, M_HEAD_DIM), F32),
                        pltpu.VMEM((M_HEADS, M_HEAD_DIM), F32),
                        pltpu.VMEM((M_HEADS, LANES), F32),
                        pltpu.VMEM((chunk + SUBLANES, width), F32),
                        pltpu.VMEM((chunk + SUBLANES, width), F32)],
        compiler_params=_cparams(("parallel", "arbitrary")),
        name="mlstm",
    )(proj, proj, proj, proj, gates, gate_bias, conv_w, norm_g)


def _attn_kernel(sc_ref, q_ref, k_ref, v_ref, bias_ref, ng_ref, y_ref,
                 m_scr, l_scr, acc_scr, *, blk, out_scale):
    t = blk
    d = DA_HEAD_DIM
    h = pl.program_id(1)
    i = pl.program_id(2)
    lam = sc_ref[0]
    far_bias = sc_ref[1 + h]

    m_scr[...] = jnp.full(m_scr.shape, NEG_BIG, F32)
    l_scr[...] = jnp.zeros_like(l_scr)
    acc_scr[...] = jnp.zeros_like(acc_scr)
    qs = (q_ref[...].astype(F32) * (d ** -0.5 * LOG2E)).astype(BF16)

    def block_step(j, bias, far):
        start = pl.multiple_of(j * t, t)
        kb = k_ref[pl.ds(start, t), :]
        vb = v_ref[pl.ds(start, t), :]
        for c in range(2):
            s = lax.dot_general(kb[:, c * d:(c + 1) * d], qs[:, c * d:(c + 1) * d],
                                (((1,), (1,)), ((), ())), preferred_element_type=F32)
            m_old = m_scr[c]
            if far:
                m_new = jnp.maximum(m_old, jnp.max(s, axis=0, keepdims=True) + bias)
                p = jnp.exp2(s - (m_new - bias))
            else:
                s = s + bias
                m_new = jnp.maximum(m_old, jnp.max(s, axis=0, keepdims=True))
                p = jnp.exp2(s - m_new)
            alpha = jnp.exp2(m_old - m_new)
            l_scr[c] = alpha * l_scr[c] + jnp.sum(p, axis=0, keepdims=True)
            acc_scr[c] = alpha * acc_scr[c] + lax.dot_general(
                vb, p.astype(BF16), (((0,), (0,)), ((), ())), preferred_element_type=F32)
            m_scr[c] = m_new

    def far_body(j, carry):
        block_step(j, far_bias, True)
        return carry

    lax.fori_loop(0, jnp.maximum(i - 1, 0), far_body, 0)

    @pl.when(i >= 1)
    def _():
        block_step(i - 1, bias_ref[0, 1], False)

    block_step(i, bias_ref[0, 0], False)

    out_t = acc_scr[0] * (1.0 / l_scr[0]) - lam * (acc_scr[1] * (1.0 / l_scr[1]))
    y_ref[...] = (_rms(out_t.T, ng_ref[...]) * out_scale).astype(BF16)


def _diff_attn(scalars, proj, bias_tiles, norm_g, *, batch, seq, blk, out_scale):
    nq = seq // blk
    kvw = DA_V_DIM
    return pl.pallas_call(
        functools.partial(_attn_kernel, blk=blk, out_scale=out_scale),
        out_shape=jax.ShapeDtypeStruct((batch * seq, DA_HEADS * DA_V_DIM), BF16),
        grid=(batch, DA_HEADS, nq),
        in_specs=[pl.BlockSpec(memory_space=pltpu.SMEM),
                  pl.BlockSpec((blk, kvw), lambda b, h, i: (b * nq + i, COL_QD * 4 + h)),
                  pl.BlockSpec((seq, kvw), lambda b, h, i: (b, COL_KD * 4 + h)),
                  pl.BlockSpec((seq, kvw), lambda b, h, i: (b, COL_VD * 4 + h)),
                  pl.BlockSpec((1, 2, blk, blk), lambda b, h, i: (h, 0, 0, 0)),
                  pl.BlockSpec((1, kvw), lambda b, h, i: (0, h))],
        out_specs=pl.BlockSpec((blk, kvw), lambda b, h, i: (b * nq + i, h)),
        scratch_shapes=[pltpu.VMEM((2, 1, blk), F32),
                        pltpu.VMEM((2, 1, blk), F32),
                        pltpu.VMEM((2, DA_V_DIM, blk), F32)],
        compiler_params=_cparams(("parallel", "parallel", "arbitrary")),
        name="diff_attn",
    )(scalars, proj, proj, proj, bias_tiles, norm_g)


def _s5_kernel(u_ref, wb_ref, wc_ref, are_ref, aim_ref, d_ref, wg_ref, y_ref,
               xre_scr, xim_scr, bu_scr, ys_scr, *, steps, batch):
    rows = steps * batch
    ns = S5_SLAB_STATES

    @pl.when(pl.program_id(0) == 0)
    def _():
        xre_scr[...] = jnp.zeros_like(xre_scr)
        xim_scr[...] = jnp.zeros_like(xim_scr)

    for s in range(S5_SLABS):
        lsl = slice(s * LANES, (s + 1) * LANES)
        bu_scr[...] = jnp.dot(u_ref[:, lsl], wb_ref[s], preferred_element_type=F32)
        a_re = jnp.broadcast_to(are_ref[s], (batch, ns))
        a_im = jnp.broadcast_to(aim_ref[s], (batch, ns))

        def step(tt, carry):
            x_re, x_im = carry
            r0 = pl.multiple_of(tt * batch, batch)
            b_re = bu_scr[pl.ds(r0, batch), 0:ns]
            b_im = bu_scr[pl.ds(r0, batch), ns:2 * ns]
            n_re = a_re * x_re - a_im * x_im + b_re
            n_im = a_re * x_im + a_im * x_re + b_im
            bu_scr[pl.ds(r0, batch), 0:ns] = n_re
            bu_scr[pl.ds(r0, batch), ns:2 * ns] = n_im
            return n_re, n_im

        x_re, x_im = lax.fori_loop(0, steps, step, (xre_scr[s], xim_scr[s]))
        xre_scr[s] = x_re
        xim_scr[s] = x_im
        ys_scr[:, lsl] = jnp.dot(bu_scr[...].astype(BF16), wc_ref[s],
                                 preferred_element_type=F32)

    y = ys_scr[...] + d_ref[...] * u_ref[...].astype(F32)
    yb = _gelu_tanh(y).astype(BF16)
    half = BRANCH_WIDTH
    a = jnp.dot(yb, wg_ref[:, 0:half], preferred_element_type=F32)
    g = jnp.dot(yb, wg_ref[:, half:2 * half], preferred_element_type=F32)
    y_ref[...] = (a * _sigmoid(g)).astype(BF16)


def _s5(u_tb, wb, wc, a_re, a_im, d_skip, w_glu, *, batch, seq, steps):
    rows = steps * batch
    ns = S5_SLAB_STATES
    const3 = lambda i: (0, 0, 0)
    return pl.pallas_call(
        functools.partial(_s5_kernel, steps=steps, batch=batch),
        out_shape=jax.ShapeDtypeStruct((seq * batch, BRANCH_WIDTH), BF16),
        grid=(seq // steps,),
        in_specs=[pl.BlockSpec((rows, BRANCH_WIDTH), lambda i: (i, 0)),
                  pl.BlockSpec((S5_SLABS, LANES, 2 * ns), const3),
                  pl.BlockSpec((S5_SLABS, 2 * ns, LANES), const3),
                  pl.BlockSpec((S5_SLABS, 1, ns), const3),
                  pl.BlockSpec((S5_SLABS, 1, ns), const3),
                  pl.BlockSpec((1, BRANCH_WIDTH), lambda i: (0, 0)),
                  pl.BlockSpec((BRANCH_WIDTH, 2 * BRANCH_WIDTH), lambda i: (0, 0))],
        out_specs=pl.BlockSpec((rows, BRANCH_WIDTH), lambda i: (i, 0)),
        scratch_shapes=[pltpu.VMEM((S5_SLABS, batch, ns), F32),
                        pltpu.VMEM((S5_SLABS, batch, ns), F32),
                        pltpu.VMEM((rows, 2 * ns), F32),
                        pltpu.VMEM((rows, BRANCH_WIDTH), F32)],
        compiler_params=_cparams(("arbitrary",)),
        name="s5",
    )(u_tb, wb, wc, a_re, a_im, d_skip, w_glu)


def _merge_kernel(ya_ref, yb_ref, yc_ref, g0_ref, g1_ref, g2_ref, x_ref, wbr_ref,
                  wo_ref, ng_ref, o_ref):
    merged = None
    for n, (y_ref, g_ref) in enumerate(((ya_ref, g0_ref), (yb_ref, g1_ref),
                                        (yc_ref, g2_ref))):
        z = jnp.dot(y_ref[...], wbr_ref[n], preferred_element_type=F32)
        term = _sigmoid(g_ref[...].astype(F32)) * z
        merged = term if merged is None else merged + term
    mix = jnp.dot(merged.astype(BF16), wo_ref[...], preferred_element_type=F32)
    o_ref[...] = x_ref[...] + _rms(mix, ng_ref[...])


def _merge(y_a, y_b, y_c, proj, x2d, w_branch, w_out, norm_g, *, tm):
    m = x2d.shape[0]
    row = lambda i: (i, 0)

    def gate_spec(n):
        return pl.BlockSpec((tm, D_MODEL), lambda i: (i, COL_GATE // 2 + n))

    return pl.pallas_call(
        _merge_kernel,
        out_shape=jax.ShapeDtypeStruct((m, D_MODEL), F32),
        grid=(m // tm,),
        in_specs=[pl.BlockSpec((tm, BRANCH_WIDTH), row),
                  pl.BlockSpec((tm, BRANCH_WIDTH), row),
                  pl.BlockSpec((tm, BRANCH_WIDTH), row),
                  gate_spec(0), gate_spec(1), gate_spec(2),
                  pl.BlockSpec((tm, D_MODEL), row),
                  pl.BlockSpec((N_BRANCH, BRANCH_WIDTH, D_MODEL), lambda i: (0, 0, 0),
                               pipeline_mode=pl.Buffered(1)),
                  pl.BlockSpec((D_MODEL, D_MODEL), lambda i: (0, 0),
                               pipeline_mode=pl.Buffered(1)),
                  pl.BlockSpec((1, D_MODEL), lambda i: (0, 0))],
        out_specs=pl.BlockSpec((tm, D_MODEL), row),
        compiler_params=_cparams(("parallel",)),
        name="merge",
    )(y_a, y_b, y_c, proj, proj, proj, x2d, w_branch, w_out, norm_g)


def _ffn_kernel(x_ref, xh_ref, g_ref, wa_ref, wv_ref, cwa_ref, cwv_ref, cba_ref,
                cbv_ref, wd_ref, ng_ref, o_ref, h_scr, *, tl):
    i = pl.program_id(1)
    f = pl.program_id(2)
    nf = pl.num_programs(2)
    hal = SUBLANES

    @pl.when(f == 0)
    def _():
        halo = _rms(xh_ref[0], g_ref[...])
        halo = jnp.where(i == 0, 0.0, halo)
        h_scr[0:hal, :] = halo.astype(BF16)
        h_scr[hal:hal + tl, :] = _rms(x_ref[0], g_ref[...]).astype(BF16)
        o_ref[0] = jnp.zeros((tl, D_MODEL), F32)

    hb = h_scr[...]

    def conv(w_ref, cw_ref, cb_ref):
        up = jnp.dot(hb, w_ref[...], preferred_element_type=F32)
        out = cb_ref[...] + cw_ref[FFN_CONV - 1:FFN_CONV, :] * up[hal:hal + tl, :]
        for j in range(FFN_CONV - 1):
            off = hal - (FFN_CONV - 1) + j
            out = out + cw_ref[j:j + 1, :] * up[off:off + tl, :]
        return out

    a = conv(wa_ref, cwa_ref, cba_ref)
    v = conv(wv_ref, cwv_ref, cbv_ref)
    act = (_gelu_tanh(a) * v).astype(BF16)
    o_ref[0] += jnp.dot(act, wd_ref[...], preferred_element_type=F32)

    @pl.when(f == nf - 1)
    def _():
        o_ref[0] = x_ref[0] + _rms(o_ref[0], ng_ref[...])


def _ffn(x3d, gain, w_up, conv_w, conv_b, w_down, norm_g, *, tl, tf):
    batch, seq, _ = x3d.shape
    nfb = D_FF // tf
    hal = SUBLANES
    return pl.pallas_call(
        functools.partial(_ffn_kernel, tl=tl),
        out_shape=jax.ShapeDtypeStruct((batch, seq, D_MODEL), F32),
        grid=(batch, seq // tl, nfb),
        in_specs=[pl.BlockSpec((1, tl, D_MODEL), lambda b, i, f: (b, i, 0)),
                  pl.BlockSpec((1, hal, D_MODEL),
                               lambda b, i, f: (b, jnp.maximum(i * (tl // hal) - 1, 0), 0)),
                  pl.BlockSpec((1, D_MODEL), lambda b, i, f: (0, 0)),
                  pl.BlockSpec((D_MODEL, tf), lambda b, i, f: (0, f)),
                  pl.BlockSpec((D_MODEL, tf), lambda b, i, f: (0, nfb + f)),
                  pl.BlockSpec((FFN_CONV, tf), lambda b, i, f: (0, f)),
                  pl.BlockSpec((FFN_CONV, tf), lambda b, i, f: (0, nfb + f)),
                  pl.BlockSpec((1, tf), lambda b, i, f: (0, f)),
                  pl.BlockSpec((1, tf), lambda b, i, f: (0, nfb + f)),
                  pl.BlockSpec((tf, D_MODEL), lambda b, i, f: (f, 0)),
                  pl.BlockSpec((1, D_MODEL), lambda b, i, f: (0, 0))],
        out_specs=pl.BlockSpec((1, tl, D_MODEL), lambda b, i, f: (b, i, 0)),
        scratch_shapes=[pltpu.VMEM((tl + hal, D_MODEL), BF16)],
        compiler_params=_cparams(("parallel", "parallel", "arbitrary")),
        name="conv_ffn",
    )(x3d, x3d, gain, w_up, w_up, conv_w, conv_w, conv_b, conv_b, w_down, norm_g)


def _t5_bucket(dist):
    n = jnp.maximum(dist, 0)
    max_exact = REL_BUCKETS // 2
    nf = jnp.maximum(n, 1).astype(F32)
    large = max_exact + (jnp.log(nf / max_exact) / math.log(REL_MAX_DIST / max_exact)
                         * (REL_BUCKETS - max_exact)).astype(jnp.int32)
    large = jnp.minimum(large, REL_BUCKETS - 1)
    return jnp.where(n < max_exact, n, large)


def _attn_bias_tiles(rel_bias, blk):
    qi = jnp.arange(blk, dtype=jnp.int32)[None, :]
    kj = jnp.arange(blk, dtype=jnp.int32)[:, None]
    table = rel_bias.astype(F32)
    tiles = []
    for off in (0, blk):
        dist = qi - kj + off
        bias = jnp.transpose(table[_t5_bucket(dist)], (2, 0, 1))
        tiles.append(jnp.where(dist >= 0, bias * LOG2E, NEG_BIG))
    return jnp.stack(tiles, axis=1)


def _s5_params(lam_re, lam_im, log_dt, b_re, b_im, c_re, c_im):
    dt = jnp.exp(log_dt)[:, None]
    mag = jnp.exp(lam_re * dt)
    a_re = mag * jnp.cos(lam_im * dt)
    a_im = mag * jnp.sin(lam_im * dt)
    den = lam_re * lam_re + lam_im * lam_im
    z_re = ((a_re - 1.0) * lam_re + a_im * lam_im) / den
    z_im = (a_im * lam_re - (a_re - 1.0) * lam_im) / den
    bb_re = z_re[..., None] * b_re - z_im[..., None] * b_im
    bb_im = z_re[..., None] * b_im + z_im[..., None] * b_re
    gs = S5_GROUPS // S5_SLABS
    eye = jnp.eye(gs, dtype=F32)

    def in_blocks(bb):
        bb = bb.reshape(S5_SLABS, gs, S5_STATE, S5_GROUP)
        w = jnp.einsum('sgpc,gh->sgchp', bb, eye)
        return w.reshape(S5_SLABS, gs * S5_GROUP, gs * S5_STATE)

    def out_blocks(cc):
        cc = cc.reshape(S5_SLABS, gs, S5_GROUP, S5_STATE)
        w = jnp.einsum('sgcp,gh->sgphc', cc, eye)
        return w.reshape(S5_SLABS, gs * S5_STATE, gs * S5_GROUP)

    wb = jnp.concatenate([in_blocks(bb_re), in_blocks(bb_im)], axis=-1).astype(BF16)
    wc = jnp.concatenate([out_blocks(c_re), out_blocks(-c_im)], axis=-2).astype(BF16)
    a_re = a_re.reshape(S5_SLABS, 1, S5_SLAB_STATES)
    a_im = a_im.reshape(S5_SLABS, 1, S5_SLAB_STATES)
    return wb, wc, a_re, a_im


IN_PROJ_TM, IN_PROJ_TN = 1024, 1024
MLSTM_CHUNK = 128
ATTN_BLOCK = 512
S5_STEPS = 32
MERGE_TM = 256
FFN_TL, FFN_TF = 512, 512


def _layer(x2d, batch, seq, layer, p):
    w_in = p['w_in'][layer]
    n_if = 2 * M_HEADS
    split = 4 * M_HEADS * M_HEAD_DIM
    w_main = jnp.concatenate([w_in[:, :split], w_in[:, split + n_if:]], axis=1).astype(BF16)
    w_gate = jnp.pad(w_in[:, split:split + n_if], ((0, 0), (0, N_GATE_PAD - n_if)))
    proj, gates = _in_proj(x2d, p['norm_mix_pre'][layer][None, :], w_gate, w_main,
                           tm=min(IN_PROJ_TM, batch * seq), tn=IN_PROJ_TN)

    gate_bias = jnp.pad(p['mlstm_b_if'][layer].reshape(1, n_if),
                        ((0, 0), (0, N_GATE_PAD - n_if)))
    y_a = _mlstm(proj, gates, gate_bias, p['mlstm_conv'][layer],
                 p['mlstm_norm'][layer][None, :], batch=batch, seq=seq,
                 chunk=min(MLSTM_CHUNK, seq))

    lambda_init = 0.8 - 0.6 * math.exp(-0.3 * layer)
    lam = p['diff_lambda'][layer]
    lam_full = (jnp.exp(jnp.sum(lam[0] * lam[1])) - jnp.exp(jnp.sum(lam[2] * lam[3]))
                + lambda_init)
    blk = min(ATTN_BLOCK, seq)
    scalars = jnp.concatenate(
        [lam_full[None], p['rel_bias'][REL_BUCKETS - 1, :] * LOG2E]).astype(F32)
    y_b = _diff_attn(scalars, proj, _attn_bias_tiles(p['rel_bias'], blk),
                     p['diff_norm'][layer][None, :], batch=batch, seq=seq, blk=blk,
                     out_scale=1.0 - lambda_init)

    wb, wc, a_re, a_im = _s5_params(
        p['s5_lambda_re'][layer], p['s5_lambda_im'][layer], p['s5_log_dt'][layer],
        p['s5_b_re'][layer], p['s5_b_im'][layer], p['s5_c_re'][layer], p['s5_c_im'][layer])
    u = proj[:, COL_US * BRANCH_WIDTH:(COL_US + 1) * BRANCH_WIDTH]
    u_tb = u.reshape(batch, seq, BRANCH_WIDTH).transpose(1, 0, 2).reshape(seq * batch, -1)
    y_c_tb = _s5(u_tb, wb, wc, a_re, a_im, p['s5_d'][layer][None, :],
                 p['s5_w_glu'][layer].astype(BF16), batch=batch, seq=seq,
                 steps=min(S5_STEPS, seq))
    y_c = y_c_tb.reshape(seq, batch, BRANCH_WIDTH).transpose(1, 0, 2).reshape(batch * seq, -1)

    x2d = _merge(y_a, y_b, y_c, proj, x2d, p['w_branch'][layer].astype(BF16),
                 p['w_out'][layer].astype(BF16), p['norm_mix_post'][layer][None, :],
                 tm=MERGE_TM)

    x3d = _ffn(x2d.reshape(batch, seq, D_MODEL), p['norm_ffn_pre'][layer][None, :],
               p['w_up'][layer].astype(BF16), p['ffn_conv'][layer],
               p['ffn_conv_b'][layer][None, :], p['w_down'][layer].astype(BF16),
               p['norm_ffn_post'][layer][None, :], tl=min(FFN_TL, seq), tf=FFN_TF)
    return x3d.reshape(batch * seq, D_MODEL)


def kernel(x, norm_mix_pre, norm_mix_post, norm_ffn_pre, norm_ffn_post, w_in, mlstm_b_if, mlstm_conv, mlstm_norm, diff_lambda, diff_norm, rel_bias, s5_lambda_re, s5_lambda_im, s5_log_dt, s5_b_re, s5_b_im, s5_c_re, s5_c_im, s5_d, s5_w_glu, w_branch, w_out, w_up, ffn_conv, ffn_conv_b, w_down):
    p = dict(norm_mix_pre=norm_mix_pre, norm_mix_post=norm_mix_post,
             norm_ffn_pre=norm_ffn_pre, norm_ffn_post=norm_ffn_post, w_in=w_in,
             mlstm_b_if=mlstm_b_if, mlstm_conv=mlstm_conv, mlstm_norm=mlstm_norm,
             diff_lambda=diff_lambda, diff_norm=diff_norm, rel_bias=rel_bias,
             s5_lambda_re=s5_lambda_re, s5_lambda_im=s5_lambda_im, s5_log_dt=s5_log_dt,
             s5_b_re=s5_b_re, s5_b_im=s5_b_im, s5_c_re=s5_c_re, s5_c_im=s5_c_im,
             s5_d=s5_d, s5_w_glu=s5_w_glu, w_branch=w_branch, w_out=w_out, w_up=w_up,
             ffn_conv=ffn_conv, ffn_conv_b=ffn_conv_b, w_down=w_down)
    batch, seq, _ = x.shape
    x2d = x.reshape(batch * seq, D_MODEL)
    for layer in range(DEPTH):
        x2d = _layer(x2d, batch, seq, layer, p)
    return x2d.reshape(batch, seq, D_MODEL)
```

```python
import functools
import math

import jax
import jax.numpy as jnp
from jax import lax
from jax.experimental import pallas as pl
from jax.experimental.pallas import tpu as pltpu

F32 = jnp.float32
BF16 = jnp.bfloat16
HIGHEST = lax.Precision.HIGHEST

D_MODEL = 2048
DEPTH = 2
BRANCH_WIDTH = 1024
N_BRANCH = 3
M_HEADS = 4
M_HEAD_DIM = 256
M_CONV = 4
DA_HEADS = 4
DA_HEAD_DIM = 128
DA_V_DIM = 256
REL_BUCKETS = 32
REL_MAX_DIST = 128
S5_GROUP = 16
S5_GROUPS = 64
S5_STATE = 64
D_FF = 5632
FFN_CONV = 3
EPS = 1e-6

LANES = 128
SUBLANES = 8
VMEM_LIMIT = 56 * 1024 * 1024

N_MAIN = 14336
COL_QM, COL_KM, COL_VM, COL_OM = 0, 1, 2, 3
COL_QD, COL_KD, COL_VD, COL_US = 4, 5, 6, 7
COL_GATE = 8
N_GATE_PAD = LANES

S5_SLABS = 8
S5_SLAB_STATES = 512

NEG_BIG = -1e30
LOG2E = math.log2(math.e)


def _cparams(sem):
    return pltpu.CompilerParams(dimension_semantics=sem, vmem_limit_bytes=VMEM_LIMIT)


def _sigmoid(x):
    return 1.0 / (1.0 + jnp.exp(-x))


def _gelu_tanh(x):
    c = math.sqrt(2.0 / math.pi)
    return 0.5 * x * (1.0 + jnp.tanh(c * (x + 0.044715 * (x * x * x))))


def _rms(x, gain):
    var = jnp.mean(x * x, axis=-1, keepdims=True)
    return x * lax.rsqrt(var + EPS) * gain


def _in_proj_kernel(x_ref, g_ref, wg_ref, w_ref, o_ref, og_ref, h_scr):
    @pl.when(pl.program_id(1) == 0)
    def _():
        h_scr[...] = _rms(x_ref[...], g_ref[...]).astype(BF16)
        og_ref[...] = jnp.dot(h_scr[...], wg_ref[...], preferred_element_type=F32)

    o_ref[...] = jnp.dot(h_scr[...], w_ref[...],
                         preferred_element_type=F32).astype(BF16)


def _in_proj(x2d, gain, w_gate, w_main, *, tm, tn):
    m = x2d.shape[0]
    return pl.pallas_call(
        _in_proj_kernel,
        out_shape=(jax.ShapeDtypeStruct((m, N_MAIN), BF16),
                   jax.ShapeDtypeStruct((m, N_GATE_PAD), F32)),
        grid=(m // tm, N_MAIN // tn),
        in_specs=[pl.BlockSpec((tm, D_MODEL), lambda i, n: (i, 0)),
                  pl.BlockSpec((1, D_MODEL), lambda i, n: (0, 0)),
                  pl.BlockSpec((D_MODEL, N_GATE_PAD), lambda i, n: (0, 0)),
                  pl.BlockSpec((D_MODEL, tn), lambda i, n: (0, n))],
        out_specs=(pl.BlockSpec((tm, tn), lambda i, n: (i, n)),
                   pl.BlockSpec((tm, N_GATE_PAD), lambda i, n: (i, 0))),
        scratch_shapes=[pltpu.VMEM((tm, D_MODEL), BF16)],
        compiler_params=_cparams(("parallel", "arbitrary")),
        name="in_proj",
    )(x2d, gain, w_gate, w_main)


def _mlstm_kernel(q_ref, k_ref, v_ref, o_ref, gt_ref, gb_ref, cw_ref, ng_ref, y_ref,
                  c_scr, n_scr, m_scr, qe_scr, ke_scr, *, chunk):
    t = chunk
    hd = M_HEAD_DIM
    width = M_HEADS * hd

    @pl.when(pl.program_id(1) == 0)
    def _():
        c_scr[...] = jnp.zeros_like(c_scr)
        n_scr[...] = jnp.zeros_like(n_scr)
        m_scr[...] = jnp.zeros_like(m_scr)
        qe_scr[0:SUBLANES, :] = jnp.zeros((SUBLANES, width), F32)
        ke_scr[0:SUBLANES, :] = jnp.zeros((SUBLANES, width), F32)

    qe_scr[SUBLANES:SUBLANES + t, :] = q_ref[...].astype(F32)
    ke_scr[SUBLANES:SUBLANES + t, :] = k_ref[...].astype(F32)

    gates = gt_ref[...] + gb_ref[...]
    log_f = jnp.minimum(gates, 0.0) - jnp.log1p(jnp.exp(-jnp.abs(gates)))
    row = lax.broadcasted_iota(jnp.int32, (t, t), 0)
    col = lax.broadcasted_iota(jnp.int32, (t, t), 1)
    causal = col <= row
    cum = jnp.dot(causal.astype(F32), log_f, preferred_element_type=F32,
                  precision=HIGHEST)
    gates_t = gates.T
    cum_t = cum.T

    for h in range(M_HEADS):
        sl = slice(h * hd, (h + 1) * hd)
        ksl = slice(width + h * hd, width + (h + 1) * hd)
        qc = jnp.zeros((t, hd), F32)
        kc = jnp.zeros((t, hd), F32)
        for j in range(M_CONV):
            off = SUBLANES - (M_CONV - 1) + j
            qc = qc + cw_ref[j:j + 1, sl] * qe_scr[off:off + t, sl]
            kc = kc + cw_ref[j:j + 1, ksl] * ke_scr[off:off + t, sl]
        qc = qc * _sigmoid(qc)
        kc = kc * _sigmoid(kc) * (hd ** -0.5)
        qb = qc.astype(BF16)
        kb = kc.astype(BF16)
        vb = v_ref[:, sl]

        li_row = gates_t[h:h + 1, :]
        b_row = cum_t[M_HEADS + h:M_HEADS + h + 1, :]
        li_col = gates[:, h:h + 1]
        b_col = cum[:, M_HEADS + h:M_HEADS + h + 1]
        m_prev = m_scr[h:h + 1, 0:1]
        c_prev = c_scr[h]
        n_prev = n_scr[h:h + 1, :]

        dmat = jnp.where(causal, b_col - b_row + li_row, -jnp.inf)
        inter = b_col + m_prev
        m_t = jnp.maximum(inter, jnp.max(dmat, axis=-1, keepdims=True))
        s = lax.dot_general(qb, kb, (((1,), (1,)), ((), ())),
                            preferred_element_type=F32) * jnp.exp(dmat - m_t)
        w_inter = jnp.exp(inter - m_t)
        num = (jnp.dot(s.astype(BF16), vb, preferred_element_type=F32)
               + w_inter * jnp.dot(qb, c_prev.astype(BF16), preferred_element_type=F32))
        den = (jnp.sum(s, axis=-1, keepdims=True)
               + w_inter * jnp.sum(qc * n_prev, axis=-1, keepdims=True))
        hh = num / jnp.maximum(jnp.abs(den), jnp.exp(-m_t))

        g = cum[t - 1:t, M_HEADS + h:M_HEADS + h + 1]
        a_col = g - b_col + li_col
        m_new = jnp.maximum(g + m_prev, jnp.max(a_col, axis=0, keepdims=True))
        ws = jnp.exp(a_col - m_new)
        decay = jnp.exp(g + m_prev - m_new)
        kw = ws * kc
        c_scr[h] = decay * c_prev + lax.dot_general(
            kw.astype(BF16), vb, (((0,), (0,)), ((), ())), preferred_element_type=F32)
        n_scr[h:h + 1, :] = decay * n_prev + jnp.sum(kw, axis=0, keepdims=True)
        m_scr[h:h + 1, :] = jnp.broadcast_to(m_new, (1, LANES))

        hn = _rms(hh, ng_ref[:, sl])
        y_ref[:, sl] = (_sigmoid(o_ref[:, sl].astype(F32)) * hn).astype(BF16)

    qe_scr[0:SUBLANES, :] = qe_scr[t:t + SUBLANES, :]
    ke_scr[0:SUBLANES, :] = ke_scr[t:t + SUBLANES, :]


def _mlstm(proj, gates, gate_bias, conv_w, norm_g, *, batch, seq, chunk):
    nc = seq // chunk
    width = M_HEADS * M_HEAD_DIM

    def col_spec(cb):
        return pl.BlockSpec((chunk, width), lambda b, c: (b * nc + c, cb))

    return pl.pallas_call(
        functools.partial(_mlstm_kernel, chunk=chunk),
        out_shape=jax.ShapeDtypeStruct((batch * seq, width), BF16),
        grid=(batch, nc),
        in_specs=[col_spec(COL_QM), col_spec(COL_KM), col_spec(COL_VM), col_spec(COL_OM),
                  pl.BlockSpec((chunk, N_GATE_PAD), lambda b, c: (b * nc + c, 0)),
                  pl.BlockSpec((1, N_GATE_PAD), lambda b, c: (0, 0)),
                  pl.BlockSpec((M_CONV, 2 * width), lambda b, c: (0, 0)),
                  pl.BlockSpec((1, width), lambda b, c: (0, 0))],
        out_specs=pl.BlockSpec((chunk, width), lambda b, c: (b * nc + c, 0)),
        scratch_shapes=[pltpu.VMEM((M_HEADS, M_HEAD_DIM, M_HEAD_DIM), F32),
                        pltpu.VMEM((M_HEADS, M_HEAD_DIM), F32),
                        pltpu.VMEM((M_HEADS, LANES), F32),
                        pltpu.VMEM((chunk + SUBLANES, width), F32),
                        pltpu.VMEM((chunk + SUBLANES, width), F32)],
        compiler_params=_cparams(("parallel", "arbitrary")),
        name="mlstm",
    )(proj, proj, proj, proj, gates, gate_bias, conv_w, norm_g)


def _attn_kernel(sc_ref, q_ref, k_ref, v_ref, bias_ref, ng_ref, y_ref,
                 m_scr, l_scr, acc_scr, *, blk, out_scale):
    t = blk
    d = DA_HEAD_DIM
    h = pl.program_id(1)
    i = pl.program_id(2)
    lam = sc_ref[0]
    far_bias = sc_ref[1 + h]

    m_scr[...] = jnp.full(m_scr.shape, NEG_BIG, F32)
    l_scr[...] = jnp.zeros_like(l_scr)
    acc_scr[...] = jnp.zeros_like(acc_scr)
    qs = (q_ref[...].astype(F32) * (d ** -0.5 * LOG2E)).astype(BF16)

    def block_step(j, bias, far):
        start = pl.multiple_of(j * t, t)
        kb = k_ref[pl.ds(start, t), :]
        vb = v_ref[pl.ds(start, t), :]
        for c in range(2):
            s = lax.dot_general(kb[:, c * d:(c + 1) * d], qs[:, c * d:(c + 1) * d],
                                (((1,), (1,)), ((), ())), preferred_element_type=F32)
            m_old = m_scr[c]
            if far:
                m_new = jnp.maximum(m_old, jnp.max(s, axis=0, keepdims=True) + bias)
                p = jnp.exp2(s - (m_new - bias))
            else:
                s = s + bias
                m_new = jnp.maximum(m_old, jnp.max(s, axis=0, keepdims=True))
                p = jnp.exp2(s - m_new)
            alpha = jnp.exp2(m_old - m_new)
            l_scr[c] = alpha * l_scr[c] + jnp.sum(p, axis=0, keepdims=True)
            acc_scr[c] = alpha * acc_scr[c] + lax.dot_general(
                vb, p.astype(BF16), (((0,), (0,)), ((), ())), preferred_element_type=F32)
            m_scr[c] = m_new

    def far_body(j, carry):
        block_step(j, far_bias, True)
        return carry

    lax.fori_loop(0, jnp.maximum(i - 1, 0), far_body, 0)

    @pl.when(i >= 1)
    def _():
        block_step(i - 1, bias_ref[0, 1], False)

    block_step(i, bias_ref[0, 0], False)

    out_t = acc_scr[0] * (1.0 / l_scr[0]) - lam * (acc_scr[1] * (1.0 / l_scr[1]))
    y_ref[...] = (_rms(out_t.T, ng_ref[...]) * out_scale).astype(BF16)


def _diff_attn(scalars, proj, bias_tiles, norm_g, *, batch, seq, blk, out_scale):
    nq = seq // blk
    kvw = DA_V_DIM
    return pl.pallas_call(
        functools.partial(_attn_kernel, blk=blk, out_scale=out_scale),
        out_shape=jax.ShapeDtypeStruct((batch * seq, DA_HEADS * DA_V_DIM), BF16),
        grid=(batch, DA_HEADS, nq),
        in_specs=[pl.BlockSpec(memory_space=pltpu.SMEM),
                  pl.BlockSpec((blk, kvw), lambda b, h, i: (b * nq + i, COL_QD * 4 + h)),
                  pl.BlockSpec((seq, kvw), lambda b, h, i: (b, COL_KD * 4 + h)),
                  pl.BlockSpec((seq, kvw), lambda b, h, i: (b, COL_VD * 4 + h)),
                  pl.BlockSpec((1, 2, blk, blk), lambda b, h, i: (h, 0, 0, 0)),
                  pl.BlockSpec((1, kvw), lambda b, h, i: (0, h))],
        out_specs=pl.BlockSpec((blk, kvw), lambda b, h, i: (b * nq + i, h)),
        scratch_shapes=[pltpu.VMEM((2, 1, blk), F32),
                        pltpu.VMEM((2, 1, blk), F32),
                        pltpu.VMEM((2, DA_V_DIM, blk), F32)],
        compiler_params=_cparams(("parallel", "parallel", "arbitrary")),
        name="diff_attn",
    )(scalars, proj, proj, proj, bias_tiles, norm_g)


def _s5_kernel(u_ref, wb_ref, wc_ref, are_ref, aim_ref, d_ref, wg_ref, y_ref,
               xre_scr, xim_scr, bu_scr, ys_scr, *, steps, batch):
    rows = steps * batch
    ns = S5_SLAB_STATES

    @pl.when(pl.program_id(0) == 0)
    def _():
        xre_scr[...] = jnp.zeros_like(xre_scr)
        xim_scr[...] = jnp.zeros_like(xim_scr)

    for s in range(S5_SLABS):
        lsl = slice(s * LANES, (s + 1) * LANES)
        bu_scr[...] = jnp.dot(u_ref[:, lsl], wb_ref[s], preferred_element_type=F32)
        a_re = jnp.broadcast_to(are_ref[s], (batch, ns))
        a_im = jnp.broadcast_to(aim_ref[s], (batch, ns))

        def step(tt, carry):
            x_re, x_im = carry
            r0 = pl.multiple_of(tt * batch, batch)
            b_re = bu_scr[pl.ds(r0, batch), 0:ns]
            b_im = bu_scr[pl.ds(r0, batch), ns:2 * ns]
            n_re = a_re * x_re - a_im * x_im + b_re
            n_im = a_re * x_im + a_im * x_re + b_im
            bu_scr[pl.ds(r0, batch), 0:ns] = n_re
            bu_scr[pl.ds(r0, batch), ns:2 * ns] = n_im
            return n_re, n_im

        x_re, x_im = lax.fori_loop(0, steps, step, (xre_scr[s], xim_scr[s]))
        xre_scr[s] = x_re
        xim_scr[s] = x_im
        ys_scr[:, lsl] = jnp.dot(bu_scr[...].astype(BF16), wc_ref[s],
                                 preferred_element_type=F32)

    y = ys_scr[...] + d_ref[...] * u_ref[...].astype(F32)
    yb = _gelu_tanh(y).astype(BF16)
    half = BRANCH_WIDTH
    a = jnp.dot(yb, wg_ref[:, 0:half], preferred_element_type=F32)
    g = jnp.dot(yb, wg_ref[:, half:2 * half], preferred_element_type=F32)
    y_ref[...] = (a * _sigmoid(g)).astype(BF16)


def _s5(u_tb, wb, wc, a_re, a_im, d_skip, w_glu, *, batch, seq, steps):
    rows = steps * batch
    ns = S5_SLAB_STATES
    const3 = lambda i: (0, 0, 0)
    return pl.pallas_call(
        functools.partial(_s5_kernel, steps=steps, batch=batch),
        out_shape=jax.ShapeDtypeStruct((seq * batch, BRANCH_WIDTH), BF16),
        grid=(seq // steps,),
        in_specs=[pl.BlockSpec((rows, BRANCH_WIDTH), lambda i: (i, 0)),
                  pl.BlockSpec((S5_SLABS, LANES, 2 * ns), const3),
                  pl.BlockSpec((S5_SLABS, 2 * ns, LANES), const3),
                  pl.BlockSpec((S5_SLABS, 1, ns), const3),
                  pl.BlockSpec((S5_SLABS, 1, ns), const3),
                  pl.BlockSpec((1, BRANCH_WIDTH), lambda i: (0, 0)),
                  pl.BlockSpec((BRANCH_WIDTH, 2 * BRANCH_WIDTH), lambda i: (0, 0))],
        out_specs=pl.BlockSpec((rows, BRANCH_WIDTH), lambda i: (i, 0)),
        scratch_shapes=[pltpu.VMEM((S5_SLABS, batch, ns), F32),
                        pltpu.VMEM((S5_SLABS, batch, ns), F32),
                        pltpu.VMEM((rows, 2 * ns), F32),
                        pltpu.VMEM((rows, BRANCH_WIDTH), F32)],
        compiler_params=_cparams(("arbitrary",)),
        name="s5",
    )(u_tb, wb, wc, a_re, a_im, d_skip, w_glu)


def _merge_kernel(ya_ref, yb_ref, yc_ref, g0_ref, g1_ref, g2_ref, x_ref, wbr_ref,
                  wo_ref, ng_ref, o_ref):
    merged = None
    for n, (y_ref, g_ref) in enumerate(((ya_ref, g0_ref), (yb_ref, g1_ref),
                                        (yc_ref, g2_ref))):
        z = jnp.dot(y_ref[...], wbr_ref[n], preferred_element_type=F32)
        term = _sigmoid(g_ref[...].astype(F32)) * z
        merged = term if merged is None else merged + term
    mix = jnp.dot(merged.astype(BF16), wo_ref[...], preferred_element_type=F32)
    o_ref[...] = x_ref[...] + _rms(mix, ng_ref[...])


def _merge(y_a, y_b, y_c, proj, x2d, w_branch, w_out, norm_g, *, tm):
    m = x2d.shape[0]
    row = lambda i: (i, 0)

    def gate_spec(n):
        return pl.BlockSpec((tm, D_MODEL), lambda i: (i, COL_GATE // 2 + n))

    return pl.pallas_call(
        _merge_kernel,
        out_shape=jax.ShapeDtypeStruct((m, D_MODEL), F32),
        grid=(m // tm,),
        in_specs=[pl.BlockSpec((tm, BRANCH_WIDTH), row),
                  pl.BlockSpec((tm, BRANCH_WIDTH), row),
                  pl.BlockSpec((tm, BRANCH_WIDTH), row),
                  gate_spec(0), gate_spec(1), gate_spec(2),
                  pl.BlockSpec((tm, D_MODEL), row),
                  pl.BlockSpec((N_BRANCH, BRANCH_WIDTH, D_MODEL), lambda i: (0, 0, 0),
                               pipeline_mode=pl.Buffered(1)),
                  pl.BlockSpec((D_MODEL, D_MODEL), lambda i: (0, 0),
                               pipeline_mode=pl.Buffered(1)),
                  pl.BlockSpec((1, D_MODEL), lambda i: (0, 0))],
        out_specs=pl.BlockSpec((tm, D_MODEL), row),
        compiler_params=_cparams(("parallel",)),
        name="merge",
    )(y_a, y_b, y_c, proj, proj, proj, x2d, w_branch, w_out, norm_g)


def _ffn_kernel(x_ref, xh_ref, g_ref, wa_ref, wv_ref, cwa_ref, cwv_ref, cba_ref,
                cbv_ref, wd_ref, ng_ref, o_ref, h_scr, *, tl):
    i = pl.program_id(1)
    f = pl.program_id(2)
    nf = pl.num_programs(2)
    hal = SUBLANES

    @pl.when(f == 0)
    def _():
        halo = _rms(xh_ref[0], g_ref[...])
        halo = jnp.where(i == 0, 0.0, halo)
        h_scr[0:hal, :] = halo.astype(BF16)
        h_scr[hal:hal + tl, :] = _rms(x_ref[0], g_ref[...]).astype(BF16)
        o_ref[0] = jnp.zeros((tl, D_MODEL), F32)

    hb = h_scr[...]

    def conv(w_ref, cw_ref, cb_ref):
        up = jnp.dot(hb, w_ref[...], preferred_element_type=F32)
        out = cb_ref[...] + cw_ref[FFN_CONV - 1:FFN_CONV, :] * up[hal:hal + tl, :]
        for j in range(FFN_CONV - 1):
            off = hal - (FFN_CONV - 1) + j
            out = out + cw_ref[j:j + 1, :] * up[off:off + tl, :]
        return out

    a = conv(wa_ref, cwa_ref, cba_ref)
    v = conv(wv_ref, cwv_ref, cbv_ref)
    act = (_gelu_tanh(a) * v).astype(BF16)
    o_ref[0] += jnp.dot(act, wd_ref[...], preferred_element_type=F32)

    @pl.when(f == nf - 1)
    def _():
        o_ref[0] = x_ref[0] + _rms(o_ref[0], ng_ref[...])


def _ffn(x3d, gain, w_up, conv_w, conv_b, w_down, norm_g, *, tl, tf):
    batch, seq, _ = x3d.shape
    nfb = D_FF // tf
    hal = SUBLANES
    return pl.pallas_call(
        functools.partial(_ffn_kernel, tl=tl),
        out_shape=jax.ShapeDtypeStruct((batch, seq, D_MODEL), F32),
        grid=(batch, seq // tl, nfb),
        in_specs=[pl.BlockSpec((1, tl, D_MODEL), lambda b, i, f: (b, i, 0)),
                  pl.BlockSpec((1, hal, D_MODEL),
                               lambda b, i, f: (b, jnp.maximum(i * (tl // hal) - 1, 0), 0)),
                  pl.BlockSpec((1, D_MODEL), lambda b, i, f: (0, 0)),
                  pl.BlockSpec((D_MODEL, tf), lambda b, i, f: (0, f)),
                  pl.BlockSpec((D_MODEL, tf), lambda b, i, f: (0, nfb + f)),
                  pl.BlockSpec((FFN_CONV, tf), lambda b, i, f: (0, f)),
                  pl.BlockSpec((FFN_CONV, tf), lambda b, i, f: (0, nfb + f)),
                  pl.BlockSpec((1, tf), lambda b, i, f: (0, f)),
                  pl.BlockSpec((1, tf), lambda b, i, f: (0, nfb + f)),
                  pl.BlockSpec((tf, D_MODEL), lambda b, i, f: (f, 0)),
                  pl.BlockSpec((1, D_MODEL), lambda b, i, f: (0, 0))],
        out_specs=pl.BlockSpec((1, tl, D_MODEL), lambda b, i, f: (b, i, 0)),
        scratch_shapes=[pltpu.VMEM((tl + hal, D_MODEL), BF16)],
        compiler_params=_cparams(("parallel", "parallel", "arbitrary")),
        name="conv_ffn",
    )(x3d, x3d, gain, w_up, w_up, conv_w, conv_w, conv_b, conv_b, w_down, norm_g)


def _t5_bucket(dist):
    n = jnp.maximum(dist, 0)
    max_exact = REL_BUCKETS // 2
    nf = jnp.maximum(n, 1).astype(F32)
    large = max_exact + (jnp.log(nf / max_exact) / math.log(REL_MAX_DIST / max_exact)
                         * (REL_BUCKETS - max_exact)).astype(jnp.int32)
    large = jnp.minimum(large, REL_BUCKETS - 1)
    return jnp.where(n < max_exact, n, large)


def _attn_bias_tiles(rel_bias, blk):
    qi = jnp.arange(blk, dtype=jnp.int32)[None, :]
    kj = jnp.arange(blk, dtype=jnp.int32)[:, None]
    table = rel_bias.astype(F32) * LOG2E
    last = table[REL_BUCKETS - 1][:, None, None]
    tiles = []
    for off in (0, blk):
        dist = qi - kj + off
        bucket = _t5_bucket(dist)
        bias = jnp.broadcast_to(last, (DA_HEADS, blk, blk))
        for b in range(REL_BUCKETS - 1):
            bias = jnp.where(bucket == b, table[b][:, None, None], bias)
        tiles.append(jnp.where(dist >= 0, bias, NEG_BIG))
    return jnp.stack(tiles, axis=1)


def _s5_params(lam_re, lam_im, log_dt, b_re, b_im, c_re, c_im):
    dt = jnp.exp(log_dt)[:, None]
    mag = jnp.exp(lam_re * dt)
    a_re = mag * jnp.cos(lam_im * dt)
    a_im = mag * jnp.sin(lam_im * dt)
    den = lam_re * lam_re + lam_im * lam_im
    z_re = ((a_re - 1.0) * lam_re + a_im * lam_im) / den
    z_im = (a_im * lam_re - (a_re - 1.0) * lam_im) / den
    bb_re = z_re[..., None] * b_re - z_im[..., None] * b_im
    bb_im = z_re[..., None] * b_im + z_im[..., None] * b_re
    gs = S5_GROUPS // S5_SLABS
    eye = jnp.eye(gs, dtype=F32)

    def in_blocks(bb):
        bb = bb.reshape(S5_SLABS, gs, S5_STATE, S5_GROUP)
        w = jnp.einsum('sgpc,gh->sgchp', bb, eye)
        return w.reshape(S5_SLABS, gs * S5_GROUP, gs * S5_STATE)

    def out_blocks(cc):
        cc = cc.reshape(S5_SLABS, gs, S5_GROUP, S5_STATE)
        w = jnp.einsum('sgcp,gh->sgphc', cc, eye)
        return w.reshape(S5_SLABS, gs * S5_STATE, gs * S5_GROUP)

    wb = jnp.concatenate([in_blocks(bb_re), in_blocks(bb_im)], axis=-1).astype(BF16)
    wc = jnp.concatenate([out_blocks(c_re), out_blocks(-c_im)], axis=-2).astype(BF16)
    a_re = a_re.reshape(S5_SLABS, 1, S5_SLAB_STATES)
    a_im = a_im.reshape(S5_SLABS, 1, S5_SLAB_STATES)
    return wb, wc, a_re, a_im


IN_PROJ_TM, IN_PROJ_TN = 1024, 2048
MLSTM_CHUNK = 128
ATTN_BLOCK = 512
S5_STEPS = 32
MERGE_TM = 256
FFN_TL, FFN_TF = 512, 512


def _layer(x2d, batch, seq, layer, p):
    w_in = p['w_in'][layer]
    n_if = 2 * M_HEADS
    split = 4 * M_HEADS * M_HEAD_DIM
    w_main = jnp.concatenate([w_in[:, :split], w_in[:, split + n_if:]], axis=1).astype(BF16)
    w_gate = jnp.pad(w_in[:, split:split + n_if],
                     ((0, 0), (0, N_GATE_PAD - n_if))).astype(BF16)
    proj, gates = _in_proj(x2d, p['norm_mix_pre'][layer][None, :], w_gate, w_main,
                           tm=min(IN_PROJ_TM, batch * seq), tn=IN_PROJ_TN)

    gate_bias = jnp.pad(p['mlstm_b_if'][layer].reshape(1, n_if),
                        ((0, 0), (0, N_GATE_PAD - n_if)))
    y_a = _mlstm(proj, gates, gate_bias, p['mlstm_conv'][layer],
                 p['mlstm_norm'][layer][None, :], batch=batch, seq=seq,
                 chunk=min(MLSTM_CHUNK, seq))

    lambda_init = 0.8 - 0.6 * math.exp(-0.3 * layer)
    lam = p['diff_lambda'][layer]
    lam_full = (jnp.exp(jnp.sum(lam[0] * lam[1])) - jnp.exp(jnp.sum(lam[2] * lam[3]))
                + lambda_init)
    blk = min(ATTN_BLOCK, seq)
    scalars = jnp.concatenate(
        [lam_full[None], p['rel_bias'][REL_BUCKETS - 1, :] * LOG2E]).astype(F32)
    y_b = _diff_attn(scalars, proj, p['attn_bias_tiles'],
                     p['diff_norm'][layer][None, :], batch=batch, seq=seq, blk=blk,
                     out_scale=1.0 - lambda_init)

    wb, wc, a_re, a_im = _s5_params(
        p['s5_lambda_re'][layer], p['s5_lambda_im'][layer], p['s5_log_dt'][layer],
        p['s5_b_re'][layer], p['s5_b_im'][layer], p['s5_c_re'][layer], p['s5_c_im'][layer])
    u = proj[:, COL_US * BRANCH_WIDTH:(COL_US + 1) * BRANCH_WIDTH]
    u_tb = u.reshape(batch, seq, BRANCH_WIDTH).transpose(1, 0, 2).reshape(seq * batch, -1)
    y_c_tb = _s5(u_tb, wb, wc, a_re, a_im, p['s5_d'][layer][None, :],
                 p['s5_w_glu'][layer].astype(BF16), batch=batch, seq=seq,
                 steps=min(S5_STEPS, seq))
    y_c = y_c_tb.reshape(seq, batch, BRANCH_WIDTH).transpose(1, 0, 2).reshape(batch * seq, -1)

    x2d = _merge(y_a, y_b, y_c, proj, x2d, p['w_branch'][layer].astype(BF16),
                 p['w_out'][layer].astype(BF16), p['norm_mix_post'][layer][None, :],
                 tm=MERGE_TM)

    x3d = _ffn(x2d.reshape(batch, seq, D_MODEL), p['norm_ffn_pre'][layer][None, :],
               p['w_up'][layer].astype(BF16), p['ffn_conv'][layer],
               p['ffn_conv_b'][layer][None, :], p['w_down'][layer].astype(BF16),
               p['norm_ffn_post'][layer][None, :], tl=min(FFN_TL, seq), tf=FFN_TF)
    return x3d.reshape(batch * seq, D_MODEL)


def kernel(x, norm_mix_pre, norm_mix_post, norm_ffn_pre, norm_ffn_post, w_in, mlstm_b_if, mlstm_conv, mlstm_norm, diff_lambda, diff_norm, rel_bias, s5_lambda_re, s5_lambda_im, s5_log_dt, s5_b_re, s5_b_im, s5_c_re, s5_c_im, s5_d, s5_w_glu, w_branch, w_out, w_up, ffn_conv, ffn_conv_b, w_down):
    p = dict(norm_mix_pre=norm_mix_pre, norm_mix_post=norm_mix_post,
             norm_ffn_pre=norm_ffn_pre, norm_ffn_post=norm_ffn_post, w_in=w_in,
             mlstm_b_if=mlstm_b_if, mlstm_conv=mlstm_conv, mlstm_norm=mlstm_norm,
             diff_lambda=diff_lambda, diff_norm=diff_norm, rel_bias=rel_bias,
             s5_lambda_re=s5_lambda_re, s5_lambda_im=s5_lambda_im, s5_log_dt=s5_log_dt,
             s5_b_re=s5_b_re, s5_b_im=s5_b_im, s5_c_re=s5_c_re, s5_c_im=s5_c_im,
             s5_d=s5_d, s5_w_glu=s5_w_glu, w_branch=w_branch, w_out=w_out, w_up=w_up,
             ffn_conv=ffn_conv, ffn_conv_b=ffn_conv_b, w_down=w_down)
    batch, seq, _ = x.shape
    p['attn_bias_tiles'] = _attn_bias_tiles(rel_bias, min(ATTN_BLOCK, seq))
    x2d = x.reshape(batch * seq, D_MODEL)
    for layer in range(DEPTH):
        x2d = _layer(x2d, batch, seq, layer, p)
    return x2d.reshape(batch, seq, D_MODEL)
```

```python
import functools
import math

import jax
import jax.numpy as jnp
from jax import lax
from jax.experimental import pallas as pl
from jax.experimental.pallas import tpu as pltpu

F32 = jnp.float32
BF16 = jnp.bfloat16
HIGHEST = lax.Precision.HIGHEST

D_MODEL = 2048
DEPTH = 2
BRANCH_WIDTH = 1024
N_BRANCH = 3
M_HEADS = 4
M_HEAD_DIM = 256
M_CONV = 4
DA_HEADS = 4
DA_HEAD_DIM = 128
DA_V_DIM = 256
REL_BUCKETS = 32
REL_MAX_DIST = 128
S5_GROUP = 16
S5_GROUPS = 64
S5_STATE = 64
D_FF = 5632
FFN_CONV = 3
EPS = 1e-6

LANES = 128
SUBLANES = 8
VMEM_LIMIT = 56 * 1024 * 1024

N_MAIN = 14336
COL_QM, COL_KM, COL_VM, COL_OM = 0, 1, 2, 3
COL_QD, COL_KD, COL_VD, COL_US = 4, 5, 6, 7
COL_GATE = 8
N_GATE_PAD = LANES

S5_SLABS = 8
S5_SLAB_STATES = 512

NEG_BIG = -1e30
LOG2E = math.log2(math.e)


def _cparams(sem):
    return pltpu.CompilerParams(dimension_semantics=sem, vmem_limit_bytes=VMEM_LIMIT)


def _sigmoid(x):
    return 0.5 * jnp.tanh(0.5 * x) + 0.5


def _gelu_tanh(x):
    c = math.sqrt(2.0 / math.pi)
    return 0.5 * x * (1.0 + jnp.tanh(c * (x + 0.044715 * (x * x * x))))


def _rms(x, gain):
    var = jnp.mean(x * x, axis=-1, keepdims=True)
    return x * lax.rsqrt(var + EPS) * gain


def _in_proj_kernel(x_ref, g_ref, wg_ref, w_ref, o_ref, og_ref, h_scr):
    @pl.when(pl.program_id(1) == 0)
    def _():
        h_scr[...] = _rms(x_ref[...], g_ref[...]).astype(BF16)
        og_ref[...] = jnp.dot(h_scr[...], wg_ref[...], preferred_element_type=F32)

    o_ref[...] = jnp.dot(h_scr[...], w_ref[...],
                         preferred_element_type=F32).astype(BF16)


def _in_proj(x2d, gain, w_gate, w_main, *, tm, tn):
    m = x2d.shape[0]
    return pl.pallas_call(
        _in_proj_kernel,
        out_shape=(jax.ShapeDtypeStruct((m, N_MAIN), BF16),
                   jax.ShapeDtypeStruct((m, N_GATE_PAD), F32)),
        grid=(m // tm, N_MAIN // tn),
        in_specs=[pl.BlockSpec((tm, D_MODEL), lambda i, n: (i, 0)),
                  pl.BlockSpec((1, D_MODEL), lambda i, n: (0, 0)),
                  pl.BlockSpec((D_MODEL, N_GATE_PAD), lambda i, n: (0, 0)),
                  pl.BlockSpec((D_MODEL, tn), lambda i, n: (0, n))],
        out_specs=(pl.BlockSpec((tm, tn), lambda i, n: (i, n)),
                   pl.BlockSpec((tm, N_GATE_PAD), lambda i, n: (i, 0))),
        scratch_shapes=[pltpu.VMEM((tm, D_MODEL), BF16)],
        compiler_params=_cparams(("parallel", "arbitrary")),
        name="in_proj",
    )(x2d, gain, w_gate, w_main)


def _mlstm_kernel(q_ref, k_ref, v_ref, o_ref, gt_ref, gb_ref, cw_ref, ng_ref, y_ref,
                  c_scr, n_scr, m_scr, qe_scr, ke_scr, *, chunk):
    t = chunk
    hd = M_HEAD_DIM
    width = M_HEADS * hd

    @pl.when(pl.program_id(1) == 0)
    def _():
        c_scr[...] = jnp.zeros_like(c_scr)
        n_scr[...] = jnp.zeros_like(n_scr)
        m_scr[...] = jnp.zeros_like(m_scr)
        qe_scr[0:SUBLANES, :] = jnp.zeros((SUBLANES, width), F32)
        ke_scr[0:SUBLANES, :] = jnp.zeros((SUBLANES, width), F32)

    qe_scr[SUBLANES:SUBLANES + t, :] = q_ref[...].astype(F32)
    ke_scr[SUBLANES:SUBLANES + t, :] = k_ref[...].astype(F32)

    gates = gt_ref[...] + gb_ref[...]
    log_f = jnp.minimum(gates, 0.0) - jnp.log1p(jnp.exp(-jnp.abs(gates)))
    row = lax.broadcasted_iota(jnp.int32, (t, t), 0)
    col = lax.broadcasted_iota(jnp.int32, (t, t), 1)
    causal = col <= row
    cum = jnp.dot(causal.astype(F32), log_f, preferred_element_type=F32,
                  precision=HIGHEST)
    gates_t = gates.T
    cum_t = cum.T

    for h in range(M_HEADS):
        sl = slice(h * hd, (h + 1) * hd)
        ksl = slice(width + h * hd, width + (h + 1) * hd)
        qc = jnp.zeros((t, hd), F32)
        kc = jnp.zeros((t, hd), F32)
        for j in range(M_CONV):
            off = SUBLANES - (M_CONV - 1) + j
            qc = qc + cw_ref[j:j + 1, sl] * qe_scr[off:off + t, sl]
            kc = kc + cw_ref[j:j + 1, ksl] * ke_scr[off:off + t, sl]
        qc = qc * _sigmoid(qc)
        kc = kc * _sigmoid(kc) * (hd ** -0.5)
        qb = qc.astype(BF16)
        kb = kc.astype(BF16)
        vb = v_ref[:, sl]

        li_row = gates_t[h:h + 1, :]
        b_row = cum_t[M_HEADS + h:M_HEADS + h + 1, :]
        li_col = gates[:, h:h + 1]
        b_col = cum[:, M_HEADS + h:M_HEADS + h + 1]
        m_prev = m_scr[h:h + 1, 0:1]
        c_prev = c_scr[h]
        n_prev = n_scr[h:h + 1, :]

        dmat = jnp.where(causal, b_col - b_row + li_row, -jnp.inf)
        inter = b_col + m_prev
        m_t = jnp.maximum(inter, jnp.max(dmat, axis=-1, keepdims=True))
        s = lax.dot_general(qb, kb, (((1,), (1,)), ((), ())),
                            preferred_element_type=F32) * jnp.exp(dmat - m_t)
        w_inter = jnp.exp(inter - m_t)
        num = (jnp.dot(s.astype(BF16), vb, preferred_element_type=F32)
               + w_inter * jnp.dot(qb, c_prev.astype(BF16), preferred_element_type=F32))
        den = (jnp.sum(s, axis=-1, keepdims=True)
               + w_inter * jnp.sum(qc * n_prev, axis=-1, keepdims=True))
        hh = num / jnp.maximum(jnp.abs(den), jnp.exp(-m_t))

        g = cum[t - 1:t, M_HEADS + h:M_HEADS + h + 1]
        a_col = g - b_col + li_col
        m_new = jnp.maximum(g + m_prev, jnp.max(a_col, axis=0, keepdims=True))
        ws = jnp.exp(a_col - m_new)
        decay = jnp.exp(g + m_prev - m_new)
        kw = ws * kc
        c_scr[h] = decay * c_prev + lax.dot_general(
            kw.astype(BF16), vb, (((0,), (0,)), ((), ())), preferred_element_type=F32)
        n_scr[h:h + 1, :] = decay * n_prev + jnp.sum(kw, axis=0, keepdims=True)
        m_scr[h:h + 1, :] = jnp.broadcast_to(m_new, (1, LANES))

        hn = _rms(hh, ng_ref[:, sl])
        y_ref[:, sl] = (_sigmoid(o_ref[:, sl].astype(F32)) * hn).astype(BF16)

    qe_scr[0:SUBLANES, :] = qe_scr[t:t + SUBLANES, :]
    ke_scr[0:SUBLANES, :] = ke_scr[t:t + SUBLANES, :]


def _mlstm(proj, gates, gate_bias, conv_w, norm_g, *, batch, seq, chunk):
    nc = seq // chunk
    width = M_HEADS * M_HEAD_DIM

    def col_spec(cb):
        return pl.BlockSpec((chunk, width), lambda b, c: (b * nc + c, cb))

    return pl.pallas_call(
        functools.partial(_mlstm_kernel, chunk=chunk),
        out_shape=jax.ShapeDtypeStruct((batch * seq, width), BF16),
        grid=(batch, nc),
        in_specs=[col_spec(COL_QM), col_spec(COL_KM), col_spec(COL_VM), col_spec(COL_OM),
                  pl.BlockSpec((chunk, N_GATE_PAD), lambda b, c: (b * nc + c, 0)),
                  pl.BlockSpec((1, N_GATE_PAD), lambda b, c: (0, 0)),
                  pl.BlockSpec((M_CONV, 2 * width), lambda b, c: (0, 0)),
                  pl.BlockSpec((1, width), lambda b, c: (0, 0))],
        out_specs=pl.BlockSpec((chunk, width), lambda b, c: (b * nc + c, 0)),
        scratch_shapes=[pltpu.VMEM((M_HEADS, M_HEAD_DIM, M_HEAD_DIM), F32),
                        pltpu.VMEM((M_HEADS, M_HEAD_DIM), F32),
                        pltpu.VMEM((M_HEADS, LANES), F32),
                        pltpu.VMEM((chunk + SUBLANES, width), F32),
                        pltpu.VMEM((chunk + SUBLANES, width), F32)],
        compiler_params=_cparams(("parallel", "arbitrary")),
        name="mlstm",
    )(proj, proj, proj, proj, gates, gate_bias, conv_w, norm_g)


def _attn_kernel(sc_ref, q_ref, k_ref, v_ref, bias_ref, ng_ref, y_ref,
                 m_scr, l_scr, acc_scr, *, blk, out_scale):
    t = blk
    d = DA_HEAD_DIM
    h = pl.program_id(1)
    i = pl.program_id(2)
    lam = sc_ref[0]
    far_bias = sc_ref[1 + h]

    m_scr[...] = jnp.full(m_scr.shape, NEG_BIG, F32)
    l_scr[...] = jnp.zeros_like(l_scr)
    acc_scr[...] = jnp.zeros_like(acc_scr)
    qs = (q_ref[...].astype(F32) * (d ** -0.5 * LOG2E)).astype(BF16)

    def block_step(j, bias, far):
        start = pl.multiple_of(j * t, t)
        kb = k_ref[pl.ds(start, t), :]
        vb = v_ref[pl.ds(start, t), :]
        for c in range(2):
            s = lax.dot_general(kb[:, c * d:(c + 1) * d], qs[:, c * d:(c + 1) * d],
                                (((1,), (1,)), ((), ())), preferred_element_type=F32)
            m_old = m_scr[c]
            if far:
                m_new = jnp.maximum(m_old, jnp.max(s, axis=0, keepdims=True) + bias)
                p = jnp.exp2(s - (m_new - bias))
            else:
                s = s + bias
                m_new = jnp.maximum(m_old, jnp.max(s, axis=0, keepdims=True))
                p = jnp.exp2(s - m_new)
            alpha = jnp.exp2(m_old - m_new)
            l_scr[c] = alpha * l_scr[c] + jnp.sum(p, axis=0, keepdims=True)
            acc_scr[c] = alpha * acc_scr[c] + lax.dot_general(
                vb, p.astype(BF16), (((0,), (0,)), ((), ())), preferred_element_type=F32)
            m_scr[c] = m_new

    def far_body(j, carry):
        block_step(j, far_bias, True)
        return carry

    lax.fori_loop(0, jnp.maximum(i - 1, 0), far_body, 0)

    @pl.when(i >= 1)
    def _():
        block_step(i - 1, bias_ref[0, 1], False)

    block_step(i, bias_ref[0, 0], False)

    out_t = acc_scr[0] * (1.0 / l_scr[0]) - lam * (acc_scr[1] * (1.0 / l_scr[1]))
    y_ref[...] = (_rms(out_t.T, ng_ref[...]) * out_scale).astype(BF16)


def _diff_attn(scalars, proj, bias_tiles, norm_g, *, batch, seq, blk, out_scale):
    nq = seq // blk
    kvw = DA_V_DIM
    return pl.pallas_call(
        functools.partial(_attn_kernel, blk=blk, out_scale=out_scale),
        out_shape=jax.ShapeDtypeStruct((batch * seq, DA_HEADS * DA_V_DIM), BF16),
        grid=(batch, DA_HEADS, nq),
        in_specs=[pl.BlockSpec(memory_space=pltpu.SMEM),
                  pl.BlockSpec((blk, kvw), lambda b, h, i: (b * nq + i, COL_QD * 4 + h)),
                  pl.BlockSpec((seq, kvw), lambda b, h, i: (b, COL_KD * 4 + h)),
                  pl.BlockSpec((seq, kvw), lambda b, h, i: (b, COL_VD * 4 + h)),
                  pl.BlockSpec((1, 2, blk, blk), lambda b, h, i: (h, 0, 0, 0)),
                  pl.BlockSpec((1, kvw), lambda b, h, i: (0, h))],
        out_specs=pl.BlockSpec((blk, kvw), lambda b, h, i: (b * nq + i, h)),
        scratch_shapes=[pltpu.VMEM((2, 1, blk), F32),
                        pltpu.VMEM((2, 1, blk), F32),
                        pltpu.VMEM((2, DA_V_DIM, blk), F32)],
        compiler_params=_cparams(("parallel", "parallel", "arbitrary")),
        name="diff_attn",
    )(scalars, proj, proj, proj, bias_tiles, norm_g)


def _s5_kernel(u_ref, wb_ref, wc_ref, are_ref, aim_ref, d_ref, wg_ref, y_ref,
               xre_scr, xim_scr, bu_scr, ys_scr, *, steps, batch):
    rows = steps * batch
    ns = S5_SLAB_STATES

    @pl.when(pl.program_id(0) == 0)
    def _():
        xre_scr[...] = jnp.zeros_like(xre_scr)
        xim_scr[...] = jnp.zeros_like(xim_scr)

    for s in range(S5_SLABS):
        lsl = slice(s * LANES, (s + 1) * LANES)
        buf = bu_scr.at[s % 2]
        buf[...] = jnp.dot(u_ref[:, lsl], wb_ref[s], preferred_element_type=F32)
        a_re = jnp.broadcast_to(are_ref[s], (batch, ns))
        a_im = jnp.broadcast_to(aim_ref[s], (batch, ns))

        x_re = xre_scr[s]
        x_im = xim_scr[s]
        for tt in range(steps):
            rs = slice(tt * batch, (tt + 1) * batch)
            n_re = a_re * x_re - a_im * x_im + buf[rs, 0:ns]
            n_im = a_re * x_im + a_im * x_re + buf[rs, ns:2 * ns]
            buf[rs, 0:ns] = n_re
            buf[rs, ns:2 * ns] = n_im
            x_re, x_im = n_re, n_im
        xre_scr[s] = x_re
        xim_scr[s] = x_im
        ys_scr[:, lsl] = jnp.dot(buf[...].astype(BF16), wc_ref[s],
                                 preferred_element_type=F32)

    y = ys_scr[...] + d_ref[...] * u_ref[...].astype(F32)
    yb = _gelu_tanh(y).astype(BF16)
    half = BRANCH_WIDTH
    a = jnp.dot(yb, wg_ref[:, 0:half], preferred_element_type=F32)
    g = jnp.dot(yb, wg_ref[:, half:2 * half], preferred_element_type=F32)
    y_ref[...] = (a * _sigmoid(g)).astype(BF16)


def _s5(u_tb, wb, wc, a_re, a_im, d_skip, w_glu, *, batch, seq, steps):
    rows = steps * batch
    ns = S5_SLAB_STATES
    const3 = lambda i: (0, 0, 0)
    return pl.pallas_call(
        functools.partial(_s5_kernel, steps=steps, batch=batch),
        out_shape=jax.ShapeDtypeStruct((seq * batch, BRANCH_WIDTH), BF16),
        grid=(seq // steps,),
        in_specs=[pl.BlockSpec((rows, BRANCH_WIDTH), lambda i: (i, 0)),
                  pl.BlockSpec((S5_SLABS, LANES, 2 * ns), const3),
                  pl.BlockSpec((S5_SLABS, 2 * ns, LANES), const3),
                  pl.BlockSpec((S5_SLABS, 1, ns), const3),
                  pl.BlockSpec((S5_SLABS, 1, ns), const3),
                  pl.BlockSpec((1, BRANCH_WIDTH), lambda i: (0, 0)),
                  pl.BlockSpec((BRANCH_WIDTH, 2 * BRANCH_WIDTH), lambda i: (0, 0))],
        out_specs=pl.BlockSpec((rows, BRANCH_WIDTH), lambda i: (i, 0)),
        scratch_shapes=[pltpu.VMEM((S5_SLABS, batch, ns), F32),
                        pltpu.VMEM((S5_SLABS, batch, ns), F32),
                        pltpu.VMEM((2, rows, 2 * ns), F32),
                        pltpu.VMEM((rows, BRANCH_WIDTH), F32)],
        compiler_params=_cparams(("arbitrary",)),
        name="s5",
    )(u_tb, wb, wc, a_re, a_im, d_skip, w_glu)


def _merge_kernel(ya_ref, yb_ref, yc_ref, g0_ref, g1_ref, g2_ref, x_ref, wbr_ref,
                  wo_ref, ng_ref, o_ref):
    merged = None
    for n, (y_ref, g_ref) in enumerate(((ya_ref, g0_ref), (yb_ref, g1_ref),
                                        (yc_ref, g2_ref))):
        z = jnp.dot(y_ref[...], wbr_ref[n], preferred_element_type=F32)
        term = _sigmoid(g_ref[...].astype(F32)) * z
        merged = term if merged is None else merged + term
    mix = jnp.dot(merged.astype(BF16), wo_ref[...], preferred_element_type=F32)
    o_ref[...] = x_ref[...] + _rms(mix, ng_ref[...])


def _merge(y_a, y_b, y_c, proj, x2d, w_branch, w_out, norm_g, *, tm):
    m = x2d.shape[0]
    row = lambda i: (i, 0)

    def gate_spec(n):
        return pl.BlockSpec((tm, D_MODEL), lambda i: (i, COL_GATE // 2 + n))

    return pl.pallas_call(
        _merge_kernel,
        out_shape=jax.ShapeDtypeStruct((m, D_MODEL), F32),
        grid=(m // tm,),
        in_specs=[pl.BlockSpec((tm, BRANCH_WIDTH), row),
                  pl.BlockSpec((tm, BRANCH_WIDTH), row),
                  pl.BlockSpec((tm, BRANCH_WIDTH), row),
                  gate_spec(0), gate_spec(1), gate_spec(2),
                  pl.BlockSpec((tm, D_MODEL), row),
                  pl.BlockSpec((N_BRANCH, BRANCH_WIDTH, D_MODEL), lambda i: (0, 0, 0),
                               pipeline_mode=pl.Buffered(1)),
                  pl.BlockSpec((D_MODEL, D_MODEL), lambda i: (0, 0),
                               pipeline_mode=pl.Buffered(1)),
                  pl.BlockSpec((1, D_MODEL), lambda i: (0, 0))],
        out_specs=pl.BlockSpec((tm, D_MODEL), row),
        compiler_params=_cparams(("parallel",)),
        name="merge",
    )(y_a, y_b, y_c, proj, proj, proj, x2d, w_branch, w_out, norm_g)


def _ffn_kernel(x_ref, xh_ref, g_ref, wa_ref, wv_ref, cwa_ref, cwv_ref, cba_ref,
                cbv_ref, wd_ref, ng_ref, o_ref, h_scr, *, tl):
    i = pl.program_id(1)
    f = pl.program_id(2)
    nf = pl.num_programs(2)
    hal = SUBLANES

    @pl.when(f == 0)
    def _():
        halo = _rms(xh_ref[0], g_ref[...])
        halo = jnp.where(i == 0, 0.0, halo)
        h_scr[0:hal, :] = halo.astype(BF16)
        h_scr[hal:hal + tl, :] = _rms(x_ref[0], g_ref[...]).astype(BF16)
        o_ref[0] = jnp.zeros((tl, D_MODEL), F32)

    hb = h_scr[...]

    def conv(w_ref, cw_ref, cb_ref):
        up = jnp.dot(hb, w_ref[...], preferred_element_type=F32)
        out = cb_ref[...] + cw_ref[FFN_CONV - 1:FFN_CONV, :] * up[hal:hal + tl, :]
        for j in range(FFN_CONV - 1):
            off = hal - (FFN_CONV - 1) + j
            out = out + cw_ref[j:j + 1, :] * up[off:off + tl, :]
        return out

    a = conv(wa_ref, cwa_ref, cba_ref)
    v = conv(wv_ref, cwv_ref, cbv_ref)
    act = (_gelu_tanh(a) * v).astype(BF16)
    o_ref[0] += jnp.dot(act, wd_ref[...], preferred_element_type=F32)

    @pl.when(f == nf - 1)
    def _():
        o_ref[0] = x_ref[0] + _rms(o_ref[0], ng_ref[...])


def _ffn(x3d, gain, w_up, conv_w, conv_b, w_down, norm_g, *, tl, tf):
    batch, seq, _ = x3d.shape
    nfb = D_FF // tf
    hal = SUBLANES
    return pl.pallas_call(
        functools.partial(_ffn_kernel, tl=tl),
        out_shape=jax.ShapeDtypeStruct((batch, seq, D_MODEL), F32),
        grid=(batch, seq // tl, nfb),
        in_specs=[pl.BlockSpec((1, tl, D_MODEL), lambda b, i, f: (b, i, 0)),
                  pl.BlockSpec((1, hal, D_MODEL),
                               lambda b, i, f: (b, jnp.maximum(i * (tl // hal) - 1, 0), 0)),
                  pl.BlockSpec((1, D_MODEL), lambda b, i, f: (0, 0)),
                  pl.BlockSpec((D_MODEL, tf), lambda b, i, f: (0, f)),
                  pl.BlockSpec((D_MODEL, tf), lambda b, i, f: (0, nfb + f)),
                  pl.BlockSpec((FFN_CONV, tf), lambda b, i, f: (0, f)),
                  pl.BlockSpec((FFN_CONV, tf), lambda b, i, f: (0, nfb + f)),
                  pl.BlockSpec((1, tf), lambda b, i, f: (0, f)),
                  pl.BlockSpec((1, tf), lambda b, i, f: (0, nfb + f)),
                  pl.BlockSpec((tf, D_MODEL), lambda b, i, f: (f, 0)),
                  pl.BlockSpec((1, D_MODEL), lambda b, i, f: (0, 0))],
        out_specs=pl.BlockSpec((1, tl, D_MODEL), lambda b, i, f: (b, i, 0)),
        scratch_shapes=[pltpu.VMEM((tl + hal, D_MODEL), BF16)],
        compiler_params=_cparams(("parallel", "parallel", "arbitrary")),
        name="conv_ffn",
    )(x3d, x3d, gain, w_up, w_up, conv_w, conv_w, conv_b, conv_b, w_down, norm_g)


def _t5_bucket(dist):
    n = jnp.maximum(dist, 0)
    max_exact = REL_BUCKETS // 2
    nf = jnp.maximum(n, 1).astype(F32)
    large = max_exact + (jnp.log(nf / max_exact) / math.log(REL_MAX_DIST / max_exact)
                         * (REL_BUCKETS - max_exact)).astype(jnp.int32)
    large = jnp.minimum(large, REL_BUCKETS - 1)
    return jnp.where(n < max_exact, n, large)


def _attn_bias_tiles(rel_bias, blk):
    qi = jnp.arange(blk, dtype=jnp.int32)[None, :]
    kj = jnp.arange(blk, dtype=jnp.int32)[:, None]
    table = rel_bias.astype(F32) * LOG2E
    last = table[REL_BUCKETS - 1][:, None, None]
    tiles = []
    for off in (0, blk):
        dist = qi - kj + off
        bucket = _t5_bucket(dist)
        bias = jnp.broadcast_to(last, (DA_HEADS, blk, blk))
        for b in range(REL_BUCKETS - 1):
            bias = jnp.where(bucket == b, table[b][:, None, None], bias)
        tiles.append(jnp.where(dist >= 0, bias, NEG_BIG))
    return jnp.stack(tiles, axis=1)


def _s5_params(lam_re, lam_im, log_dt, b_re, b_im, c_re, c_im):
    dt = jnp.exp(log_dt)[:, None]
    mag = jnp.exp(lam_re * dt)
    a_re = mag * jnp.cos(lam_im * dt)
    a_im = mag * jnp.sin(lam_im * dt)
    den = lam_re * lam_re + lam_im * lam_im
    z_re = ((a_re - 1.0) * lam_re + a_im * lam_im) / den
    z_im = (a_im * lam_re - (a_re - 1.0) * lam_im) / den
    bb_re = z_re[..., None] * b_re - z_im[..., None] * b_im
    bb_im = z_re[..., None] * b_im + z_im[..., None] * b_re
    gs = S5_GROUPS // S5_SLABS
    eye = jnp.eye(gs, dtype=F32)

    def in_blocks(bb):
        bb = bb.reshape(S5_SLABS, gs, S5_STATE, S5_GROUP)
        w = jnp.einsum('sgpc,gh->sgchp', bb, eye)
        return w.reshape(S5_SLABS, gs * S5_GROUP, gs * S5_STATE)

    def out_blocks(cc):
        cc = cc.reshape(S5_SLABS, gs, S5_GROUP, S5_STATE)
        w = jnp.einsum('sgcp,gh->sgphc', cc, eye)
        return w.reshape(S5_SLABS, gs * S5_STATE, gs * S5_GROUP)

    wb = jnp.concatenate([in_blocks(bb_re), in_blocks(bb_im)], axis=-1).astype(BF16)
    wc = jnp.concatenate([out_blocks(c_re), out_blocks(-c_im)], axis=-2).astype(BF16)
    a_re = a_re.reshape(S5_SLABS, 1, S5_SLAB_STATES)
    a_im = a_im.reshape(S5_SLABS, 1, S5_SLAB_STATES)
    return wb, wc, a_re, a_im


IN_PROJ_TM, IN_PROJ_TN = 1024, 2048
MLSTM_CHUNK = 128
ATTN_BLOCK = 512
S5_STEPS = 32
MERGE_TM = 256
FFN_TL, FFN_TF = 512, 512


def _layer(x2d, batch, seq, layer, p):
    w_in = p['w_in'][layer]
    n_if = 2 * M_HEADS
    split = 4 * M_HEADS * M_HEAD_DIM
    w_main = jnp.concatenate([w_in[:, :split], w_in[:, split + n_if:]], axis=1).astype(BF16)
    w_gate = jnp.pad(w_in[:, split:split + n_if],
                     ((0, 0), (0, N_GATE_PAD - n_if))).astype(BF16)
    proj, gates = _in_proj(x2d, p['norm_mix_pre'][layer][None, :], w_gate, w_main,
                           tm=min(IN_PROJ_TM, batch * seq), tn=IN_PROJ_TN)

    gate_bias = jnp.pad(p['mlstm_b_if'][layer].reshape(1, n_if),
                        ((0, 0), (0, N_GATE_PAD - n_if)))
    y_a = _mlstm(proj, gates, gate_bias, p['mlstm_conv'][layer],
                 p['mlstm_norm'][layer][None, :], batch=batch, seq=seq,
                 chunk=min(MLSTM_CHUNK, seq))

    lambda_init = 0.8 - 0.6 * math.exp(-0.3 * layer)
    lam = p['diff_lambda'][layer]
    lam_full = (jnp.exp(jnp.sum(lam[0] * lam[1])) - jnp.exp(jnp.sum(lam[2] * lam[3]))
                + lambda_init)
    blk = min(ATTN_BLOCK, seq)
    scalars = jnp.concatenate(
        [lam_full[None], p['rel_bias'][REL_BUCKETS - 1, :] * LOG2E]).astype(F32)
    y_b = _diff_attn(scalars, proj, p['attn_bias_tiles'],
                     p['diff_norm'][layer][None, :], batch=batch, seq=seq, blk=blk,
                     out_scale=1.0 - lambda_init)

    wb, wc, a_re, a_im = _s5_params(
        p['s5_lambda_re'][layer], p['s5_lambda_im'][layer], p['s5_log_dt'][layer],
        p['s5_b_re'][layer], p['s5_b_im'][layer], p['s5_c_re'][layer], p['s5_c_im'][layer])
    u = proj[:, COL_US * BRANCH_WIDTH:(COL_US + 1) * BRANCH_WIDTH]
    u_tb = u.reshape(batch, seq, BRANCH_WIDTH).transpose(1, 0, 2).reshape(seq * batch, -1)
    y_c_tb = _s5(u_tb, wb, wc, a_re, a_im, p['s5_d'][layer][None, :],
                 p['s5_w_glu'][layer].astype(BF16), batch=batch, seq=seq,
                 steps=min(S5_STEPS, seq))
    y_c = y_c_tb.reshape(seq, batch, BRANCH_WIDTH).transpose(1, 0, 2).reshape(batch * seq, -1)

    x2d = _merge(y_a, y_b, y_c, proj, x2d, p['w_branch'][layer].astype(BF16),
                 p['w_out'][layer].astype(BF16), p['norm_mix_post'][layer][None, :],
                 tm=MERGE_TM)

    x3d = _ffn(x2d.reshape(batch, seq, D_MODEL), p['norm_ffn_pre'][layer][None, :],
               p['w_up'][layer].astype(BF16), p['ffn_conv'][layer],
               p['ffn_conv_b'][layer][None, :], p['w_down'][layer].astype(BF16),
               p['norm_ffn_post'][layer][None, :], tl=min(FFN_TL, seq), tf=FFN_TF)
    return x3d.reshape(batch * seq, D_MODEL)


def kernel(x, norm_mix_pre, norm_mix_post, norm_ffn_pre, norm_ffn_post, w_in, mlstm_b_if, mlstm_conv, mlstm_norm, diff_lambda, diff_norm, rel_bias, s5_lambda_re, s5_lambda_im, s5_log_dt, s5_b_re, s5_b_im, s5_c_re, s5_c_im, s5_d, s5_w_glu, w_branch, w_out, w_up, ffn_conv, ffn_conv_b, w_down):
    p = dict(norm_mix_pre=norm_mix_pre, norm_mix_post=norm_mix_post,
             norm_ffn_pre=norm_ffn_pre, norm_ffn_post=norm_ffn_post, w_in=w_in,
             mlstm_b_if=mlstm_b_if, mlstm_conv=mlstm_conv, mlstm_norm=mlstm_norm,
             diff_lambda=diff_lambda, diff_norm=diff_norm, rel_bias=rel_bias,
             s5_lambda_re=s5_lambda_re, s5_lambda_im=s5_lambda_im, s5_log_dt=s5_log_dt,
             s5_b_re=s5_b_re, s5_b_im=s5_b_im, s5_c_re=s5_c_re, s5_c_im=s5_c_im,
             s5_d=s5_d, s5_w_glu=s5_w_glu, w_branch=w_branch, w_out=w_out, w_up=w_up,
             ffn_conv=ffn_conv, ffn_conv_b=ffn_conv_b, w_down=w_down)
    batch, seq, _ = x.shape
    p['attn_bias_tiles'] = _attn_bias_tiles(rel_bias, min(ATTN_BLOCK, seq))
    x2d = x.reshape(batch * seq, D_MODEL)
    for layer in range(DEPTH):
        x2d = _layer(x2d, batch, seq, layer, p)
    return x2d.reshape(batch, seq, D_MODEL)
```

```python
import functools
import math

import jax
import jax.numpy as jnp
from jax import lax
from jax.experimental import pallas as pl
from jax.experimental.pallas import tpu as pltpu

F32 = jnp.float32
BF16 = jnp.bfloat16
HIGHEST = lax.Precision.HIGHEST

D_MODEL = 2048
DEPTH = 2
BRANCH_WIDTH = 1024
N_BRANCH = 3
M_HEADS = 4
M_HEAD_DIM = 256
M_CONV = 4
DA_HEADS = 4
DA_HEAD_DIM = 128
DA_V_DIM = 256
REL_BUCKETS = 32
REL_MAX_DIST = 128
S5_GROUP = 16
S5_GROUPS = 64
S5_STATE = 64
D_FF = 5632
FFN_CONV = 3
EPS = 1e-6

LANES = 128
SUBLANES = 8
MXU_TILE = 256
VMEM_LIMIT = 56 * 1024 * 1024

N_MAIN = 14336
COL_QM, COL_KM, COL_VM, COL_OM = 0, 1, 2, 3
COL_QD, COL_KD, COL_VD, COL_US = 4, 5, 6, 7
COL_GATE = 8
N_GATE_PAD = LANES

S5_SLABS = 8
S5_SLAB_STATES = 512

NEG_BIG = -1e30
LOG2E = math.log2(math.e)


def _cparams(sem):
    return pltpu.CompilerParams(dimension_semantics=sem, vmem_limit_bytes=VMEM_LIMIT)


def _sigmoid(x):
    return 0.5 * jnp.tanh(0.5 * x) + 0.5


def _gelu_tanh(x):
    c = math.sqrt(2.0 / math.pi)
    return 0.5 * x * (1.0 + jnp.tanh(c * (x + 0.044715 * (x * x * x))))


def _rms(x, gain):
    var = jnp.mean(x * x, axis=-1, keepdims=True)
    return x * lax.rsqrt(var + EPS) * gain


def _in_proj_kernel(x_ref, g_ref, wg_ref, w_ref, o_ref, og_ref, h_scr):
    @pl.when(pl.program_id(1) == 0)
    def _():
        h_scr[...] = _rms(x_ref[...], g_ref[...]).astype(BF16)
        og_ref[...] = jnp.dot(h_scr[...], wg_ref[...], preferred_element_type=F32)

    o_ref[...] = jnp.dot(h_scr[...], w_ref[...],
                         preferred_element_type=F32).astype(BF16)


def _in_proj(x2d, gain, w_gate, w_main, *, tm, tn):
    m = x2d.shape[0]
    return pl.pallas_call(
        _in_proj_kernel,
        out_shape=(jax.ShapeDtypeStruct((m, N_MAIN), BF16),
                   jax.ShapeDtypeStruct((m, N_GATE_PAD), F32)),
        grid=(m // tm, N_MAIN // tn),
        in_specs=[pl.BlockSpec((tm, D_MODEL), lambda i, n: (i, 0)),
                  pl.BlockSpec((1, D_MODEL), lambda i, n: (0, 0)),
                  pl.BlockSpec((D_MODEL, N_GATE_PAD), lambda i, n: (0, 0)),
                  pl.BlockSpec((D_MODEL, tn), lambda i, n: (0, n))],
        out_specs=(pl.BlockSpec((tm, tn), lambda i, n: (i, n)),
                   pl.BlockSpec((tm, N_GATE_PAD), lambda i, n: (i, 0))),
        scratch_shapes=[pltpu.VMEM((tm, D_MODEL), BF16)],
        compiler_params=_cparams(("parallel", "arbitrary")),
        name="in_proj",
    )(x2d, gain, w_gate, w_main)


def _mlstm_kernel(q_ref, k_ref, v_ref, o_ref, gt_ref, gb_ref, cw_ref, ng_ref, y_ref,
                  c_scr, n_scr, m_scr, qe_scr, ke_scr, *, chunk):
    t = chunk
    hd = M_HEAD_DIM
    width = M_HEADS * hd

    @pl.when(pl.program_id(1) == 0)
    def _():
        c_scr[...] = jnp.zeros_like(c_scr)
        n_scr[...] = jnp.zeros_like(n_scr)
        m_scr[...] = jnp.zeros_like(m_scr)
        qe_scr[0:SUBLANES, :] = jnp.zeros((SUBLANES, width), F32)
        ke_scr[0:SUBLANES, :] = jnp.zeros((SUBLANES, width), F32)

    qe_scr[SUBLANES:SUBLANES + t, :] = q_ref[...].astype(F32)
    ke_scr[SUBLANES:SUBLANES + t, :] = k_ref[...].astype(F32)

    gates = gt_ref[...] + gb_ref[...]
    log_f = jnp.minimum(gates, 0.0) - jnp.log1p(jnp.exp(-jnp.abs(gates)))
    row = lax.broadcasted_iota(jnp.int32, (t, t), 0)
    col = lax.broadcasted_iota(jnp.int32, (t, t), 1)
    causal = col <= row
    cum = jnp.dot(causal.astype(F32), log_f, preferred_element_type=F32,
                  precision=HIGHEST)
    gates_t = gates.T
    cum_t = cum.T

    for h in range(M_HEADS):
        sl = slice(h * hd, (h + 1) * hd)
        ksl = slice(width + h * hd, width + (h + 1) * hd)
        qc = jnp.zeros((t, hd), F32)
        kc = jnp.zeros((t, hd), F32)
        for j in range(M_CONV):
            off = SUBLANES - (M_CONV - 1) + j
            qc = qc + cw_ref[j:j + 1, sl] * qe_scr[off:off + t, sl]
            kc = kc + cw_ref[j:j + 1, ksl] * ke_scr[off:off + t, sl]
        qc = qc * _sigmoid(qc)
        kc = kc * _sigmoid(kc) * (hd ** -0.5)
        qb = qc.astype(BF16)
        kb = kc.astype(BF16)
        vb = v_ref[:, sl]

        li_row = gates_t[h:h + 1, :]
        b_row = cum_t[M_HEADS + h:M_HEADS + h + 1, :]
        li_col = gates[:, h:h + 1]
        b_col = cum[:, M_HEADS + h:M_HEADS + h + 1]
        m_prev = m_scr[h:h + 1, 0:1]
        c_prev = c_scr[h]
        n_prev = n_scr[h:h + 1, :]

        dmat = jnp.where(causal, b_col - b_row + li_row, -jnp.inf)
        inter = b_col + m_prev
        m_t = jnp.maximum(inter, jnp.max(dmat, axis=-1, keepdims=True))
        s = lax.dot_general(qb, kb, (((1,), (1,)), ((), ())),
                            preferred_element_type=F32) * jnp.exp(dmat - m_t)
        w_inter = jnp.exp(inter - m_t)
        num = (jnp.dot(s.astype(BF16), vb, preferred_element_type=F32)
               + w_inter * jnp.dot(qb, c_prev.astype(BF16), preferred_element_type=F32))
        den = (jnp.sum(s, axis=-1, keepdims=True)
               + w_inter * jnp.sum(qc * n_prev, axis=-1, keepdims=True))
        hh = num / jnp.maximum(jnp.abs(den), jnp.exp(-m_t))

        g = cum[t - 1:t, M_HEADS + h:M_HEADS + h + 1]
        a_col = g - b_col + li_col
        m_new = jnp.maximum(g + m_prev, jnp.max(a_col, axis=0, keepdims=True))
        ws = jnp.exp(a_col - m_new)
        decay = jnp.exp(g + m_prev - m_new)
        kw = ws * kc
        c_scr[h] = decay * c_prev + lax.dot_general(
            kw.astype(BF16), vb, (((0,), (0,)), ((), ())), preferred_element_type=F32)
        n_scr[h:h + 1, :] = decay * n_prev + jnp.sum(kw, axis=0, keepdims=True)
        m_scr[h:h + 1, :] = jnp.broadcast_to(m_new, (1, LANES))

        hn = _rms(hh, ng_ref[:, sl])
        y_ref[:, sl] = (_sigmoid(o_ref[:, sl].astype(F32)) * hn).astype(BF16)

    qe_scr[0:SUBLANES, :] = qe_scr[t:t + SUBLANES, :]
    ke_scr[0:SUBLANES, :] = ke_scr[t:t + SUBLANES, :]


def _mlstm(proj, gates, gate_bias, conv_w, norm_g, *, batch, seq, chunk):
    nc = seq // chunk
    width = M_HEADS * M_HEAD_DIM

    def col_spec(cb):
        return pl.BlockSpec((chunk, width), lambda b, c: (b * nc + c, cb))

    return pl.pallas_call(
        functools.partial(_mlstm_kernel, chunk=chunk),
        out_shape=jax.ShapeDtypeStruct((batch * seq, width), BF16),
        grid=(batch, nc),
        in_specs=[col_spec(COL_QM), col_spec(COL_KM), col_spec(COL_VM), col_spec(COL_OM),
                  pl.BlockSpec((chunk, N_GATE_PAD), lambda b, c: (b * nc + c, 0)),
                  pl.BlockSpec((1, N_GATE_PAD), lambda b, c: (0, 0)),
                  pl.BlockSpec((M_CONV, 2 * width), lambda b, c: (0, 0)),
                  pl.BlockSpec((1, width), lambda b, c: (0, 0))],
        out_specs=pl.BlockSpec((chunk, width), lambda b, c: (b * nc + c, 0)),
        scratch_shapes=[pltpu.VMEM((M_HEADS, M_HEAD_DIM, M_HEAD_DIM), F32),
                        pltpu.VMEM((M_HEADS, M_HEAD_DIM), F32),
                        pltpu.VMEM((M_HEADS, LANES), F32),
                        pltpu.VMEM((chunk + SUBLANES, width), F32),
                        pltpu.VMEM((chunk + SUBLANES, width), F32)],
        compiler_params=_cparams(("parallel", "arbitrary")),
        name="mlstm",
    )(proj, proj, proj, proj, gates, gate_bias, conv_w, norm_g)


def _attn_kernel(sc_ref, q_ref, k_ref, v_ref, bias_ref, ng_ref, y_ref,
                 m_scr, l_scr, acc_scr, *, blk, out_scale):
    t = blk
    d = DA_HEAD_DIM
    h = pl.program_id(1)
    i = pl.program_id(2)
    lam = sc_ref[0]
    far_bias = sc_ref[1 + h]

    qs = (q_ref[...].astype(F32) * (d ** -0.5 * LOG2E)).astype(BF16)

    def block_step(j, bias, far=False, first=False):
        start = pl.multiple_of(j * t, t)
        kb = k_ref[pl.ds(start, t), :]
        vb = v_ref[pl.ds(start, t), :]
        for c in range(2):
            s = lax.dot_general(kb[:, c * d:(c + 1) * d], qs[:, c * d:(c + 1) * d],
                                (((1,), (1,)), ((), ())), preferred_element_type=F32)
            if far:
                m_blk = jnp.max(s, axis=0, keepdims=True) + bias
            else:
                s = s + bias
                m_blk = jnp.max(s, axis=0, keepdims=True)
            m_new = m_blk if first else jnp.maximum(m_scr[c], m_blk)
            p = jnp.exp2(s - ((m_new - bias) if far else m_new))
            l_blk = jnp.sum(p, axis=0, keepdims=True)
            pv = lax.dot_general(vb, p.astype(BF16), (((0,), (0,)), ((), ())),
                                 preferred_element_type=F32)
            if first:
                l_scr[c] = l_blk
                acc_scr[c] = pv
            else:
                alpha = jnp.exp2(m_scr[c] - m_new)
                l_scr[c] = alpha * l_scr[c] + l_blk
                acc_scr[c] = alpha * acc_scr[c] + pv
            m_scr[c] = m_new

    block_step(i, bias_ref[0, 0], first=True)

    @pl.when(i >= 1)
    def _():
        block_step(i - 1, bias_ref[0, 1])

    def far_body(j, carry):
        block_step(j, far_bias, far=True)
        return carry

    lax.fori_loop(0, jnp.maximum(i - 1, 0), far_body, 0)

    out_t = acc_scr[0] * (1.0 / l_scr[0]) - lam * (acc_scr[1] * (1.0 / l_scr[1]))
    y_ref[...] = (_rms(out_t.T, ng_ref[...]) * out_scale).astype(BF16)


def _diff_attn(scalars, proj, bias_tiles, norm_g, *, batch, seq, blk, out_scale):
    nq = seq // blk
    kvw = DA_V_DIM
    return pl.pallas_call(
        functools.partial(_attn_kernel, blk=blk, out_scale=out_scale),
        out_shape=jax.ShapeDtypeStruct((batch * seq, DA_HEADS * DA_V_DIM), BF16),
        grid=(batch, DA_HEADS, nq),
        in_specs=[pl.BlockSpec(memory_space=pltpu.SMEM),
                  pl.BlockSpec((blk, kvw), lambda b, h, i: (b * nq + i, COL_QD * 4 + h)),
                  pl.BlockSpec((seq, kvw), lambda b, h, i: (b, COL_KD * 4 + h)),
                  pl.BlockSpec((seq, kvw), lambda b, h, i: (b, COL_VD * 4 + h)),
                  pl.BlockSpec((1, 2, blk, blk), lambda b, h, i: (h, 0, 0, 0)),
                  pl.BlockSpec((1, kvw), lambda b, h, i: (0, h))],
        out_specs=pl.BlockSpec((blk, kvw), lambda b, h, i: (b * nq + i, h)),
        scratch_shapes=[pltpu.VMEM((2, 1, blk), F32),
                        pltpu.VMEM((2, 1, blk), F32),
                        pltpu.VMEM((2, DA_V_DIM, blk), F32)],
        compiler_params=_cparams(("parallel", "parallel", "arbitrary")),
        name="diff_attn",
    )(scalars, proj, proj, proj, bias_tiles, norm_g)


def _s5_kernel(u_ref, wb_ref, wc_ref, are_ref, aim_ref, d_ref, wg_ref, y_ref,
               xre_scr, xim_scr, bu_scr, uil_scr, yil_scr, *, steps, batch):
    ns = S5_SLAB_STATES

    @pl.when(pl.program_id(0) == 0)
    def _():
        xre_scr[...] = jnp.zeros_like(xre_scr)
        xim_scr[...] = jnp.zeros_like(xim_scr)

    for b in range(batch):
        ub = u_ref[b].astype(F32)
        for c in range(S5_SLABS):
            uil_scr.at[c][pl.ds(b, steps, stride=batch), :] = ub[:, c * LANES:(c + 1) * LANES]

    ys = []
    for s in range(S5_SLABS):
        us = uil_scr[s]
        buf = bu_scr.at[s % 2]
        buf[...] = jnp.dot(us.astype(BF16), wb_ref[s], preferred_element_type=F32)
        a_re = jnp.broadcast_to(are_ref[s], (batch, ns))
        a_im = jnp.broadcast_to(aim_ref[s], (batch, ns))

        x_re = xre_scr[s]
        x_im = xim_scr[s]
        for tt in range(steps):
            rs = slice(tt * batch, (tt + 1) * batch)
            n_re = a_re * x_re - a_im * x_im + buf[rs, 0:ns]
            n_im = a_re * x_im + a_im * x_re + buf[rs, ns:2 * ns]
            buf[rs, 0:ns] = n_re
            buf[rs, ns:2 * ns] = n_im
            x_re, x_im = n_re, n_im
        xre_scr[s] = x_re
        xim_scr[s] = x_im
        y = (jnp.dot(buf[...].astype(BF16), wc_ref[s], preferred_element_type=F32)
             + d_ref[:, s * LANES:(s + 1) * LANES] * us)
        ys.append(_gelu_tanh(y).astype(BF16))

    yb = jnp.concatenate(ys, axis=1)
    half = BRANCH_WIDTH
    a = jnp.dot(yb, wg_ref[:, 0:half], preferred_element_type=F32)
    g = jnp.dot(yb, wg_ref[:, half:2 * half], preferred_element_type=F32)
    out = a * _sigmoid(g)
    for c in range(S5_SLABS):
        yil_scr[c] = out[:, c * LANES:(c + 1) * LANES]
    for b in range(batch):
        for c in range(S5_SLABS):
            y_ref[b, :, c * LANES:(c + 1) * LANES] = (
                yil_scr.at[c][pl.ds(b, steps, stride=batch), :].astype(BF16))


def _s5(proj3d, wb, wc, a_re, a_im, d_skip, w_glu, *, steps):
    batch, seq, _ = proj3d.shape
    rows = steps * batch
    ns = S5_SLAB_STATES
    const3 = lambda i: (0, 0, 0)
    return pl.pallas_call(
        functools.partial(_s5_kernel, steps=steps, batch=batch),
        out_shape=jax.ShapeDtypeStruct((batch, seq, BRANCH_WIDTH), BF16),
        grid=(seq // steps,),
        in_specs=[pl.BlockSpec((batch, steps, BRANCH_WIDTH), lambda i: (0, i, COL_US)),
                  pl.BlockSpec((S5_SLABS, LANES, 2 * ns), const3),
                  pl.BlockSpec((S5_SLABS, 2 * ns, LANES), const3),
                  pl.BlockSpec((S5_SLABS, 1, ns), const3),
                  pl.BlockSpec((S5_SLABS, 1, ns), const3),
                  pl.BlockSpec((1, BRANCH_WIDTH), lambda i: (0, 0)),
                  pl.BlockSpec((BRANCH_WIDTH, 2 * BRANCH_WIDTH), lambda i: (0, 0))],
        out_specs=pl.BlockSpec((batch, steps, BRANCH_WIDTH), lambda i: (0, i, 0)),
        scratch_shapes=[pltpu.VMEM((S5_SLABS, batch, ns), F32),
                        pltpu.VMEM((S5_SLABS, batch, ns), F32),
                        pltpu.VMEM((2, rows, 2 * ns), F32),
                        pltpu.VMEM((S5_SLABS, rows, LANES), F32),
                        pltpu.VMEM((S5_SLABS, rows, LANES), F32)],
        compiler_params=_cparams(("arbitrary",)),
        name="s5",
    )(proj3d, wb, wc, a_re, a_im, d_skip, w_glu)


def _merge_kernel(ya_ref, yb_ref, yc_ref, g0_ref, g1_ref, g2_ref, x_ref, wbr_ref,
                  wo_ref, ng_ref, o_ref):
    merged = None
    for n, (y_ref, g_ref) in enumerate(((ya_ref, g0_ref), (yb_ref, g1_ref),
                                        (yc_ref, g2_ref))):
        z = jnp.dot(y_ref[...], wbr_ref[n], preferred_element_type=F32)
        term = _sigmoid(g_ref[...].astype(F32)) * z
        merged = term if merged is None else merged + term
    mix = jnp.dot(merged.astype(BF16), wo_ref[...], preferred_element_type=F32)
    o_ref[...] = x_ref[...] + _rms(mix, ng_ref[...])


def _merge(y_a, y_b, y_c, proj, x2d, w_branch, w_out, norm_g, *, tm):
    m = x2d.shape[0]
    row = lambda i: (i, 0)

    def gate_spec(n):
        return pl.BlockSpec((tm, D_MODEL), lambda i: (i, COL_GATE // 2 + n))

    return pl.pallas_call(
        _merge_kernel,
        out_shape=jax.ShapeDtypeStruct((m, D_MODEL), F32),
        grid=(m // tm,),
        in_specs=[pl.BlockSpec((tm, BRANCH_WIDTH), row),
                  pl.BlockSpec((tm, BRANCH_WIDTH), row),
                  pl.BlockSpec((tm, BRANCH_WIDTH), row),
                  gate_spec(0), gate_spec(1), gate_spec(2),
                  pl.BlockSpec((tm, D_MODEL), row),
                  pl.BlockSpec((N_BRANCH, BRANCH_WIDTH, D_MODEL), lambda i: (0, 0, 0),
                               pipeline_mode=pl.Buffered(1)),
                  pl.BlockSpec((D_MODEL, D_MODEL), lambda i: (0, 0),
                               pipeline_mode=pl.Buffered(1)),
                  pl.BlockSpec((1, D_MODEL), lambda i: (0, 0))],
        out_specs=pl.BlockSpec((tm, D_MODEL), row),
        compiler_params=_cparams(("parallel",)),
        name="merge",
    )(y_a, y_b, y_c, proj, proj, proj, x2d, w_branch, w_out, norm_g)


def _ffn_kernel(x_ref, g_ref, wa_ref, wv_ref, cwa_ref, cwv_ref, cba_ref,
                cbv_ref, wd_ref, ng_ref, o_ref, h_scr, tail_scr, *, tl):
    i = pl.program_id(1)
    f = pl.program_id(2)
    nf = pl.num_programs(2)
    hal = SUBLANES

    @pl.when(f == 0)
    def _():
        h_scr[...] = _rms(x_ref[0], g_ref[...]).astype(BF16)
        o_ref[0] = jnp.zeros((tl, D_MODEL), F32)

    @pl.when(i == 0)
    def _():
        tail_scr[f] = jnp.zeros(tail_scr.shape[1:], F32)

    hb = h_scr[...]

    def conv(part, w_ref, cw_ref, cb_ref):
        up = jnp.dot(hb, w_ref[...], preferred_element_type=F32)
        ext = jnp.concatenate([tail_scr[f, part], up], axis=0)
        tail_scr[f, part] = up[tl - hal:tl, :]
        out = cb_ref[...] + cw_ref[FFN_CONV - 1:FFN_CONV, :] * up
        for j in range(FFN_CONV - 1):
            off = hal - (FFN_CONV - 1) + j
            out = out + cw_ref[j:j + 1, :] * ext[off:off + tl, :]
        return out

    a = conv(0, wa_ref, cwa_ref, cba_ref)
    v = conv(1, wv_ref, cwv_ref, cbv_ref)
    act = (_gelu_tanh(a) * v).astype(BF16)
    o_ref[0] += jnp.dot(act, wd_ref[...], preferred_element_type=F32)

    @pl.when(f == nf - 1)
    def _():
        o_ref[0] = x_ref[0] + _rms(o_ref[0], ng_ref[...])


def _ffn(x3d, gain, w_up, conv_w, conv_b, w_down, norm_g, *, tl, tf):
    batch, seq, _ = x3d.shape
    nfb = D_FF // tf
    hal = SUBLANES
    return pl.pallas_call(
        functools.partial(_ffn_kernel, tl=tl),
        out_shape=jax.ShapeDtypeStruct((batch, seq, D_MODEL), F32),
        grid=(batch, seq // tl, nfb),
        in_specs=[pl.BlockSpec((1, tl, D_MODEL), lambda b, i, f: (b, i, 0)),
                  pl.BlockSpec((1, D_MODEL), lambda b, i, f: (0, 0)),
                  pl.BlockSpec((D_MODEL, tf), lambda b, i, f: (0, f)),
                  pl.BlockSpec((D_MODEL, tf), lambda b, i, f: (0, nfb + f)),
                  pl.BlockSpec((FFN_CONV, tf), lambda b, i, f: (0, f)),
                  pl.BlockSpec((FFN_CONV, tf), lambda b, i, f: (0, nfb + f)),
                  pl.BlockSpec((1, tf), lambda b, i, f: (0, f)),
                  pl.BlockSpec((1, tf), lambda b, i, f: (0, nfb + f)),
                  pl.BlockSpec((tf, D_MODEL), lambda b, i, f: (f, 0)),
                  pl.BlockSpec((1, D_MODEL), lambda b, i, f: (0, 0))],
        out_specs=pl.BlockSpec((1, tl, D_MODEL), lambda b, i, f: (b, i, 0)),
        scratch_shapes=[pltpu.VMEM((tl, D_MODEL), BF16),
                        pltpu.VMEM((nfb, 2, hal, tf), F32)],
        compiler_params=_cparams(("parallel", "arbitrary", "arbitrary")),
        name="conv_ffn",
    )(x3d, gain, w_up, w_up, conv_w, conv_w, conv_b, conv_b, w_down, norm_g)


def _t5_bucket(dist):
    n = jnp.maximum(dist, 0)
    max_exact = REL_BUCKETS // 2
    nf = jnp.maximum(n, 1).astype(F32)
    large = max_exact + (jnp.log(nf / max_exact) / math.log(REL_MAX_DIST / max_exact)
                         * (REL_BUCKETS - max_exact)).astype(jnp.int32)
    large = jnp.minimum(large, REL_BUCKETS - 1)
    return jnp.where(n < max_exact, n, large)


def _attn_bias_tiles(rel_bias, blk):
    qi = jnp.arange(blk, dtype=jnp.int32)[None, :]
    kj = jnp.arange(blk, dtype=jnp.int32)[:, None]
    table = rel_bias.astype(F32) * LOG2E
    last = table[REL_BUCKETS - 1][:, None, None]
    tiles = []
    for off in (0, blk):
        dist = qi - kj + off
        bucket = _t5_bucket(dist)
        bias = jnp.broadcast_to(last, (DA_HEADS, blk, blk))
        for b in range(REL_BUCKETS - 1):
            bias = jnp.where(bucket == b, table[b][:, None, None], bias)
        tiles.append(jnp.where(dist >= 0, bias, NEG_BIG))
    return jnp.stack(tiles, axis=1)


def _s5_params(lam_re, lam_im, log_dt, b_re, b_im, c_re, c_im):
    dt = jnp.exp(log_dt)[:, None]
    mag = jnp.exp(lam_re * dt)
    a_re = mag * jnp.cos(lam_im * dt)
    a_im = mag * jnp.sin(lam_im * dt)
    den = lam_re * lam_re + lam_im * lam_im
    z_re = ((a_re - 1.0) * lam_re + a_im * lam_im) / den
    z_im = (a_im * lam_re - (a_re - 1.0) * lam_im) / den
    bb_re = z_re[..., None] * b_re - z_im[..., None] * b_im
    bb_im = z_re[..., None] * b_im + z_im[..., None] * b_re
    gs = S5_GROUPS // S5_SLABS
    eye = jnp.eye(gs, dtype=F32)

    def in_blocks(bb):
        bb = bb.reshape(S5_SLABS, gs, S5_STATE, S5_GROUP)
        w = jnp.einsum('sgpc,gh->sgchp', bb, eye)
        return w.reshape(S5_SLABS, gs * S5_GROUP, gs * S5_STATE)

    def out_blocks(cc):
        cc = cc.reshape(S5_SLABS, gs, S5_GROUP, S5_STATE)
        w = jnp.einsum('sgcp,gh->sgphc', cc, eye)
        return w.reshape(S5_SLABS, gs * S5_STATE, gs * S5_GROUP)

    wb = jnp.concatenate([in_blocks(bb_re), in_blocks(bb_im)], axis=-1).astype(BF16)
    wc = jnp.concatenate([out_blocks(c_re), out_blocks(-c_im)], axis=-2).astype(BF16)
    a_re = a_re.reshape(S5_SLABS, 1, S5_SLAB_STATES)
    a_im = a_im.reshape(S5_SLABS, 1, S5_SLAB_STATES)
    return wb, wc, a_re, a_im


IN_PROJ_TM, IN_PROJ_TN = 1024, 2048
MLSTM_CHUNK = 128
ATTN_BLOCK = 512
S5_STEPS = 32
MERGE_TM = 256
FFN_TL, FFN_TF = 512, 512


def _layer(x2d, batch, seq, layer, p):
    w_in = p['w_in'][layer]
    n_if = 2 * M_HEADS
    split = 4 * M_HEADS * M_HEAD_DIM
    w_main = jnp.concatenate([w_in[:, :split], w_in[:, split + n_if:]], axis=1).astype(BF16)
    w_gate = jnp.pad(w_in[:, split:split + n_if],
                     ((0, 0), (0, N_GATE_PAD - n_if))).astype(BF16)
    proj, gates = _in_proj(x2d, p['norm_mix_pre'][layer][None, :], w_gate, w_main,
                           tm=min(IN_PROJ_TM, batch * seq), tn=IN_PROJ_TN)

    gate_bias = jnp.pad(p['mlstm_b_if'][layer].reshape(1, n_if),
                        ((0, 0), (0, N_GATE_PAD - n_if)))
    y_a = _mlstm(proj, gates, gate_bias, p['mlstm_conv'][layer],
                 p['mlstm_norm'][layer][None, :], batch=batch, seq=seq,
                 chunk=min(MLSTM_CHUNK, seq))

    lambda_init = 0.8 - 0.6 * math.exp(-0.3 * layer)
    lam = p['diff_lambda'][layer]
    lam_full = (jnp.exp(jnp.sum(lam[0] * lam[1])) - jnp.exp(jnp.sum(lam[2] * lam[3]))
                + lambda_init)
    blk = min(ATTN_BLOCK, seq)
    scalars = jnp.concatenate(
        [lam_full[None], p['rel_bias'][REL_BUCKETS - 1, :] * LOG2E]).astype(F32)
    y_b = _diff_attn(scalars, proj, p['attn_bias_tiles'],
                     p['diff_norm'][layer][None, :], batch=batch, seq=seq, blk=blk,
                     out_scale=1.0 - lambda_init)

    wb, wc, a_re, a_im = _s5_params(
        p['s5_lambda_re'][layer], p['s5_lambda_im'][layer], p['s5_log_dt'][layer],
        p['s5_b_re'][layer], p['s5_b_im'][layer], p['s5_c_re'][layer], p['s5_c_im'][layer])
    y_c = _s5(proj.reshape(batch, seq, N_MAIN), wb, wc, a_re, a_im,
              p['s5_d'][layer][None, :], p['s5_w_glu'][layer].astype(BF16),
              steps=min(S5_STEPS, seq)).reshape(batch * seq, BRANCH_WIDTH)

    x2d = _merge(y_a, y_b, y_c, proj, x2d, p['w_branch'][layer].astype(BF16),
                 p['w_out'][layer].astype(BF16), p['norm_mix_post'][layer][None, :],
                 tm=MERGE_TM)

    x3d = _ffn(x2d.reshape(batch, seq, D_MODEL), p['norm_ffn_pre'][layer][None, :],
               p['w_up'][layer].astype(BF16), p['ffn_conv'][layer],
               p['ffn_conv_b'][layer][None, :], p['w_down'][layer].astype(BF16),
               p['norm_ffn_post'][layer][None, :], tl=min(FFN_TL, seq), tf=FFN_TF)
    return x3d.reshape(batch * seq, D_MODEL)


def kernel(x, norm_mix_pre, norm_mix_post, norm_ffn_pre, norm_ffn_post, w_in, mlstm_b_if, mlstm_conv, mlstm_norm, diff_lambda, diff_norm, rel_bias, s5_lambda_re, s5_lambda_im, s5_log_dt, s5_b_re, s5_b_im, s5_c_re, s5_c_im, s5_d, s5_w_glu, w_branch, w_out, w_up, ffn_conv, ffn_conv_b, w_down):
    p = dict(norm_mix_pre=norm_mix_pre, norm_mix_post=norm_mix_post,
             norm_ffn_pre=norm_ffn_pre, norm_ffn_post=norm_ffn_post, w_in=w_in,
             mlstm_b_if=mlstm_b_if, mlstm_conv=mlstm_conv, mlstm_norm=mlstm_norm,
             diff_lambda=diff_lambda, diff_norm=diff_norm, rel_bias=rel_bias,
             s5_lambda_re=s5_lambda_re, s5_lambda_im=s5_lambda_im, s5_log_dt=s5_log_dt,
             s5_b_re=s5_b_re, s5_b_im=s5_b_im, s5_c_re=s5_c_re, s5_c_im=s5_c_im,
             s5_d=s5_d, s5_w_glu=s5_w_glu, w_branch=w_branch, w_out=w_out, w_up=w_up,
             ffn_conv=ffn_conv, ffn_conv_b=ffn_conv_b, w_down=w_down)
    batch, seq, _ = x.shape
    p['attn_bias_tiles'] = _attn_bias_tiles(rel_bias, min(ATTN_BLOCK, seq))
    x2d = x.reshape(batch * seq, D_MODEL)
    for layer in range(DEPTH):
        x2d = _layer(x2d, batch, seq, layer, p)
    return x2d.reshape(batch, seq, D_MODEL)
```

```python
import functools
import math

import jax
import jax.numpy as jnp
from jax import lax
from jax.experimental import pallas as pl
from jax.experimental.pallas import tpu as pltpu

F32 = jnp.float32
BF16 = jnp.bfloat16
HIGHEST = lax.Precision.HIGHEST

D_MODEL = 2048
DEPTH = 2
BRANCH_WIDTH = 1024
N_BRANCH = 3
M_HEADS = 4
M_HEAD_DIM = 256
M_CONV = 4
DA_HEADS = 4
DA_HEAD_DIM = 128
DA_V_DIM = 256
REL_BUCKETS = 32
REL_MAX_DIST = 128
S5_GROUP = 16
S5_GROUPS = 64
S5_STATE = 64
D_FF = 5632
FFN_CONV = 3
EPS = 1e-6

LANES = 128
SUBLANES = 8
MXU_TILE = 256
VMEM_LIMIT = 56 * 1024 * 1024

N_MAIN = 14336
COL_QM, COL_KM, COL_VM, COL_OM = 0, 1, 2, 3
COL_QD, COL_KD, COL_VD, COL_US = 4, 5, 6, 7
COL_GATE = 8
N_GATE_PAD = LANES

S5_SLABS = 8
S5_SLAB_STATES = 512

NEG_BIG = -1e30
LOG2E = math.log2(math.e)


def _cparams(sem):
    return pltpu.CompilerParams(dimension_semantics=sem, vmem_limit_bytes=VMEM_LIMIT)


def _sigmoid(x):
    return 0.5 * jnp.tanh(0.5 * x) + 0.5


def _gelu_tanh(x):
    c = math.sqrt(2.0 / math.pi)
    return 0.5 * x * (1.0 + jnp.tanh(c * (x + 0.044715 * (x * x * x))))


def _rms(x, gain):
    var = jnp.mean(x * x, axis=-1, keepdims=True)
    return x * lax.rsqrt(var + EPS) * gain


def _in_proj_kernel(x_ref, g_ref, wg_ref, w_ref, o_ref, og_ref, h_scr):
    @pl.when(pl.program_id(1) == 0)
    def _():
        h_scr[...] = _rms(x_ref[...], g_ref[...]).astype(BF16)
        og_ref[...] = jnp.dot(h_scr[...], wg_ref[...], preferred_element_type=F32)

    o_ref[...] = jnp.dot(h_scr[...], w_ref[...],
                         preferred_element_type=F32).astype(BF16)


def _in_proj(x2d, gain, w_gate, w_main, *, tm, tn):
    m = x2d.shape[0]
    return pl.pallas_call(
        _in_proj_kernel,
        out_shape=(jax.ShapeDtypeStruct((m, N_MAIN), BF16),
                   jax.ShapeDtypeStruct((m, N_GATE_PAD), F32)),
        grid=(m // tm, N_MAIN // tn),
        in_specs=[pl.BlockSpec((tm, D_MODEL), lambda i, n: (i, 0)),
                  pl.BlockSpec((1, D_MODEL), lambda i, n: (0, 0)),
                  pl.BlockSpec((D_MODEL, N_GATE_PAD), lambda i, n: (0, 0)),
                  pl.BlockSpec((D_MODEL, tn), lambda i, n: (0, n))],
        out_specs=(pl.BlockSpec((tm, tn), lambda i, n: (i, n)),
                   pl.BlockSpec((tm, N_GATE_PAD), lambda i, n: (i, 0))),
        scratch_shapes=[pltpu.VMEM((tm, D_MODEL), BF16)],
        compiler_params=_cparams(("parallel", "arbitrary")),
        name="in_proj",
    )(x2d, gain, w_gate, w_main)


def _mlstm_kernel(q_ref, k_ref, v_ref, o_ref, gt_ref, gb_ref, cw_ref, ng_ref, y_ref,
                  c_scr, n_scr, m_scr, qe_scr, ke_scr, *, chunk):
    t = chunk
    hd = M_HEAD_DIM
    width = M_HEADS * hd

    @pl.when(pl.program_id(1) == 0)
    def _():
        c_scr[...] = jnp.zeros_like(c_scr)
        n_scr[...] = jnp.zeros_like(n_scr)
        m_scr[...] = jnp.zeros_like(m_scr)
        qe_scr[0:SUBLANES, :] = jnp.zeros((SUBLANES, width), F32)
        ke_scr[0:SUBLANES, :] = jnp.zeros((SUBLANES, width), F32)

    qe_scr[SUBLANES:SUBLANES + t, :] = q_ref[...].astype(F32)
    ke_scr[SUBLANES:SUBLANES + t, :] = k_ref[...].astype(F32)

    gates = gt_ref[...] + gb_ref[...]
    log_f = jnp.minimum(gates, 0.0) - jnp.log1p(jnp.exp(-jnp.abs(gates)))
    row = lax.broadcasted_iota(jnp.int32, (t, t), 0)
    col = lax.broadcasted_iota(jnp.int32, (t, t), 1)
    causal = col <= row
    cum = jnp.dot(causal.astype(F32), log_f, preferred_element_type=F32,
                  precision=HIGHEST)
    gates_t = gates.T
    cum_t = cum.T

    for h in range(M_HEADS):
        sl = slice(h * hd, (h + 1) * hd)
        ksl = slice(width + h * hd, width + (h + 1) * hd)
        qc = jnp.zeros((t, hd), F32)
        kc = jnp.zeros((t, hd), F32)
        for j in range(M_CONV):
            off = SUBLANES - (M_CONV - 1) + j
            qc = qc + cw_ref[j:j + 1, sl] * qe_scr[off:off + t, sl]
            kc = kc + cw_ref[j:j + 1, ksl] * ke_scr[off:off + t, sl]
        qc = qc * _sigmoid(qc)
        kc = kc * _sigmoid(kc) * (hd ** -0.5)
        qb = qc.astype(BF16)
        kb = kc.astype(BF16)
        vb = v_ref[:, sl]

        li_row = gates_t[h:h + 1, :]
        b_row = cum_t[M_HEADS + h:M_HEADS + h + 1, :]
        li_col = gates[:, h:h + 1]
        b_col = cum[:, M_HEADS + h:M_HEADS + h + 1]
        m_prev = m_scr[h:h + 1, 0:1]
        c_prev = c_scr[h]
        n_prev = n_scr[h:h + 1, :]

        dmat = jnp.where(causal, b_col - b_row + li_row, -jnp.inf)
        inter = b_col + m_prev
        m_t = jnp.maximum(inter, jnp.max(dmat, axis=-1, keepdims=True))
        s = lax.dot_general(qb, kb, (((1,), (1,)), ((), ())),
                            preferred_element_type=F32) * jnp.exp(dmat - m_t)
        w_inter = jnp.exp(inter - m_t)
        num = (jnp.dot(s.astype(BF16), vb, preferred_element_type=F32)
               + w_inter * jnp.dot(qb, c_prev.astype(BF16), preferred_element_type=F32))
        den = (jnp.sum(s, axis=-1, keepdims=True)
               + w_inter * jnp.sum(qc * n_prev, axis=-1, keepdims=True))
        hh = num / jnp.maximum(jnp.abs(den), jnp.exp(-m_t))

        g = cum[t - 1:t, M_HEADS + h:M_HEADS + h + 1]
        a_col = g - b_col + li_col
        m_new = jnp.maximum(g + m_prev, jnp.max(a_col, axis=0, keepdims=True))
        ws = jnp.exp(a_col - m_new)
        decay = jnp.exp(g + m_prev - m_new)
        kw = ws * kc
        c_scr[h] = decay * c_prev + lax.dot_general(
            kw.astype(BF16), vb, (((0,), (0,)), ((), ())), preferred_element_type=F32)
        n_scr[h:h + 1, :] = decay * n_prev + jnp.sum(kw, axis=0, keepdims=True)
        m_scr[h:h + 1, :] = jnp.broadcast_to(m_new, (1, LANES))

        hn = _rms(hh, ng_ref[:, sl])
        y_ref[:, sl] = (_sigmoid(o_ref[:, sl].astype(F32)) * hn).astype(BF16)

    qe_scr[0:SUBLANES, :] = qe_scr[t:t + SUBLANES, :]
    ke_scr[0:SUBLANES, :] = ke_scr[t:t + SUBLANES, :]


def _mlstm(proj, gates, gate_bias, conv_w, norm_g, *, batch, seq, chunk):
    nc = seq // chunk
    width = M_HEADS * M_HEAD_DIM

    def col_spec(cb):
        return pl.BlockSpec((chunk, width), lambda b, c: (b * nc + c, cb))

    return pl.pallas_call(
        functools.partial(_mlstm_kernel, chunk=chunk),
        out_shape=jax.ShapeDtypeStruct((batch * seq, width), BF16),
        grid=(batch, nc),
        in_specs=[col_spec(COL_QM), col_spec(COL_KM), col_spec(COL_VM), col_spec(COL_OM),
                  pl.BlockSpec((chunk, N_GATE_PAD), lambda b, c: (b * nc + c, 0)),
                  pl.BlockSpec((1, N_GATE_PAD), lambda b, c: (0, 0)),
                  pl.BlockSpec((M_CONV, 2 * width), lambda b, c: (0, 0)),
                  pl.BlockSpec((1, width), lambda b, c: (0, 0))],
        out_specs=pl.BlockSpec((chunk, width), lambda b, c: (b * nc + c, 0)),
        scratch_shapes=[pltpu.VMEM((M_HEADS, M_HEAD_DIM, M_HEAD_DIM), F32),
                        pltpu.VMEM((M_HEADS, M_HEAD_DIM), F32),
                        pltpu.VMEM((M_HEADS, LANES), F32),
                        pltpu.VMEM((chunk + SUBLANES, width), F32),
                        pltpu.VMEM((chunk + SUBLANES, width), F32)],
        compiler_params=_cparams(("parallel", "arbitrary")),
        name="mlstm",
    )(proj, proj, proj, proj, gates, gate_bias, conv_w, norm_g)


def _attn_kernel(sc_ref, q_ref, k_ref, v_ref, bias_ref, ng_ref, y_ref,
                 m_scr, l_scr, acc_scr, vt_scr, s_scr, *, blk, heads, out_scale):
    t = blk
    d = DA_HEAD_DIM
    w = DA_V_DIM
    h0 = pl.program_id(1) * heads
    i = pl.program_id(2)
    lam = sc_ref[0]

    qs = (q_ref[...].astype(F32) * (d ** -0.5 * LOG2E)).astype(BF16)

    @pl.when(i == 0)
    def _():
        vt_scr[...] = v_ref[...].astype(F32).T.astype(BF16)

    def block_step(j, bias_tile=None, first=False):
        start = pl.multiple_of(j * t, t)
        kb = k_ref[pl.ds(start, t), :]
        vtb = vt_scr[:, pl.ds(start, t)]
        far = bias_tile is None
        for ci in range(2 * heads):
            cols = slice(ci * d, (ci + 1) * d)
            s_scr[ci] = lax.dot_general(kb[:, cols], qs[:, cols], (((1,), (1,)), ((), ())),
                                        preferred_element_type=F32)
        for hh in range(heads):
            bias = sc_ref[1 + h0 + hh] if far else bias_ref[hh, bias_tile]
            for c in range(2):
                ci = 2 * hh + c
                s = s_scr[ci]
                if far:
                    m_blk = jnp.max(s, axis=0, keepdims=True) + bias
                else:
                    s = s + bias
                    m_blk = jnp.max(s, axis=0, keepdims=True)
                m_new = m_blk if first else jnp.maximum(m_scr[ci], m_blk)
                p = jnp.exp2(s - ((m_new - bias) if far else m_new))
                l_blk = jnp.sum(p, axis=0, keepdims=True)
                pv = jnp.dot(vtb[hh * w:(hh + 1) * w, :], p.astype(BF16),
                             preferred_element_type=F32)
                if first:
                    l_scr[ci] = l_blk
                    acc_scr[ci] = pv
                else:
                    alpha = jnp.exp2(m_scr[ci] - m_new)
                    l_scr[ci] = alpha * l_scr[ci] + l_blk
                    acc_scr[ci] = alpha * acc_scr[ci] + pv
                m_scr[ci] = m_new

    block_step(i, 0, first=True)

    @pl.when(i >= 1)
    def _():
        block_step(i - 1, 1)

    def far_body(j, carry):
        block_step(j)
        return carry

    lax.fori_loop(0, jnp.maximum(i - 1, 0), far_body, 0)

    for hh in range(heads):
        ca, cb = 2 * hh, 2 * hh + 1
        out_t = (acc_scr[ca] * (1.0 / l_scr[ca])
                 - lam * (acc_scr[cb] * (1.0 / l_scr[cb])))
        hs = slice(hh * w, (hh + 1) * w)
        y_ref[:, hs] = (_rms(out_t.T, ng_ref[:, hs]) * out_scale).astype(BF16)


def _diff_attn(scalars, proj, bias_tiles, norm_g, *, batch, seq, blk, heads, out_scale):
    nq = seq // blk
    kvw = heads * DA_V_DIM
    per_row = BRANCH_WIDTH // kvw
    return pl.pallas_call(
        functools.partial(_attn_kernel, blk=blk, heads=heads, out_scale=out_scale),
        out_shape=jax.ShapeDtypeStruct((batch * seq, DA_HEADS * DA_V_DIM), BF16),
        grid=(batch, DA_HEADS // heads, nq),
        in_specs=[pl.BlockSpec(memory_space=pltpu.SMEM),
                  pl.BlockSpec((blk, kvw), lambda b, h, i: (b * nq + i, COL_QD * per_row + h)),
                  pl.BlockSpec((seq, kvw), lambda b, h, i: (b, COL_KD * per_row + h)),
                  pl.BlockSpec((seq, kvw), lambda b, h, i: (b, COL_VD * per_row + h)),
                  pl.BlockSpec((heads, 2, blk, blk), lambda b, h, i: (h, 0, 0, 0)),
                  pl.BlockSpec((1, kvw), lambda b, h, i: (0, h))],
        out_specs=pl.BlockSpec((blk, kvw), lambda b, h, i: (b * nq + i, h)),
        scratch_shapes=[pltpu.VMEM((2 * heads, 1, blk), F32),
                        pltpu.VMEM((2 * heads, 1, blk), F32),
                        pltpu.VMEM((2 * heads, DA_V_DIM, blk), F32),
                        pltpu.VMEM((kvw, seq), BF16),
                        pltpu.VMEM((2 * heads, blk, blk), F32)],
        compiler_params=_cparams(("parallel", "parallel", "arbitrary")),
        name="diff_attn",
    )(scalars, proj, proj, proj, bias_tiles, norm_g)


def _s5_kernel(u_ref, wb_ref, wc_ref, are_ref, aim_ref, d_ref, wg_ref, y_ref,
               xre_scr, xim_scr, bu_scr, uil_scr, yil_scr, *, steps, batch):
    ns = S5_SLAB_STATES

    @pl.when(pl.program_id(0) == 0)
    def _():
        xre_scr[...] = jnp.zeros_like(xre_scr)
        xim_scr[...] = jnp.zeros_like(xim_scr)

    for b in range(batch):
        ub = u_ref[b].astype(F32)
        for c in range(S5_SLABS):
            uil_scr.at[c][pl.ds(b, steps, stride=batch), :] = ub[:, c * LANES:(c + 1) * LANES]

    ys = []
    for s in range(S5_SLABS):
        us = uil_scr[s]
        buf = bu_scr.at[s % 2]
        buf[...] = jnp.dot(us.astype(BF16), wb_ref[s], preferred_element_type=F32)
        a_re = jnp.broadcast_to(are_ref[s], (batch, ns))
        a_im = jnp.broadcast_to(aim_ref[s], (batch, ns))

        x_re = xre_scr[s]
        x_im = xim_scr[s]
        for tt in range(steps):
            rs = slice(tt * batch, (tt + 1) * batch)
            n_re = a_re * x_re - a_im * x_im + buf[rs, 0:ns]
            n_im = a_re * x_im + a_im * x_re + buf[rs, ns:2 * ns]
            buf[rs, 0:ns] = n_re
            buf[rs, ns:2 * ns] = n_im
            x_re, x_im = n_re, n_im
        xre_scr[s] = x_re
        xim_scr[s] = x_im
        y = (jnp.dot(buf[...].astype(BF16), wc_ref[s], preferred_element_type=F32)
             + d_ref[:, s * LANES:(s + 1) * LANES] * us)
        ys.append(_gelu_tanh(y).astype(BF16))

    yb = jnp.concatenate(ys, axis=1)
    half = BRANCH_WIDTH
    a = jnp.dot(yb, wg_ref[:, 0:half], preferred_element_type=F32)
    g = jnp.dot(yb, wg_ref[:, half:2 * half], preferred_element_type=F32)
    out = a * _sigmoid(g)
    for c in range(S5_SLABS):
        yil_scr[c] = out[:, c * LANES:(c + 1) * LANES]
    for b in range(batch):
        for c in range(S5_SLABS):
            y_ref[b, :, c * LANES:(c + 1) * LANES] = (
                yil_scr.at[c][pl.ds(b, steps, stride=batch), :].astype(BF16))


def _s5(proj3d, wb, wc, a_re, a_im, d_skip, w_glu, *, steps):
    batch, seq, _ = proj3d.shape
    rows = steps * batch
    ns = S5_SLAB_STATES
    const3 = lambda i: (0, 0, 0)
    return pl.pallas_call(
        functools.partial(_s5_kernel, steps=steps, batch=batch),
        out_shape=jax.ShapeDtypeStruct((batch, seq, BRANCH_WIDTH), BF16),
        grid=(seq // steps,),
        in_specs=[pl.BlockSpec((batch, steps, BRANCH_WIDTH), lambda i: (0, i, COL_US)),
                  pl.BlockSpec((S5_SLABS, LANES, 2 * ns), const3),
                  pl.BlockSpec((S5_SLABS, 2 * ns, LANES), const3),
                  pl.BlockSpec((S5_SLABS, 1, ns), const3),
                  pl.BlockSpec((S5_SLABS, 1, ns), const3),
                  pl.BlockSpec((1, BRANCH_WIDTH), lambda i: (0, 0)),
                  pl.BlockSpec((BRANCH_WIDTH, 2 * BRANCH_WIDTH), lambda i: (0, 0))],
        out_specs=pl.BlockSpec((batch, steps, BRANCH_WIDTH), lambda i: (0, i, 0)),
        scratch_shapes=[pltpu.VMEM((S5_SLABS, batch, ns), F32),
                        pltpu.VMEM((S5_SLABS, batch, ns), F32),
                        pltpu.VMEM((2, rows, 2 * ns), F32),
                        pltpu.VMEM((S5_SLABS, rows, LANES), F32),
                        pltpu.VMEM((S5_SLABS, rows, LANES), F32)],
        compiler_params=_cparams(("arbitrary",)),
        name="s5",
    )(proj3d, wb, wc, a_re, a_im, d_skip, w_glu)


def _merge_kernel(ya_ref, yb_ref, yc_ref, g0_ref, g1_ref, g2_ref, x_ref, wbr_ref,
                  wo_ref, ng_ref, o_ref):
    merged = None
    for n, (y_ref, g_ref) in enumerate(((ya_ref, g0_ref), (yb_ref, g1_ref),
                                        (yc_ref, g2_ref))):
        z = jnp.dot(y_ref[...], wbr_ref[n], preferred_element_type=F32)
        term = _sigmoid(g_ref[...].astype(F32)) * z
        merged = term if merged is None else merged + term
    mix = jnp.dot(merged.astype(BF16), wo_ref[...], preferred_element_type=F32)
    o_ref[...] = x_ref[...] + _rms(mix, ng_ref[...])


def _merge(y_a, y_b, y_c, proj, x2d, w_branch, w_out, norm_g, *, tm):
    m = x2d.shape[0]
    row = lambda i: (i, 0)

    def gate_spec(n):
        return pl.BlockSpec((tm, D_MODEL), lambda i: (i, COL_GATE // 2 + n))

    return pl.pallas_call(
        _merge_kernel,
        out_shape=jax.ShapeDtypeStruct((m, D_MODEL), F32),
        grid=(m // tm,),
        in_specs=[pl.BlockSpec((tm, BRANCH_WIDTH), row),
                  pl.BlockSpec((tm, BRANCH_WIDTH), row),
                  pl.BlockSpec((tm, BRANCH_WIDTH), row),
                  gate_spec(0), gate_spec(1), gate_spec(2),
                  pl.BlockSpec((tm, D_MODEL), row),
                  pl.BlockSpec((N_BRANCH, BRANCH_WIDTH, D_MODEL), lambda i: (0, 0, 0),
                               pipeline_mode=pl.Buffered(1)),
                  pl.BlockSpec((D_MODEL, D_MODEL), lambda i: (0, 0),
                               pipeline_mode=pl.Buffered(1)),
                  pl.BlockSpec((1, D_MODEL), lambda i: (0, 0))],
        out_specs=pl.BlockSpec((tm, D_MODEL), row),
        compiler_params=_cparams(("parallel",)),
        name="merge",
    )(y_a, y_b, y_c, proj, proj, proj, x2d, w_branch, w_out, norm_g)


def _ffn_kernel(x_ref, g_ref, wa_ref, wv_ref, cwa_ref, cwv_ref, cba_ref,
                cbv_ref, wd_ref, ng_ref, o_ref, h_scr, tail_scr, *, tl):
    i = pl.program_id(1)
    f = pl.program_id(2)
    nf = pl.num_programs(2)
    hal = SUBLANES

    @pl.when(f == 0)
    def _():
        h_scr[...] = _rms(x_ref[0], g_ref[...]).astype(BF16)
        o_ref[0] = jnp.zeros((tl, D_MODEL), F32)

    @pl.when(i == 0)
    def _():
        tail_scr[f] = jnp.zeros(tail_scr.shape[1:], F32)

    hb = h_scr[...]

    def conv(part, w_ref, cw_ref, cb_ref):
        up = jnp.dot(hb, w_ref[...], preferred_element_type=F32)
        ext = jnp.concatenate([tail_scr[f, part], up], axis=0)
        tail_scr[f, part] = up[tl - hal:tl, :]
        out = cb_ref[...] + cw_ref[FFN_CONV - 1:FFN_CONV, :] * up
        for j in range(FFN_CONV - 1):
            off = hal - (FFN_CONV - 1) + j
            out = out + cw_ref[j:j + 1, :] * ext[off:off + tl, :]
        return out

    a = conv(0, wa_ref, cwa_ref, cba_ref)
    v = conv(1, wv_ref, cwv_ref, cbv_ref)
    act = (_gelu_tanh(a) * v).astype(BF16)
    o_ref[0] += jnp.dot(act, wd_ref[...], preferred_element_type=F32)

    @pl.when(f == nf - 1)
    def _():
        o_ref[0] = x_ref[0] + _rms(o_ref[0], ng_ref[...])


def _ffn(x3d, gain, w_up, conv_w, conv_b, w_down, norm_g, *, tl, tf):
    batch, seq, _ = x3d.shape
    nfb = D_FF // tf
    hal = SUBLANES
    return pl.pallas_call(
        functools.partial(_ffn_kernel, tl=tl),
        out_shape=jax.ShapeDtypeStruct((batch, seq, D_MODEL), F32),
        grid=(batch, seq // tl, nfb),
        in_specs=[pl.BlockSpec((1, tl, D_MODEL), lambda b, i, f: (b, i, 0)),
                  pl.BlockSpec((1, D_MODEL), lambda b, i, f: (0, 0)),
                  pl.BlockSpec((D_MODEL, tf), lambda b, i, f: (0, f)),
                  pl.BlockSpec((D_MODEL, tf), lambda b, i, f: (0, nfb + f)),
                  pl.BlockSpec((FFN_CONV, tf), lambda b, i, f: (0, f)),
                  pl.BlockSpec((FFN_CONV, tf), lambda b, i, f: (0, nfb + f)),
                  pl.BlockSpec((1, tf), lambda b, i, f: (0, f)),
                  pl.BlockSpec((1, tf), lambda b, i, f: (0, nfb + f)),
                  pl.BlockSpec((tf, D_MODEL), lambda b, i, f: (f, 0)),
                  pl.BlockSpec((1, D_MODEL), lambda b, i, f: (0, 0))],
        out_specs=pl.BlockSpec((1, tl, D_MODEL), lambda b, i, f: (b, i, 0)),
        scratch_shapes=[pltpu.VMEM((tl, D_MODEL), BF16),
                        pltpu.VMEM((nfb, 2, hal, tf), F32)],
        compiler_params=_cparams(("parallel", "arbitrary", "arbitrary")),
        name="conv_ffn",
    )(x3d, gain, w_up, w_up, conv_w, conv_w, conv_b, conv_b, w_down, norm_g)


def _t5_bucket(dist):
    n = jnp.maximum(dist, 0)
    max_exact = REL_BUCKETS // 2
    nf = jnp.maximum(n, 1).astype(F32)
    large = max_exact + (jnp.log(nf / max_exact) / math.log(REL_MAX_DIST / max_exact)
                         * (REL_BUCKETS - max_exact)).astype(jnp.int32)
    large = jnp.minimum(large, REL_BUCKETS - 1)
    return jnp.where(n < max_exact, n, large)


def _attn_bias_tiles(rel_bias, blk):
    qi = jnp.arange(blk, dtype=jnp.int32)[None, :]
    kj = jnp.arange(blk, dtype=jnp.int32)[:, None]
    table = rel_bias.astype(F32) * LOG2E
    last = table[REL_BUCKETS - 1][:, None, None]
    tiles = []
    for off in (0, blk):
        dist = qi - kj + off
        bucket = _t5_bucket(dist)
        bias = jnp.broadcast_to(last, (DA_HEADS, blk, blk))
        for b in range(REL_BUCKETS - 1):
            bias = jnp.where(bucket == b, table[b][:, None, None], bias)
        tiles.append(jnp.where(dist >= 0, bias, NEG_BIG))
    return jnp.stack(tiles, axis=1)


def _s5_params(lam_re, lam_im, log_dt, b_re, b_im, c_re, c_im):
    dt = jnp.exp(log_dt)[:, None]
    mag = jnp.exp(lam_re * dt)
    a_re = mag * jnp.cos(lam_im * dt)
    a_im = mag * jnp.sin(lam_im * dt)
    den = lam_re * lam_re + lam_im * lam_im
    z_re = ((a_re - 1.0) * lam_re + a_im * lam_im) / den
    z_im = (a_im * lam_re - (a_re - 1.0) * lam_im) / den
    bb_re = z_re[..., None] * b_re - z_im[..., None] * b_im
    bb_im = z_re[..., None] * b_im + z_im[..., None] * b_re
    gs = S5_GROUPS // S5_SLABS
    eye = jnp.eye(gs, dtype=F32)

    def in_blocks(bb):
        bb = bb.reshape(S5_SLABS, gs, S5_STATE, S5_GROUP)
        w = jnp.einsum('sgpc,gh->sgchp', bb, eye)
        return w.reshape(S5_SLABS, gs * S5_GROUP, gs * S5_STATE)

    def out_blocks(cc):
        cc = cc.reshape(S5_SLABS, gs, S5_GROUP, S5_STATE)
        w = jnp.einsum('sgcp,gh->sgphc', cc, eye)
        return w.reshape(S5_SLABS, gs * S5_STATE, gs * S5_GROUP)

    wb = jnp.concatenate([in_blocks(bb_re), in_blocks(bb_im)], axis=-1).astype(BF16)
    wc = jnp.concatenate([out_blocks(c_re), out_blocks(-c_im)], axis=-2).astype(BF16)
    a_re = a_re.reshape(S5_SLABS, 1, S5_SLAB_STATES)
    a_im = a_im.reshape(S5_SLABS, 1, S5_SLAB_STATES)
    return wb, wc, a_re, a_im


IN_PROJ_TM, IN_PROJ_TN = 1024, 2048
MLSTM_CHUNK = 128
ATTN_BLOCK = 512
ATTN_HEADS_PER_STEP = 2
S5_STEPS = 32
MERGE_TM = 256
FFN_TL, FFN_TF = 512, 512


def _layer(x2d, batch, seq, layer, p):
    w_in = p['w_in'][layer]
    n_if = 2 * M_HEADS
    split = 4 * M_HEADS * M_HEAD_DIM
    w_main = jnp.concatenate([w_in[:, :split], w_in[:, split + n_if:]], axis=1).astype(BF16)
    w_gate = jnp.pad(w_in[:, split:split + n_if],
                     ((0, 0), (0, N_GATE_PAD - n_if))).astype(BF16)
    proj, gates = _in_proj(x2d, p['norm_mix_pre'][layer][None, :], w_gate, w_main,
                           tm=min(IN_PROJ_TM, batch * seq), tn=IN_PROJ_TN)

    gate_bias = jnp.pad(p['mlstm_b_if'][layer].reshape(1, n_if),
                        ((0, 0), (0, N_GATE_PAD - n_if)))
    y_a = _mlstm(proj, gates, gate_bias, p['mlstm_conv'][layer],
                 p['mlstm_norm'][layer][None, :], batch=batch, seq=seq,
                 chunk=min(MLSTM_CHUNK, seq))

    lambda_init = 0.8 - 0.6 * math.exp(-0.3 * layer)
    lam = p['diff_lambda'][layer]
    lam_full = (jnp.exp(jnp.sum(lam[0] * lam[1])) - jnp.exp(jnp.sum(lam[2] * lam[3]))
                + lambda_init)
    blk = min(ATTN_BLOCK, seq)
    scalars = jnp.concatenate(
        [lam_full[None], p['rel_bias'][REL_BUCKETS - 1, :] * LOG2E]).astype(F32)
    y_b = _diff_attn(scalars, proj, p['attn_bias_tiles'],
                     p['diff_norm'][layer][None, :], batch=batch, seq=seq, blk=blk,
                     heads=ATTN_HEADS_PER_STEP, out_scale=1.0 - lambda_init)

    wb, wc, a_re, a_im = _s5_params(
        p['s5_lambda_re'][layer], p['s5_lambda_im'][layer], p['s5_log_dt'][layer],
        p['s5_b_re'][layer], p['s5_b_im'][layer], p['s5_c_re'][layer], p['s5_c_im'][layer])
    y_c = _s5(proj.reshape(batch, seq, N_MAIN), wb, wc, a_re, a_im,
              p['s5_d'][layer][None, :], p['s5_w_glu'][layer].astype(BF16),
              steps=min(S5_STEPS, seq)).reshape(batch * seq, BRANCH_WIDTH)

    x2d = _merge(y_a, y_b, y_c, proj, x2d, p['w_branch'][layer].astype(BF16),
                 p['w_out'][layer].astype(BF16), p['norm_mix_post'][layer][None, :],
                 tm=MERGE_TM)

    x3d = _ffn(x2d.reshape(batch, seq, D_MODEL), p['norm_ffn_pre'][layer][None, :],
               p['w_up'][layer].astype(BF16), p['ffn_conv'][layer],
               p['ffn_conv_b'][layer][None, :], p['w_down'][layer].astype(BF16),
               p['norm_ffn_post'][layer][None, :], tl=min(FFN_TL, seq), tf=FFN_TF)
    return x3d.reshape(batch * seq, D_MODEL)


def kernel(x, norm_mix_pre, norm_mix_post, norm_ffn_pre, norm_ffn_post, w_in, mlstm_b_if, mlstm_conv, mlstm_norm, diff_lambda, diff_norm, rel_bias, s5_lambda_re, s5_lambda_im, s5_log_dt, s5_b_re, s5_b_im, s5_c_re, s5_c_im, s5_d, s5_w_glu, w_branch, w_out, w_up, ffn_conv, ffn_conv_b, w_down):
    p = dict(norm_mix_pre=norm_mix_pre, norm_mix_post=norm_mix_post,
             norm_ffn_pre=norm_ffn_pre, norm_ffn_post=norm_ffn_post, w_in=w_in,
             mlstm_b_if=mlstm_b_if, mlstm_conv=mlstm_conv, mlstm_norm=mlstm_norm,
             diff_lambda=diff_lambda, diff_norm=diff_norm, rel_bias=rel_bias,
             s5_lambda_re=s5_lambda_re, s5_lambda_im=s5_lambda_im, s5_log_dt=s5_log_dt,
             s5_b_re=s5_b_re, s5_b_im=s5_b_im, s5_c_re=s5_c_re, s5_c_im=s5_c_im,
             s5_d=s5_d, s5_w_glu=s5_w_glu, w_branch=w_branch, w_out=w_out, w_up=w_up,
             ffn_conv=ffn_conv, ffn_conv_b=ffn_conv_b, w_down=w_down)
    batch, seq, _ = x.shape
    p['attn_bias_tiles'] = _attn_bias_tiles(rel_bias, min(ATTN_BLOCK, seq))
    x2d = x.reshape(batch * seq, D_MODEL)
    for layer in range(DEPTH):
        x2d = _layer(x2d, batch, seq, layer, p)
    return x2d.reshape(batch, seq, D_MODEL)
```

```python
import functools
import math

import jax
import jax.numpy as jnp
from jax import lax
from jax.experimental import pallas as pl
from jax.experimental.pallas import tpu as pltpu

F32 = jnp.float32
BF16 = jnp.bfloat16
HIGHEST = lax.Precision.HIGHEST

D_MODEL = 2048
DEPTH = 2
BRANCH_WIDTH = 1024
N_BRANCH = 3
M_HEADS = 4
M_HEAD_DIM = 256
M_CONV = 4
DA_HEADS = 4
DA_HEAD_DIM = 128
DA_V_DIM = 256
REL_BUCKETS = 32
REL_MAX_DIST = 128
S5_GROUP = 16
S5_GROUPS = 64
S5_STATE = 64
D_FF = 5632
FFN_CONV = 3
EPS = 1e-6

LANES = 128
SUBLANES = 8
MXU_TILE = 256
VMEM_LIMIT = 56 * 1024 * 1024

N_MAIN = 14336
COL_QM, COL_KM, COL_VM, COL_OM = 0, 1, 2, 3
COL_QD, COL_KD, COL_VD, COL_US = 4, 5, 6, 7
COL_GATE = 8
N_GATE_PAD = LANES

S5_SLABS = 8
S5_SLAB_STATES = 512

NEG_BIG = -1e30
LOG2E = math.log2(math.e)


def _cparams(sem):
    return pltpu.CompilerParams(dimension_semantics=sem, vmem_limit_bytes=VMEM_LIMIT)


def _sigmoid(x):
    return 0.5 * jnp.tanh(0.5 * x) + 0.5


def _gelu_tanh(x):
    c = math.sqrt(2.0 / math.pi)
    return 0.5 * x * (1.0 + jnp.tanh(c * (x + 0.044715 * (x * x * x))))


def _rms(x, gain):
    var = jnp.mean(x * x, axis=-1, keepdims=True)
    return x * lax.rsqrt(var + EPS) * gain


def _in_proj_kernel(x_ref, g_ref, wg_ref, w_ref, o_ref, og_ref, h_scr):
    @pl.when(pl.program_id(1) == 0)
    def _():
        h_scr[...] = _rms(x_ref[...], g_ref[...]).astype(BF16)
        og_ref[...] = jnp.dot(h_scr[...], wg_ref[...], preferred_element_type=F32)

    o_ref[...] = jnp.dot(h_scr[...], w_ref[...],
                         preferred_element_type=F32).astype(BF16)


def _in_proj(x2d, gain, w_gate, w_main, *, tm, tn):
    m = x2d.shape[0]
    return pl.pallas_call(
        _in_proj_kernel,
        out_shape=(jax.ShapeDtypeStruct((m, N_MAIN), BF16),
                   jax.ShapeDtypeStruct((m, N_GATE_PAD), F32)),
        grid=(m // tm, N_MAIN // tn),
        in_specs=[pl.BlockSpec((tm, D_MODEL), lambda i, n: (i, 0)),
                  pl.BlockSpec((1, D_MODEL), lambda i, n: (0, 0)),
                  pl.BlockSpec((D_MODEL, N_GATE_PAD), lambda i, n: (0, 0)),
                  pl.BlockSpec((D_MODEL, tn), lambda i, n: (0, n))],
        out_specs=(pl.BlockSpec((tm, tn), lambda i, n: (i, n)),
                   pl.BlockSpec((tm, N_GATE_PAD), lambda i, n: (i, 0))),
        scratch_shapes=[pltpu.VMEM((tm, D_MODEL), BF16)],
        compiler_params=_cparams(("parallel", "arbitrary")),
        name="in_proj",
    )(x2d, gain, w_gate, w_main)


def _mlstm_kernel(q_ref, k_ref, v_ref, o_ref, gt_ref, gb_ref, cw_ref, ng_ref, y_ref,
                  c_scr, n_scr, m_scr, qe_scr, ke_scr, *, chunk):
    t = chunk
    hd = M_HEAD_DIM
    width = M_HEADS * hd

    @pl.when(pl.program_id(1) == 0)
    def _():
        c_scr[...] = jnp.zeros_like(c_scr)
        n_scr[...] = jnp.zeros_like(n_scr)
        m_scr[...] = jnp.zeros_like(m_scr)
        qe_scr[0:SUBLANES, :] = jnp.zeros((SUBLANES, width), F32)
        ke_scr[0:SUBLANES, :] = jnp.zeros((SUBLANES, width), F32)

    qe_scr[SUBLANES:SUBLANES + t, :] = q_ref[...].astype(F32)
    ke_scr[SUBLANES:SUBLANES + t, :] = k_ref[...].astype(F32)

    gates = gt_ref[...] + gb_ref[...]
    log_f = jnp.minimum(gates, 0.0) - jnp.log1p(jnp.exp(-jnp.abs(gates)))
    row = lax.broadcasted_iota(jnp.int32, (t, t), 0)
    col = lax.broadcasted_iota(jnp.int32, (t, t), 1)
    causal = col <= row
    cum = jnp.dot(causal.astype(F32), log_f, preferred_element_type=F32,
                  precision=HIGHEST)
    gates_t = gates.T
    cum_t = cum.T

    for h in range(M_HEADS):
        sl = slice(h * hd, (h + 1) * hd)
        ksl = slice(width + h * hd, width + (h + 1) * hd)
        qc = jnp.zeros((t, hd), F32)
        kc = jnp.zeros((t, hd), F32)
        for j in range(M_CONV):
            off = SUBLANES - (M_CONV - 1) + j
            qc = qc + cw_ref[j:j + 1, sl] * qe_scr[off:off + t, sl]
            kc = kc + cw_ref[j:j + 1, ksl] * ke_scr[off:off + t, sl]
        qc = qc * _sigmoid(qc)
        kc = kc * _sigmoid(kc) * (hd ** -0.5)
        qb = qc.astype(BF16)
        kb = kc.astype(BF16)
        vb = v_ref[:, sl]

        li_row = gates_t[h:h + 1, :]
        b_row = cum_t[M_HEADS + h:M_HEADS + h + 1, :]
        li_col = gates[:, h:h + 1]
        b_col = cum[:, M_HEADS + h:M_HEADS + h + 1]
        m_prev = m_scr[h:h + 1, 0:1]
        c_prev = c_scr[h]
        n_prev = n_scr[h:h + 1, :]

        dmat = jnp.where(causal, b_col - b_row + li_row, -jnp.inf)
        inter = b_col + m_prev
        m_t = jnp.maximum(inter, jnp.max(dmat, axis=-1, keepdims=True))
        s = lax.dot_general(qb, kb, (((1,), (1,)), ((), ())),
                            preferred_element_type=F32) * jnp.exp(dmat - m_t)
        w_inter = jnp.exp(inter - m_t)
        num = (jnp.dot(s.astype(BF16), vb, preferred_element_type=F32)
               + w_inter * jnp.dot(qb, c_prev.astype(BF16), preferred_element_type=F32))
        den = (jnp.sum(s, axis=-1, keepdims=True)
               + w_inter * jnp.sum(qc * n_prev, axis=-1, keepdims=True))
        hh = num / jnp.maximum(jnp.abs(den), jnp.exp(-m_t))

        g = cum[t - 1:t, M_HEADS + h:M_HEADS + h + 1]
        a_col = g - b_col + li_col
        m_new = jnp.maximum(g + m_prev, jnp.max(a_col, axis=0, keepdims=True))
        ws = jnp.exp(a_col - m_new)
        decay = jnp.exp(g + m_prev - m_new)
        kw = ws * kc
        c_scr[h] = decay * c_prev + lax.dot_general(
            kw.astype(BF16), vb, (((0,), (0,)), ((), ())), preferred_element_type=F32)
        n_scr[h:h + 1, :] = decay * n_prev + jnp.sum(kw, axis=0, keepdims=True)
        m_scr[h:h + 1, :] = jnp.broadcast_to(m_new, (1, LANES))

        hn = _rms(hh, ng_ref[:, sl])
        y_ref[:, sl] = (_sigmoid(o_ref[:, sl].astype(F32)) * hn).astype(BF16)

    qe_scr[0:SUBLANES, :] = qe_scr[t:t + SUBLANES, :]
    ke_scr[0:SUBLANES, :] = ke_scr[t:t + SUBLANES, :]


def _mlstm(proj, gates, gate_bias, conv_w, norm_g, *, batch, seq, chunk):
    nc = seq // chunk
    width = M_HEADS * M_HEAD_DIM

    def col_spec(cb):
        return pl.BlockSpec((chunk, width), lambda b, c: (b * nc + c, cb))

    return pl.pallas_call(
        functools.partial(_mlstm_kernel, chunk=chunk),
        out_shape=jax.ShapeDtypeStruct((batch * seq, width), BF16),
        grid=(batch, nc),
        in_specs=[col_spec(COL_QM), col_spec(COL_KM), col_spec(COL_VM), col_spec(COL_OM),
                  pl.BlockSpec((chunk, N_GATE_PAD), lambda b, c: (b * nc + c, 0)),
                  pl.BlockSpec((1, N_GATE_PAD), lambda b, c: (0, 0)),
                  pl.BlockSpec((M_CONV, 2 * width), lambda b, c: (0, 0)),
                  pl.BlockSpec((1, width), lambda b, c: (0, 0))],
        out_specs=pl.BlockSpec((chunk, width), lambda b, c: (b * nc + c, 0)),
        scratch_shapes=[pltpu.VMEM((M_HEADS, M_HEAD_DIM, M_HEAD_DIM), F32),
                        pltpu.VMEM((M_HEADS, M_HEAD_DIM), F32),
                        pltpu.VMEM((M_HEADS, LANES), F32),
                        pltpu.VMEM((chunk + SUBLANES, width), F32),
                        pltpu.VMEM((chunk + SUBLANES, width), F32)],
        compiler_params=_cparams(("parallel", "arbitrary")),
        name="mlstm",
    )(proj, proj, proj, proj, gates, gate_bias, conv_w, norm_g)


def _attn_kernel(sc_ref, q_ref, k_ref, v_ref, bias_ref, ng_ref, y_ref,
                 m_scr, l_scr, acc_scr, vt_scr, s_scr, *, blk, heads, out_scale):
    t = blk
    d = DA_HEAD_DIM
    w = DA_V_DIM
    h0 = pl.program_id(1) * heads
    i = pl.program_id(2)
    lam = sc_ref[0]

    qs = (q_ref[...].astype(F32) * (d ** -0.5 * LOG2E)).astype(BF16)

    @pl.when(i == 0)
    def _():
        vt_scr[...] = v_ref[...].astype(F32).T.astype(BF16)

    def block_step(j, bias_tile=None, first=False):
        start = pl.multiple_of(j * t, t)
        kb = k_ref[pl.ds(start, t), :]
        vtb = vt_scr[:, pl.ds(start, t)]
        far = bias_tile is None
        for ci in range(2 * heads):
            cols = slice(ci * d, (ci + 1) * d)
            s_scr[ci] = lax.dot_general(kb[:, cols], qs[:, cols], (((1,), (1,)), ((), ())),
                                        preferred_element_type=F32)
        for hh in range(heads):
            bias = sc_ref[1 + h0 + hh] if far else bias_ref[hh, bias_tile]
            for c in range(2):
                ci = 2 * hh + c
                s = s_scr[ci]
                if far:
                    m_blk = jnp.max(s, axis=0, keepdims=True) + bias
                else:
                    s = s + bias
                    m_blk = jnp.max(s, axis=0, keepdims=True)
                m_new = m_blk if first else jnp.maximum(m_scr[ci], m_blk)
                p = jnp.exp2(s - ((m_new - bias) if far else m_new))
                l_blk = jnp.sum(p, axis=0, keepdims=True)
                pv = jnp.dot(vtb[hh * w:(hh + 1) * w, :], p.astype(BF16),
                             preferred_element_type=F32)
                if first:
                    l_scr[ci] = l_blk
                    acc_scr[ci] = pv
                else:
                    alpha = jnp.exp2(m_scr[ci] - m_new)
                    l_scr[ci] = alpha * l_scr[ci] + l_blk
                    acc_scr[ci] = alpha * acc_scr[ci] + pv
                m_scr[ci] = m_new

    block_step(i, 0, first=True)

    @pl.when(i >= 1)
    def _():
        block_step(i - 1, 1)

    def far_body(j, carry):
        block_step(j)
        return carry

    lax.fori_loop(0, jnp.maximum(i - 1, 0), far_body, 0)

    for hh in range(heads):
        ca, cb = 2 * hh, 2 * hh + 1
        out_t = (acc_scr[ca] * (1.0 / l_scr[ca])
                 - lam * (acc_scr[cb] * (1.0 / l_scr[cb])))
        hs = slice(hh * w, (hh + 1) * w)
        y_ref[:, hs] = (_rms(out_t.T, ng_ref[:, hs]) * out_scale).astype(BF16)


def _diff_attn(scalars, proj, bias_tiles, norm_g, *, batch, seq, blk, heads, out_scale):
    nq = seq // blk
    kvw = heads * DA_V_DIM
    per_row = BRANCH_WIDTH // kvw
    return pl.pallas_call(
        functools.partial(_attn_kernel, blk=blk, heads=heads, out_scale=out_scale),
        out_shape=jax.ShapeDtypeStruct((batch * seq, DA_HEADS * DA_V_DIM), BF16),
        grid=(batch, DA_HEADS // heads, nq),
        in_specs=[pl.BlockSpec(memory_space=pltpu.SMEM),
                  pl.BlockSpec((blk, kvw), lambda b, h, i: (b * nq + i, COL_QD * per_row + h)),
                  pl.BlockSpec((seq, kvw), lambda b, h, i: (b, COL_KD * per_row + h)),
                  pl.BlockSpec((seq, kvw), lambda b, h, i: (b, COL_VD * per_row + h)),
                  pl.BlockSpec((heads, 2, blk, blk), lambda b, h, i: (h, 0, 0, 0)),
                  pl.BlockSpec((1, kvw), lambda b, h, i: (0, h))],
        out_specs=pl.BlockSpec((blk, kvw), lambda b, h, i: (b * nq + i, h)),
        scratch_shapes=[pltpu.VMEM((2 * heads, 1, blk), F32),
                        pltpu.VMEM((2 * heads, 1, blk), F32),
                        pltpu.VMEM((2 * heads, DA_V_DIM, blk), F32),
                        pltpu.VMEM((kvw, seq), BF16),
                        pltpu.VMEM((2 * heads, blk, blk), F32)],
        compiler_params=_cparams(("parallel", "parallel", "arbitrary")),
        name="diff_attn",
    )(scalars, proj, proj, proj, bias_tiles, norm_g)


def _s5_kernel(u_ref, wb_ref, wc_ref, are_ref, aim_ref, d_ref, wg_ref, y_ref,
               xre_scr, xim_scr, bu_scr, uil_scr, yil_scr, *, steps, batch):
    ns = S5_SLAB_STATES

    @pl.when(pl.program_id(0) == 0)
    def _():
        xre_scr[...] = jnp.zeros_like(xre_scr)
        xim_scr[...] = jnp.zeros_like(xim_scr)

    for b in range(batch):
        ub = u_ref[b].astype(F32)
        for c in range(S5_SLABS):
            uil_scr.at[c][pl.ds(b, steps, stride=batch), :] = ub[:, c * LANES:(c + 1) * LANES]

    ys = []
    for s in range(S5_SLABS):
        us = uil_scr[s]
        buf = bu_scr.at[s % 2]
        buf[...] = jnp.dot(us.astype(BF16), wb_ref[s], preferred_element_type=F32)
        a_re = jnp.broadcast_to(are_ref[s], (batch, ns))
        a_im = jnp.broadcast_to(aim_ref[s], (batch, ns))

        x_re = xre_scr[s]
        x_im = xim_scr[s]
        for tt in range(steps):
            rs = slice(tt * batch, (tt + 1) * batch)
            n_re = a_re * x_re - a_im * x_im + buf[rs, 0:ns]
            n_im = a_re * x_im + a_im * x_re + buf[rs, ns:2 * ns]
            buf[rs, 0:ns] = n_re
            buf[rs, ns:2 * ns] = n_im
            x_re, x_im = n_re, n_im
        xre_scr[s] = x_re
        xim_scr[s] = x_im
        y = (jnp.dot(buf[...].astype(BF16), wc_ref[s], preferred_element_type=F32)
             + d_ref[:, s * LANES:(s + 1) * LANES] * us)
        ys.append(_gelu_tanh(y).astype(BF16))

    yb = jnp.concatenate(ys, axis=1)
    half = BRANCH_WIDTH
    a = jnp.dot(yb, wg_ref[:, 0:half], preferred_element_type=F32)
    g = jnp.dot(yb, wg_ref[:, half:2 * half], preferred_element_type=F32)
    out = a * _sigmoid(g)
    for c in range(S5_SLABS):
        yil_scr[c] = out[:, c * LANES:(c + 1) * LANES]
    for b in range(batch):
        for c in range(S5_SLABS):
            y_ref[b, :, c * LANES:(c + 1) * LANES] = (
                yil_scr.at[c][pl.ds(b, steps, stride=batch), :].astype(BF16))


def _s5(proj3d, wb, wc, a_re, a_im, d_skip, w_glu, *, steps):
    batch, seq, _ = proj3d.shape
    rows = steps * batch
    ns = S5_SLAB_STATES
    const3 = lambda i: (0, 0, 0)
    return pl.pallas_call(
        functools.partial(_s5_kernel, steps=steps, batch=batch),
        out_shape=jax.ShapeDtypeStruct((batch, seq, BRANCH_WIDTH), BF16),
        grid=(seq // steps,),
        in_specs=[pl.BlockSpec((batch, steps, BRANCH_WIDTH), lambda i: (0, i, COL_US)),
                  pl.BlockSpec((S5_SLABS, LANES, 2 * ns), const3),
                  pl.BlockSpec((S5_SLABS, 2 * ns, LANES), const3),
                  pl.BlockSpec((S5_SLABS, 1, ns), const3),
                  pl.BlockSpec((S5_SLABS, 1, ns), const3),
                  pl.BlockSpec((1, BRANCH_WIDTH), lambda i: (0, 0)),
                  pl.BlockSpec((BRANCH_WIDTH, 2 * BRANCH_WIDTH), lambda i: (0, 0))],
        out_specs=pl.BlockSpec((batch, steps, BRANCH_WIDTH), lambda i: (0, i, 0)),
        scratch_shapes=[pltpu.VMEM((S5_SLABS, batch, ns), F32),
                        pltpu.VMEM((S5_SLABS, batch, ns), F32),
                        pltpu.VMEM((2, rows, 2 * ns), F32),
                        pltpu.VMEM((S5_SLABS, rows, LANES), F32),
                        pltpu.VMEM((S5_SLABS, rows, LANES), F32)],
        compiler_params=_cparams(("arbitrary",)),
        name="s5",
    )(proj3d, wb, wc, a_re, a_im, d_skip, w_glu)


def _merge_kernel(ya_ref, yb_ref, yc_ref, g0_ref, g1_ref, g2_ref, x_ref, wbr_ref,
                  wo_ref, ng_ref, o_ref):
    merged = None
    for n, (y_ref, g_ref) in enumerate(((ya_ref, g0_ref), (yb_ref, g1_ref),
                                        (yc_ref, g2_ref))):
        z = jnp.dot(y_ref[...], wbr_ref[n], preferred_element_type=F32)
        term = _sigmoid(g_ref[...].astype(F32)) * z
        merged = term if merged is None else merged + term
    mix = jnp.dot(merged.astype(BF16), wo_ref[...], preferred_element_type=F32)
    o_ref[...] = x_ref[...] + _rms(mix, ng_ref[...])


def _merge(y_a, y_b, y_c, proj, x2d, w_branch, w_out, norm_g, *, tm):
    m = x2d.shape[0]
    row = lambda i: (i, 0)

    def gate_spec(n):
        return pl.BlockSpec((tm, D_MODEL), lambda i: (i, COL_GATE // 2 + n))

    return pl.pallas_call(
        _merge_kernel,
        out_shape=jax.ShapeDtypeStruct((m, D_MODEL), F32),
        grid=(m // tm,),
        in_specs=[pl.BlockSpec((tm, BRANCH_WIDTH), row),
                  pl.BlockSpec((tm, BRANCH_WIDTH), row),
                  pl.BlockSpec((tm, BRANCH_WIDTH), row),
                  gate_spec(0), gate_spec(1), gate_spec(2),
                  pl.BlockSpec((tm, D_MODEL), row),
                  pl.BlockSpec((N_BRANCH, BRANCH_WIDTH, D_MODEL), lambda i: (0, 0, 0),
                               pipeline_mode=pl.Buffered(1)),
                  pl.BlockSpec((D_MODEL, D_MODEL), lambda i: (0, 0),
                               pipeline_mode=pl.Buffered(1)),
                  pl.BlockSpec((1, D_MODEL), lambda i: (0, 0))],
        out_specs=pl.BlockSpec((tm, D_MODEL), row),
        compiler_params=_cparams(("parallel",)),
        name="merge",
    )(y_a, y_b, y_c, proj, proj, proj, x2d, w_branch, w_out, norm_g)


def _ffn_kernel(x_ref, g_ref, wa_ref, wv_ref, cwa_ref, cwv_ref, cba_ref,
                cbv_ref, wd_ref, ng_ref, o_ref, h_scr, tail_scr, *, tl):
    i = pl.program_id(1)
    f = pl.program_id(2)
    nf = pl.num_programs(2)
    hal = SUBLANES

    @pl.when(f == 0)
    def _():
        h_scr[...] = _rms(x_ref[0], g_ref[...]).astype(BF16)
        o_ref[0] = jnp.zeros((tl, D_MODEL), F32)

    @pl.when(i == 0)
    def _():
        tail_scr[f] = jnp.zeros(tail_scr.shape[1:], F32)

    rs = tl // FFN_ROW_SPLIT
    ups = []
    for r in range(FFN_ROW_SPLIT):
        hb = h_scr[r * rs:(r + 1) * rs, :]
        ups.append((jnp.dot(hb, wa_ref[...], preferred_element_type=F32),
                    jnp.dot(hb, wv_ref[...], preferred_element_type=F32)))

    def conv(tail, up, cw_ref, cb_ref):
        ext = jnp.concatenate([tail, up], axis=0)
        out = cb_ref[...] + cw_ref[FFN_CONV - 1:FFN_CONV, :] * up
        for j in range(FFN_CONV - 1):
            off = hal - (FFN_CONV - 1) + j
            out = out + cw_ref[j:j + 1, :] * ext[off:off + rs, :]
        return out

    tail_a = tail_scr[f, 0]
    tail_v = tail_scr[f, 1]
    for r, (up_a, up_v) in enumerate(ups):
        a = conv(tail_a, up_a, cwa_ref, cba_ref)
        v = conv(tail_v, up_v, cwv_ref, cbv_ref)
        tail_a = up_a[rs - hal:rs, :]
        tail_v = up_v[rs - hal:rs, :]
        act = (_gelu_tanh(a) * v).astype(BF16)
        o_ref[0, r * rs:(r + 1) * rs, :] += jnp.dot(act, wd_ref[...],
                                                    preferred_element_type=F32)
    tail_scr[f, 0] = tail_a
    tail_scr[f, 1] = tail_v

    @pl.when(f == nf - 1)
    def _():
        o_ref[0] = x_ref[0] + _rms(o_ref[0], ng_ref[...])


def _ffn(x3d, gain, w_up, conv_w, conv_b, w_down, norm_g, *, tl, tf):
    batch, seq, _ = x3d.shape
    nfb = D_FF // tf
    hal = SUBLANES
    return pl.pallas_call(
        functools.partial(_ffn_kernel, tl=tl),
        out_shape=jax.ShapeDtypeStruct((batch, seq, D_MODEL), F32),
        grid=(batch, seq // tl, nfb),
        in_specs=[pl.BlockSpec((1, tl, D_MODEL), lambda b, i, f: (b, i, 0)),
                  pl.BlockSpec((1, D_MODEL), lambda b, i, f: (0, 0)),
                  pl.BlockSpec((D_MODEL, tf), lambda b, i, f: (0, f)),
                  pl.BlockSpec((D_MODEL, tf), lambda b, i, f: (0, nfb + f)),
                  pl.BlockSpec((FFN_CONV, tf), lambda b, i, f: (0, f)),
                  pl.BlockSpec((FFN_CONV, tf), lambda b, i, f: (0, nfb + f)),
                  pl.BlockSpec((1, tf), lambda b, i, f: (0, f)),
                  pl.BlockSpec((1, tf), lambda b, i, f: (0, nfb + f)),
                  pl.BlockSpec((tf, D_MODEL), lambda b, i, f: (f, 0)),
                  pl.BlockSpec((1, D_MODEL), lambda b, i, f: (0, 0))],
        out_specs=pl.BlockSpec((1, tl, D_MODEL), lambda b, i, f: (b, i, 0)),
        scratch_shapes=[pltpu.VMEM((tl, D_MODEL), BF16),
                        pltpu.VMEM((nfb, 2, hal, tf), F32)],
        compiler_params=_cparams(("parallel", "arbitrary", "arbitrary")),
        name="conv_ffn",
    )(x3d, gain, w_up, w_up, conv_w, conv_w, conv_b, conv_b, w_down, norm_g)


def _t5_bucket(dist):
    n = jnp.maximum(dist, 0)
    max_exact = REL_BUCKETS // 2
    nf = jnp.maximum(n, 1).astype(F32)
    large = max_exact + (jnp.log(nf / max_exact) / math.log(REL_MAX_DIST / max_exact)
                         * (REL_BUCKETS - max_exact)).astype(jnp.int32)
    large = jnp.minimum(large, REL_BUCKETS - 1)
    return jnp.where(n < max_exact, n, large)


def _attn_bias_tiles(rel_bias, blk):
    qi = jnp.arange(blk, dtype=jnp.int32)[None, :]
    kj = jnp.arange(blk, dtype=jnp.int32)[:, None]
    table = rel_bias.astype(F32) * LOG2E
    last = table[REL_BUCKETS - 1][:, None, None]
    tiles = []
    for off in (0, blk):
        dist = qi - kj + off
        bucket = _t5_bucket(dist)
        bias = jnp.broadcast_to(last, (DA_HEADS, blk, blk))
        for b in range(REL_BUCKETS - 1):
            bias = jnp.where(bucket == b, table[b][:, None, None], bias)
        tiles.append(jnp.where(dist >= 0, bias, NEG_BIG))
    return jnp.stack(tiles, axis=1)


def _s5_params(lam_re, lam_im, log_dt, b_re, b_im, c_re, c_im):
    dt = jnp.exp(log_dt)[:, None]
    mag = jnp.exp(lam_re * dt)
    a_re = mag * jnp.cos(lam_im * dt)
    a_im = mag * jnp.sin(lam_im * dt)
    den = lam_re * lam_re + lam_im * lam_im
    z_re = ((a_re - 1.0) * lam_re + a_im * lam_im) / den
    z_im = (a_im * lam_re - (a_re - 1.0) * lam_im) / den
    bb_re = z_re[..., None] * b_re - z_im[..., None] * b_im
    bb_im = z_re[..., None] * b_im + z_im[..., None] * b_re
    gs = S5_GROUPS // S5_SLABS
    eye = jnp.eye(gs, dtype=F32)

    def in_blocks(bb):
        bb = bb.reshape(S5_SLABS, gs, S5_STATE, S5_GROUP)
        w = jnp.einsum('sgpc,gh->sgchp', bb, eye)
        return w.reshape(S5_SLABS, gs * S5_GROUP, gs * S5_STATE)

    def out_blocks(cc):
        cc = cc.reshape(S5_SLABS, gs, S5_GROUP, S5_STATE)
        w = jnp.einsum('sgcp,gh->sgphc', cc, eye)
        return w.reshape(S5_SLABS, gs * S5_STATE, gs * S5_GROUP)

    wb = jnp.concatenate([in_blocks(bb_re), in_blocks(bb_im)], axis=-1).astype(BF16)
    wc = jnp.concatenate([out_blocks(c_re), out_blocks(-c_im)], axis=-2).astype(BF16)
    a_re = a_re.reshape(S5_SLABS, 1, S5_SLAB_STATES)
    a_im = a_im.reshape(S5_SLABS, 1, S5_SLAB_STATES)
    return wb, wc, a_re, a_im


IN_PROJ_TM, IN_PROJ_TN = 1024, 2048
MLSTM_CHUNK = 128
ATTN_BLOCK = 512
ATTN_HEADS_PER_STEP = 2
S5_STEPS = 32
MERGE_TM = 256
FFN_TL, FFN_TF = 512, 512
FFN_ROW_SPLIT = 2


def _layer(x2d, batch, seq, layer, p):
    w_in = p['w_in'][layer]
    n_if = 2 * M_HEADS
    split = 4 * M_HEADS * M_HEAD_DIM
    w_main = jnp.concatenate([w_in[:, :split], w_in[:, split + n_if:]], axis=1).astype(BF16)
    w_gate = jnp.pad(w_in[:, split:split + n_if],
                     ((0, 0), (0, N_GATE_PAD - n_if))).astype(BF16)
    proj, gates = _in_proj(x2d, p['norm_mix_pre'][layer][None, :], w_gate, w_main,
                           tm=min(IN_PROJ_TM, batch * seq), tn=IN_PROJ_TN)

    gate_bias = jnp.pad(p['mlstm_b_if'][layer].reshape(1, n_if),
                        ((0, 0), (0, N_GATE_PAD - n_if)))
    y_a = _mlstm(proj, gates, gate_bias, p['mlstm_conv'][layer],
                 p['mlstm_norm'][layer][None, :], batch=batch, seq=seq,
                 chunk=min(MLSTM_CHUNK, seq))

    lambda_init = 0.8 - 0.6 * math.exp(-0.3 * layer)
    lam = p['diff_lambda'][layer]
    lam_full = (jnp.exp(jnp.sum(lam[0] * lam[1])) - jnp.exp(jnp.sum(lam[2] * lam[3]))
                + lambda_init)
    blk = min(ATTN_BLOCK, seq)
    scalars = jnp.concatenate(
        [lam_full[None], p['rel_bias'][REL_BUCKETS - 1, :] * LOG2E]).astype(F32)
    y_b = _diff_attn(scalars, proj, p['attn_bias_tiles'],
                     p['diff_norm'][layer][None, :], batch=batch, seq=seq, blk=blk,
                     heads=ATTN_HEADS_PER_STEP, out_scale=1.0 - lambda_init)

    wb, wc, a_re, a_im = _s5_params(
        p['s5_lambda_re'][layer], p['s5_lambda_im'][layer], p['s5_log_dt'][layer],
        p['s5_b_re'][layer], p['s5_b_im'][layer], p['s5_c_re'][layer], p['s5_c_im'][layer])
    y_c = _s5(proj.reshape(batch, seq, N_MAIN), wb, wc, a_re, a_im,
              p['s5_d'][layer][None, :], p['s5_w_glu'][layer].astype(BF16),
              steps=min(S5_STEPS, seq)).reshape(batch * seq, BRANCH_WIDTH)

    x2d = _merge(y_a, y_b, y_c, proj, x2d, p['w_branch'][layer].astype(BF16),
                 p['w_out'][layer].astype(BF16), p['norm_mix_post'][layer][None, :],
                 tm=MERGE_TM)

    x3d = _ffn(x2d.reshape(batch, seq, D_MODEL), p['norm_ffn_pre'][layer][None, :],
               p['w_up'][layer].astype(BF16), p['ffn_conv'][layer],
               p['ffn_conv_b'][layer][None, :], p['w_down'][layer].astype(BF16),
               p['norm_ffn_post'][layer][None, :], tl=min(FFN_TL, seq), tf=FFN_TF)
    return x3d.reshape(batch * seq, D_MODEL)


def kernel(x, norm_mix_pre, norm_mix_post, norm_ffn_pre, norm_ffn_post, w_in, mlstm_b_if, mlstm_conv, mlstm_norm, diff_lambda, diff_norm, rel_bias, s5_lambda_re, s5_lambda_im, s5_log_dt, s5_b_re, s5_b_im, s5_c_re, s5_c_im, s5_d, s5_w_glu, w_branch, w_out, w_up, ffn_conv, ffn_conv_b, w_down):
    p = dict(norm_mix_pre=norm_mix_pre, norm_mix_post=norm_mix_post,
             norm_ffn_pre=norm_ffn_pre, norm_ffn_post=norm_ffn_post, w_in=w_in,
             mlstm_b_if=mlstm_b_if, mlstm_conv=mlstm_conv, mlstm_norm=mlstm_norm,
             diff_lambda=diff_lambda, diff_norm=diff_norm, rel_bias=rel_bias,
             s5_lambda_re=s5_lambda_re, s5_lambda_im=s5_lambda_im, s5_log_dt=s5_log_dt,
             s5_b_re=s5_b_re, s5_b_im=s5_b_im, s5_c_re=s5_c_re, s5_c_im=s5_c_im,
             s5_d=s5_d, s5_w_glu=s5_w_glu, w_branch=w_branch, w_out=w_out, w_up=w_up,
             ffn_conv=ffn_conv, ffn_conv_b=ffn_conv_b, w_down=w_down)
    batch, seq, _ = x.shape
    p['attn_bias_tiles'] = _attn_bias_tiles(rel_bias, min(ATTN_BLOCK, seq))
    x2d = x.reshape(batch * seq, D_MODEL)
    for layer in range(DEPTH):
        x2d = _layer(x2d, batch, seq, layer, p)
    return x2d.reshape(batch, seq, D_MODEL)
```

```python
import functools
import math

import jax
import jax.numpy as jnp
from jax import lax
from jax.experimental import pallas as pl
from jax.experimental.pallas import tpu as pltpu

F32 = jnp.float32
BF16 = jnp.bfloat16
HIGHEST = lax.Precision.HIGHEST

D_MODEL = 2048
DEPTH = 2
BRANCH_WIDTH = 1024
N_BRANCH = 3
M_HEADS = 4
M_HEAD_DIM = 256
M_CONV = 4
DA_HEADS = 4
DA_HEAD_DIM = 128
DA_V_DIM = 256
REL_BUCKETS = 32
REL_MAX_DIST = 128
S5_GROUP = 16
S5_GROUPS = 64
S5_STATE = 64
D_FF = 5632
FFN_CONV = 3
EPS = 1e-6

LANES = 128
SUBLANES = 8
MXU_TILE = 256
VMEM_LIMIT = 56 * 1024 * 1024

N_MAIN = 14336
COL_QM, COL_KM, COL_VM, COL_OM = 0, 1, 2, 3
COL_QD, COL_KD, COL_VD, COL_US = 4, 5, 6, 7
COL_GATE = 8
N_GATE_PAD = LANES

S5_SLABS = 8
S5_SLAB_STATES = 512

NEG_BIG = -1e30
LOG2E = math.log2(math.e)


def _cparams(sem):
    return pltpu.CompilerParams(dimension_semantics=sem, vmem_limit_bytes=VMEM_LIMIT)


def _sigmoid(x):
    return 0.5 * jnp.tanh(0.5 * x) + 0.5


def _gelu_tanh(x):
    c = math.sqrt(2.0 / math.pi)
    return 0.5 * x * (1.0 + jnp.tanh(c * (x + 0.044715 * (x * x * x))))


def _rms(x, gain):
    var = jnp.mean(x * x, axis=-1, keepdims=True)
    return x * lax.rsqrt(var + EPS) * gain


def _in_proj_kernel(x_ref, g_ref, wg_ref, w_ref, o_ref, og_ref, h_scr):
    @pl.when(pl.program_id(1) == 0)
    def _():
        h_scr[...] = _rms(x_ref[...], g_ref[...]).astype(BF16)
        og_ref[...] = jnp.dot(h_scr[...], wg_ref[...], preferred_element_type=F32)

    o_ref[...] = jnp.dot(h_scr[...], w_ref[...],
                         preferred_element_type=F32).astype(BF16)


def _in_proj(x2d, gain, w_gate, w_main, *, layer, tm, tn):
    m = x2d.shape[0]
    return pl.pallas_call(
        _in_proj_kernel,
        out_shape=(jax.ShapeDtypeStruct((m, N_MAIN), BF16),
                   jax.ShapeDtypeStruct((m, N_GATE_PAD), F32)),
        grid=(m // tm, N_MAIN // tn),
        in_specs=[pl.BlockSpec((tm, D_MODEL), lambda i, n: (i, 0)),
                  pl.BlockSpec((1, D_MODEL), lambda i, n: (0, 0)),
                  pl.BlockSpec((None, D_MODEL, N_GATE_PAD), lambda i, n: (layer, 0, 0)),
                  pl.BlockSpec((None, D_MODEL, tn), lambda i, n: (layer, 0, n))],
        out_specs=(pl.BlockSpec((tm, tn), lambda i, n: (i, n)),
                   pl.BlockSpec((tm, N_GATE_PAD), lambda i, n: (i, 0))),
        scratch_shapes=[pltpu.VMEM((tm, D_MODEL), BF16)],
        compiler_params=_cparams(("parallel", "arbitrary")),
        name="in_proj",
    )(x2d, gain, w_gate, w_main)


def _mlstm_kernel(q_ref, k_ref, v_ref, o_ref, gt_ref, gb_ref, cw_ref, ng_ref, y_ref,
                  c_scr, n_scr, m_scr, qe_scr, ke_scr, *, chunk):
    t = chunk
    hd = M_HEAD_DIM
    width = M_HEADS * hd

    @pl.when(pl.program_id(1) == 0)
    def _():
        c_scr[...] = jnp.zeros_like(c_scr)
        n_scr[...] = jnp.zeros_like(n_scr)
        m_scr[...] = jnp.zeros_like(m_scr)
        qe_scr[0:SUBLANES, :] = jnp.zeros((SUBLANES, width), F32)
        ke_scr[0:SUBLANES, :] = jnp.zeros((SUBLANES, width), F32)

    qe_scr[SUBLANES:SUBLANES + t, :] = q_ref[...].astype(F32)
    ke_scr[SUBLANES:SUBLANES + t, :] = k_ref[...].astype(F32)

    gates = gt_ref[...] + gb_ref[...]
    log_f = jnp.minimum(gates, 0.0) - jnp.log1p(jnp.exp(-jnp.abs(gates)))
    row = lax.broadcasted_iota(jnp.int32, (t, t), 0)
    col = lax.broadcasted_iota(jnp.int32, (t, t), 1)
    causal = col <= row
    cum = jnp.dot(causal.astype(F32), log_f, preferred_element_type=F32,
                  precision=HIGHEST)
    gates_t = gates.T
    cum_t = cum.T

    for h in range(M_HEADS):
        sl = slice(h * hd, (h + 1) * hd)
        ksl = slice(width + h * hd, width + (h + 1) * hd)
        qc = jnp.zeros((t, hd), F32)
        kc = jnp.zeros((t, hd), F32)
        for j in range(M_CONV):
            off = SUBLANES - (M_CONV - 1) + j
            qc = qc + cw_ref[j:j + 1, sl] * qe_scr[off:off + t, sl]
            kc = kc + cw_ref[j:j + 1, ksl] * ke_scr[off:off + t, sl]
        qc = qc * _sigmoid(qc)
        kc = kc * _sigmoid(kc) * (hd ** -0.5)
        qb = qc.astype(BF16)
        kb = kc.astype(BF16)
        vb = v_ref[:, sl]

        li_row = gates_t[h:h + 1, :]
        b_row = cum_t[M_HEADS + h:M_HEADS + h + 1, :]
        li_col = gates[:, h:h + 1]
        b_col = cum[:, M_HEADS + h:M_HEADS + h + 1]
        m_prev = m_scr[h:h + 1, 0:1]
        c_prev = c_scr[h]
        n_prev = n_scr[h:h + 1, :]

        dmat = jnp.where(causal, b_col - b_row + li_row, -jnp.inf)
        inter = b_col + m_prev
        m_t = jnp.maximum(inter, jnp.max(dmat, axis=-1, keepdims=True))
        s = lax.dot_general(qb, kb, (((1,), (1,)), ((), ())),
                            preferred_element_type=F32) * jnp.exp(dmat - m_t)
        w_inter = jnp.exp(inter - m_t)
        num = (jnp.dot(s.astype(BF16), vb, preferred_element_type=F32)
               + w_inter * jnp.dot(qb, c_prev.astype(BF16), preferred_element_type=F32))
        den = (jnp.sum(s, axis=-1, keepdims=True)
               + w_inter * jnp.sum(qc * n_prev, axis=-1, keepdims=True))
        hh = num / jnp.maximum(jnp.abs(den), jnp.exp(-m_t))

        g = cum[t - 1:t, M_HEADS + h:M_HEADS + h + 1]
        a_col = g - b_col + li_col
        m_new = jnp.maximum(g + m_prev, jnp.max(a_col, axis=0, keepdims=True))
        ws = jnp.exp(a_col - m_new)
        decay = jnp.exp(g + m_prev - m_new)
        kw = ws * kc
        c_scr[h] = decay * c_prev + lax.dot_general(
            kw.astype(BF16), vb, (((0,), (0,)), ((), ())), preferred_element_type=F32)
        n_scr[h:h + 1, :] = decay * n_prev + jnp.sum(kw, axis=0, keepdims=True)
        m_scr[h:h + 1, :] = jnp.broadcast_to(m_new, (1, LANES))

        hn = _rms(hh, ng_ref[:, sl])
        y_ref[:, sl] = (_sigmoid(o_ref[:, sl].astype(F32)) * hn).astype(BF16)

    qe_scr[0:SUBLANES, :] = qe_scr[t:t + SUBLANES, :]
    ke_scr[0:SUBLANES, :] = ke_scr[t:t + SUBLANES, :]


def _mlstm(proj, gates, gate_bias, conv_w, norm_g, *, batch, seq, chunk):
    nc = seq // chunk
    width = M_HEADS * M_HEAD_DIM

    def col_spec(cb):
        return pl.BlockSpec((chunk, width), lambda b, c: (b * nc + c, cb))

    return pl.pallas_call(
        functools.partial(_mlstm_kernel, chunk=chunk),
        out_shape=jax.ShapeDtypeStruct((batch * seq, width), BF16),
        grid=(batch, nc),
        in_specs=[col_spec(COL_QM), col_spec(COL_KM), col_spec(COL_VM), col_spec(COL_OM),
                  pl.BlockSpec((chunk, N_GATE_PAD), lambda b, c: (b * nc + c, 0)),
                  pl.BlockSpec((1, N_GATE_PAD), lambda b, c: (0, 0)),
                  pl.BlockSpec((M_CONV, 2 * width), lambda b, c: (0, 0)),
                  pl.BlockSpec((1, width), lambda b, c: (0, 0))],
        out_specs=pl.BlockSpec((chunk, width), lambda b, c: (b * nc + c, 0)),
        scratch_shapes=[pltpu.VMEM((M_HEADS, M_HEAD_DIM, M_HEAD_DIM), F32),
                        pltpu.VMEM((M_HEADS, M_HEAD_DIM), F32),
                        pltpu.VMEM((M_HEADS, LANES), F32),
                        pltpu.VMEM((chunk + SUBLANES, width), F32),
                        pltpu.VMEM((chunk + SUBLANES, width), F32)],
        compiler_params=_cparams(("parallel", "arbitrary")),
        name="mlstm",
    )(proj, proj, proj, proj, gates, gate_bias, conv_w, norm_g)


def _attn_kernel(sc_ref, q_ref, k_ref, v_ref, bias_ref, ng_ref, y_ref,
                 m_scr, l_scr, acc_scr, vt_scr, s_scr, *, blk, heads, out_scale):
    t = blk
    d = DA_HEAD_DIM
    w = DA_V_DIM
    h0 = pl.program_id(1) * heads
    i = pl.program_id(2)
    lam = sc_ref[0]

    qs = (q_ref[...].astype(F32) * (d ** -0.5 * LOG2E)).astype(BF16)

    @pl.when(i == 0)
    def _():
        vt_scr[...] = v_ref[...].astype(F32).T.astype(BF16)

    def block_step(j, bias_tile=None, first=False):
        start = pl.multiple_of(j * t, t)
        kb = k_ref[pl.ds(start, t), :]
        vtb = vt_scr[:, pl.ds(start, t)]
        far = bias_tile is None
        for ci in range(2 * heads):
            cols = slice(ci * d, (ci + 1) * d)
            s_scr[ci] = lax.dot_general(kb[:, cols], qs[:, cols], (((1,), (1,)), ((), ())),
                                        preferred_element_type=F32)
        for hh in range(heads):
            bias = sc_ref[1 + h0 + hh] if far else bias_ref[hh, bias_tile]
            for c in range(2):
                ci = 2 * hh + c
                s = s_scr[ci]
                if far:
                    m_blk = jnp.max(s, axis=0, keepdims=True) + bias
                else:
                    s = s + bias
                    m_blk = jnp.max(s, axis=0, keepdims=True)
                m_new = m_blk if first else jnp.maximum(m_scr[ci], m_blk)
                p = jnp.exp2(s - ((m_new - bias) if far else m_new))
                l_blk = jnp.sum(p, axis=0, keepdims=True)
                pv = jnp.dot(vtb[hh * w:(hh + 1) * w, :], p.astype(BF16),
                             preferred_element_type=F32)
                if first:
                    l_scr[ci] = l_blk
                    acc_scr[ci] = pv
                else:
                    alpha = jnp.exp2(m_scr[ci] - m_new)
                    l_scr[ci] = alpha * l_scr[ci] + l_blk
                    acc_scr[ci] = alpha * acc_scr[ci] + pv
                m_scr[ci] = m_new

    block_step(i, 0, first=True)

    @pl.when(i >= 1)
    def _():
        block_step(i - 1, 1)

    def far_body(j, carry):
        block_step(j)
        return carry

    lax.fori_loop(0, jnp.maximum(i - 1, 0), far_body, 0)

    for hh in range(heads):
        ca, cb = 2 * hh, 2 * hh + 1
        out_t = (acc_scr[ca] * (1.0 / l_scr[ca])
                 - lam * (acc_scr[cb] * (1.0 / l_scr[cb])))
        hs = slice(hh * w, (hh + 1) * w)
        y_ref[:, hs] = (_rms(out_t.T, ng_ref[:, hs]) * out_scale).astype(BF16)


def _diff_attn(scalars, proj, bias_tiles, norm_g, *, batch, seq, blk, heads, out_scale):
    nq = seq // blk
    kvw = heads * DA_V_DIM
    per_row = BRANCH_WIDTH // kvw
    return pl.pallas_call(
        functools.partial(_attn_kernel, blk=blk, heads=heads, out_scale=out_scale),
        out_shape=jax.ShapeDtypeStruct((batch * seq, DA_HEADS * DA_V_DIM), BF16),
        grid=(batch, DA_HEADS // heads, nq),
        in_specs=[pl.BlockSpec(memory_space=pltpu.SMEM),
                  pl.BlockSpec((blk, kvw), lambda b, h, i: (b * nq + i, COL_QD * per_row + h)),
                  pl.BlockSpec((seq, kvw), lambda b, h, i: (b, COL_KD * per_row + h)),
                  pl.BlockSpec((seq, kvw), lambda b, h, i: (b, COL_VD * per_row + h)),
                  pl.BlockSpec((heads, 2, blk, blk), lambda b, h, i: (h, 0, 0, 0)),
                  pl.BlockSpec((1, kvw), lambda b, h, i: (0, h))],
        out_specs=pl.BlockSpec((blk, kvw), lambda b, h, i: (b * nq + i, h)),
        scratch_shapes=[pltpu.VMEM((2 * heads, 1, blk), F32),
                        pltpu.VMEM((2 * heads, 1, blk), F32),
                        pltpu.VMEM((2 * heads, DA_V_DIM, blk), F32),
                        pltpu.VMEM((kvw, seq), BF16),
                        pltpu.VMEM((2 * heads, blk, blk), F32)],
        compiler_params=_cparams(("parallel", "parallel", "arbitrary")),
        name="diff_attn",
    )(scalars, proj, proj, proj, bias_tiles, norm_g)


def _s5_kernel(u_ref, wb_ref, wc_ref, are_ref, aim_ref, d_ref, wg_ref, y_ref,
               xre_scr, xim_scr, bu_scr, uil_scr, yil_scr, *, steps, batch):
    ns = S5_SLAB_STATES

    @pl.when(pl.program_id(0) == 0)
    def _():
        xre_scr[...] = jnp.zeros_like(xre_scr)
        xim_scr[...] = jnp.zeros_like(xim_scr)

    for b in range(batch):
        ub = u_ref[b].astype(F32)
        for c in range(S5_SLABS):
            uil_scr.at[c][pl.ds(b, steps, stride=batch), :] = ub[:, c * LANES:(c + 1) * LANES]

    ys = []
    for s in range(S5_SLABS):
        us = uil_scr[s]
        buf = bu_scr.at[s % 2]
        buf[...] = jnp.dot(us.astype(BF16), wb_ref[s], preferred_element_type=F32)
        a_re = jnp.broadcast_to(are_ref[s], (batch, ns))
        a_im = jnp.broadcast_to(aim_ref[s], (batch, ns))

        x_re = xre_scr[s]
        x_im = xim_scr[s]
        for tt in range(steps):
            rs = slice(tt * batch, (tt + 1) * batch)
            n_re = a_re * x_re - a_im * x_im + buf[rs, 0:ns]
            n_im = a_re * x_im + a_im * x_re + buf[rs, ns:2 * ns]
            buf[rs, 0:ns] = n_re
            buf[rs, ns:2 * ns] = n_im
            x_re, x_im = n_re, n_im
        xre_scr[s] = x_re
        xim_scr[s] = x_im
        y = (jnp.dot(buf[...].astype(BF16), wc_ref[s], preferred_element_type=F32)
             + d_ref[:, s * LANES:(s + 1) * LANES] * us)
        ys.append(_gelu_tanh(y).astype(BF16))

    yb = jnp.concatenate(ys, axis=1)
    half = BRANCH_WIDTH
    a = jnp.dot(yb, wg_ref[:, 0:half], preferred_element_type=F32)
    g = jnp.dot(yb, wg_ref[:, half:2 * half], preferred_element_type=F32)
    out = a * _sigmoid(g)
    for c in range(S5_SLABS):
        yil_scr[c] = out[:, c * LANES:(c + 1) * LANES]
    for b in range(batch):
        for c in range(S5_SLABS):
            y_ref[b, :, c * LANES:(c + 1) * LANES] = (
                yil_scr.at[c][pl.ds(b, steps, stride=batch), :].astype(BF16))


def _s5(proj3d, wb, wc, a_re, a_im, d_skip, w_glu, *, layer, steps):
    batch, seq, _ = proj3d.shape
    rows = steps * batch
    ns = S5_SLAB_STATES
    const3 = lambda i: (0, 0, 0)
    return pl.pallas_call(
        functools.partial(_s5_kernel, steps=steps, batch=batch),
        out_shape=jax.ShapeDtypeStruct((batch, seq, BRANCH_WIDTH), BF16),
        grid=(seq // steps,),
        in_specs=[pl.BlockSpec((batch, steps, BRANCH_WIDTH), lambda i: (0, i, COL_US)),
                  pl.BlockSpec((S5_SLABS, LANES, 2 * ns), const3),
                  pl.BlockSpec((S5_SLABS, 2 * ns, LANES), const3),
                  pl.BlockSpec((S5_SLABS, 1, ns), const3),
                  pl.BlockSpec((S5_SLABS, 1, ns), const3),
                  pl.BlockSpec((1, BRANCH_WIDTH), lambda i: (0, 0)),
                  pl.BlockSpec((None, BRANCH_WIDTH, 2 * BRANCH_WIDTH),
                               lambda i: (layer, 0, 0))],
        out_specs=pl.BlockSpec((batch, steps, BRANCH_WIDTH), lambda i: (0, i, 0)),
        scratch_shapes=[pltpu.VMEM((S5_SLABS, batch, ns), F32),
                        pltpu.VMEM((S5_SLABS, batch, ns), F32),
                        pltpu.VMEM((2, rows, 2 * ns), F32),
                        pltpu.VMEM((S5_SLABS, rows, LANES), F32),
                        pltpu.VMEM((S5_SLABS, rows, LANES), F32)],
        compiler_params=_cparams(("arbitrary",)),
        name="s5",
    )(proj3d, wb, wc, a_re, a_im, d_skip, w_glu)


def _merge_kernel(ya_ref, yb_ref, yc_ref, g0_ref, g1_ref, g2_ref, x_ref, wbr_ref,
                  wo_ref, ng_ref, o_ref):
    merged = None
    for n, (y_ref, g_ref) in enumerate(((ya_ref, g0_ref), (yb_ref, g1_ref),
                                        (yc_ref, g2_ref))):
        z = jnp.dot(y_ref[...], wbr_ref[n], preferred_element_type=F32)
        term = _sigmoid(g_ref[...].astype(F32)) * z
        merged = term if merged is None else merged + term
    mix = jnp.dot(merged.astype(BF16), wo_ref[...], preferred_element_type=F32)
    o_ref[...] = x_ref[...] + _rms(mix, ng_ref[...])


def _merge(y_a, y_b, y_c, proj, x2d, w_branch, w_out, norm_g, *, layer, tm):
    m = x2d.shape[0]
    row = lambda i: (i, 0)

    def gate_spec(n):
        return pl.BlockSpec((tm, D_MODEL), lambda i: (i, COL_GATE // 2 + n))

    return pl.pallas_call(
        _merge_kernel,
        out_shape=jax.ShapeDtypeStruct((m, D_MODEL), F32),
        grid=(m // tm,),
        in_specs=[pl.BlockSpec((tm, BRANCH_WIDTH), row),
                  pl.BlockSpec((tm, BRANCH_WIDTH), row),
                  pl.BlockSpec((tm, BRANCH_WIDTH), row),
                  gate_spec(0), gate_spec(1), gate_spec(2),
                  pl.BlockSpec((tm, D_MODEL), row),
                  pl.BlockSpec((None, N_BRANCH, BRANCH_WIDTH, D_MODEL),
                               lambda i: (layer, 0, 0, 0), pipeline_mode=pl.Buffered(1)),
                  pl.BlockSpec((None, D_MODEL, D_MODEL), lambda i: (layer, 0, 0),
                               pipeline_mode=pl.Buffered(1)),
                  pl.BlockSpec((1, D_MODEL), lambda i: (0, 0))],
        out_specs=pl.BlockSpec((tm, D_MODEL), row),
        compiler_params=_cparams(("parallel",)),
        name="merge",
    )(y_a, y_b, y_c, proj, proj, proj, x2d, w_branch, w_out, norm_g)


def _ffn_kernel(x_ref, g_ref, wa_ref, wv_ref, cwa_ref, cwv_ref, cba_ref,
                cbv_ref, wd_ref, ng_ref, o_ref, h_scr, tail_scr, *, tl):
    i = pl.program_id(1)
    f = pl.program_id(2)
    nf = pl.num_programs(2)
    hal = SUBLANES

    @pl.when(f == 0)
    def _():
        h_scr[...] = _rms(x_ref[0], g_ref[...]).astype(BF16)
        o_ref[0] = jnp.zeros((tl, D_MODEL), F32)

    @pl.when(i == 0)
    def _():
        tail_scr[f] = jnp.zeros(tail_scr.shape[1:], F32)

    hb = h_scr[...]

    def conv(part, w_ref, cw_ref, cb_ref):
        up = jnp.dot(hb, w_ref[...], preferred_element_type=F32)
        ext = jnp.concatenate([tail_scr[f, part], up], axis=0)
        tail_scr[f, part] = up[tl - hal:tl, :]
        out = cb_ref[...] + cw_ref[FFN_CONV - 1:FFN_CONV, :] * up
        for j in range(FFN_CONV - 1):
            off = hal - (FFN_CONV - 1) + j
            out = out + cw_ref[j:j + 1, :] * ext[off:off + tl, :]
        return out

    a = conv(0, wa_ref, cwa_ref, cba_ref)
    v = conv(1, wv_ref, cwv_ref, cbv_ref)
    act = (_gelu_tanh(a) * v).astype(BF16)
    o_ref[0] += jnp.dot(act, wd_ref[...], preferred_element_type=F32)

    @pl.when(f == nf - 1)
    def _():
        o_ref[0] = x_ref[0] + _rms(o_ref[0], ng_ref[...])


def _ffn(x3d, gain, w_up, conv_w, conv_b, w_down, norm_g, *, layer, tl, tf):
    batch, seq, _ = x3d.shape
    nfb = D_FF // tf
    hal = SUBLANES
    return pl.pallas_call(
        functools.partial(_ffn_kernel, tl=tl),
        out_shape=jax.ShapeDtypeStruct((batch, seq, D_MODEL), F32),
        grid=(batch, seq // tl, nfb),
        in_specs=[pl.BlockSpec((1, tl, D_MODEL), lambda b, i, f: (b, i, 0)),
                  pl.BlockSpec((1, D_MODEL), lambda b, i, f: (0, 0)),
                  pl.BlockSpec((None, D_MODEL, tf), lambda b, i, f: (layer, 0, f)),
                  pl.BlockSpec((None, D_MODEL, tf), lambda b, i, f: (layer, 0, nfb + f)),
                  pl.BlockSpec((FFN_CONV, tf), lambda b, i, f: (0, f)),
                  pl.BlockSpec((FFN_CONV, tf), lambda b, i, f: (0, nfb + f)),
                  pl.BlockSpec((1, tf), lambda b, i, f: (0, f)),
                  pl.BlockSpec((1, tf), lambda b, i, f: (0, nfb + f)),
                  pl.BlockSpec((None, tf, D_MODEL), lambda b, i, f: (layer, f, 0)),
                  pl.BlockSpec((1, D_MODEL), lambda b, i, f: (0, 0))],
        out_specs=pl.BlockSpec((1, tl, D_MODEL), lambda b, i, f: (b, i, 0)),
        scratch_shapes=[pltpu.VMEM((tl, D_MODEL), BF16),
                        pltpu.VMEM((nfb, 2, hal, tf), F32)],
        compiler_params=_cparams(("parallel", "arbitrary", "arbitrary")),
        name="conv_ffn",
    )(x3d, gain, w_up, w_up, conv_w, conv_w, conv_b, conv_b, w_down, norm_g)


def _t5_bucket(dist):
    n = jnp.maximum(dist, 0)
    max_exact = REL_BUCKETS // 2
    nf = jnp.maximum(n, 1).astype(F32)
    large = max_exact + (jnp.log(nf / max_exact) / math.log(REL_MAX_DIST / max_exact)
                         * (REL_BUCKETS - max_exact)).astype(jnp.int32)
    large = jnp.minimum(large, REL_BUCKETS - 1)
    return jnp.where(n < max_exact, n, large)


def _attn_bias_tiles(rel_bias, blk):
    qi = jnp.arange(blk, dtype=jnp.int32)[None, :]
    kj = jnp.arange(blk, dtype=jnp.int32)[:, None]
    table = rel_bias.astype(F32) * LOG2E
    last = table[REL_BUCKETS - 1][:, None, None]
    tiles = []
    for off in (0, blk):
        dist = qi - kj + off
        bucket = _t5_bucket(dist)
        bias = jnp.broadcast_to(last, (DA_HEADS, blk, blk))
        for b in range(REL_BUCKETS - 1):
            bias = jnp.where(bucket == b, table[b][:, None, None], bias)
        tiles.append(jnp.where(dist >= 0, bias, NEG_BIG))
    return jnp.stack(tiles, axis=1)


def _s5_params(lam_re, lam_im, log_dt, b_re, b_im, c_re, c_im):
    dt = jnp.exp(log_dt)[:, None]
    mag = jnp.exp(lam_re * dt)
    a_re = mag * jnp.cos(lam_im * dt)
    a_im = mag * jnp.sin(lam_im * dt)
    den = lam_re * lam_re + lam_im * lam_im
    z_re = ((a_re - 1.0) * lam_re + a_im * lam_im) / den
    z_im = (a_im * lam_re - (a_re - 1.0) * lam_im) / den
    bb_re = z_re[..., None] * b_re - z_im[..., None] * b_im
    bb_im = z_re[..., None] * b_im + z_im[..., None] * b_re
    gs = S5_GROUPS // S5_SLABS
    eye = jnp.eye(gs, dtype=F32)

    def in_blocks(bb):
        bb = bb.reshape(S5_SLABS, gs, S5_STATE, S5_GROUP)
        w = jnp.einsum('sgpc,gh->sgchp', bb, eye)
        return w.reshape(S5_SLABS, gs * S5_GROUP, gs * S5_STATE)

    def out_blocks(cc):
        cc = cc.reshape(S5_SLABS, gs, S5_GROUP, S5_STATE)
        w = jnp.einsum('sgcp,gh->sgphc', cc, eye)
        return w.reshape(S5_SLABS, gs * S5_STATE, gs * S5_GROUP)

    wb = jnp.concatenate([in_blocks(bb_re), in_blocks(bb_im)], axis=-1).astype(BF16)
    wc = jnp.concatenate([out_blocks(c_re), out_blocks(-c_im)], axis=-2).astype(BF16)
    a_re = a_re.reshape(S5_SLABS, 1, S5_SLAB_STATES)
    a_im = a_im.reshape(S5_SLABS, 1, S5_SLAB_STATES)
    return wb, wc, a_re, a_im


IN_PROJ_TM, IN_PROJ_TN = 1024, 2048
MLSTM_CHUNK = 128
ATTN_BLOCK = 512
ATTN_HEADS_PER_STEP = 2
S5_STEPS = 32
MERGE_TM = 256
FFN_TL, FFN_TF = 512, 512


def _layer(x2d, batch, seq, layer, p):
    n_if = 2 * M_HEADS
    proj, gates = _in_proj(x2d, p['norm_mix_pre'][layer][None, :], p['w_gate'],
                           p['w_main'], layer=layer, tm=min(IN_PROJ_TM, batch * seq),
                           tn=IN_PROJ_TN)

    gate_bias = jnp.pad(p['mlstm_b_if'][layer].reshape(1, n_if),
                        ((0, 0), (0, N_GATE_PAD - n_if)))
    y_a = _mlstm(proj, gates, gate_bias, p['mlstm_conv'][layer],
                 p['mlstm_norm'][layer][None, :], batch=batch, seq=seq,
                 chunk=min(MLSTM_CHUNK, seq))

    lambda_init = 0.8 - 0.6 * math.exp(-0.3 * layer)
    lam = p['diff_lambda'][layer]
    lam_full = (jnp.exp(jnp.sum(lam[0] * lam[1])) - jnp.exp(jnp.sum(lam[2] * lam[3]))
                + lambda_init)
    blk = min(ATTN_BLOCK, seq)
    scalars = jnp.concatenate(
        [lam_full[None], p['rel_bias'][REL_BUCKETS - 1, :] * LOG2E]).astype(F32)
    y_b = _diff_attn(scalars, proj, p['attn_bias_tiles'],
                     p['diff_norm'][layer][None, :], batch=batch, seq=seq, blk=blk,
                     heads=ATTN_HEADS_PER_STEP, out_scale=1.0 - lambda_init)

    wb, wc, a_re, a_im = _s5_params(
        p['s5_lambda_re'][layer], p['s5_lambda_im'][layer], p['s5_log_dt'][layer],
        p['s5_b_re'][layer], p['s5_b_im'][layer], p['s5_c_re'][layer], p['s5_c_im'][layer])
    y_c = _s5(proj.reshape(batch, seq, N_MAIN), wb, wc, a_re, a_im,
              p['s5_d'][layer][None, :], p['w_glu'], layer=layer,
              steps=min(S5_STEPS, seq)).reshape(batch * seq, BRANCH_WIDTH)

    x2d = _merge(y_a, y_b, y_c, proj, x2d, p['w_branch'], p['w_out'],
                 p['norm_mix_post'][layer][None, :], layer=layer, tm=MERGE_TM)

    x3d = _ffn(x2d.reshape(batch, seq, D_MODEL), p['norm_ffn_pre'][layer][None, :],
               p['w_up'], p['ffn_conv'][layer], p['ffn_conv_b'][layer][None, :],
               p['w_down'], p['norm_ffn_post'][layer][None, :], layer=layer,
               tl=min(FFN_TL, seq), tf=FFN_TF)
    return x3d.reshape(batch * seq, D_MODEL)


def kernel(x, norm_mix_pre, norm_mix_post, norm_ffn_pre, norm_ffn_post, w_in, mlstm_b_if, mlstm_conv, mlstm_norm, diff_lambda, diff_norm, rel_bias, s5_lambda_re, s5_lambda_im, s5_log_dt, s5_b_re, s5_b_im, s5_c_re, s5_c_im, s5_d, s5_w_glu, w_branch, w_out, w_up, ffn_conv, ffn_conv_b, w_down):
    n_if = 2 * M_HEADS
    split = 4 * M_HEADS * M_HEAD_DIM
    w_main = jnp.concatenate([w_in[:, :, :split], w_in[:, :, split + n_if:]],
                             axis=2).astype(BF16)
    w_gate = jnp.pad(w_in[:, :, split:split + n_if],
                     ((0, 0), (0, 0), (0, N_GATE_PAD - n_if))).astype(BF16)
    p = dict(norm_mix_pre=norm_mix_pre, norm_mix_post=norm_mix_post,
             norm_ffn_pre=norm_ffn_pre, norm_ffn_post=norm_ffn_post,
             w_main=w_main, w_gate=w_gate,
             mlstm_b_if=mlstm_b_if, mlstm_conv=mlstm_conv, mlstm_norm=mlstm_norm,
             diff_lambda=diff_lambda, diff_norm=diff_norm, rel_bias=rel_bias,
             s5_lambda_re=s5_lambda_re, s5_lambda_im=s5_lambda_im, s5_log_dt=s5_log_dt,
             s5_b_re=s5_b_re, s5_b_im=s5_b_im, s5_c_re=s5_c_re, s5_c_im=s5_c_im,
             s5_d=s5_d, w_glu=s5_w_glu.astype(BF16), w_branch=w_branch.astype(BF16),
             w_out=w_out.astype(BF16), w_up=w_up.astype(BF16),
             ffn_conv=ffn_conv, ffn_conv_b=ffn_conv_b, w_down=w_down.astype(BF16))
    batch, seq, _ = x.shape
    p['attn_bias_tiles'] = _attn_bias_tiles(rel_bias, min(ATTN_BLOCK, seq))
    x2d = x.reshape(batch * seq, D_MODEL)
    for layer in range(DEPTH):
        x2d = _layer(x2d, batch, seq, layer, p)
    return x2d.reshape(batch, seq, D_MODEL)
```

```python
import functools
import math

import jax
import jax.numpy as jnp
from jax import lax
from jax.experimental import pallas as pl
from jax.experimental.pallas import tpu as pltpu

F32 = jnp.float32
BF16 = jnp.bfloat16
HIGHEST = lax.Precision.HIGHEST

D_MODEL = 2048
DEPTH = 2
BRANCH_WIDTH = 1024
N_BRANCH = 3
M_HEADS = 4
M_HEAD_DIM = 256
M_CONV = 4
DA_HEADS = 4
DA_HEAD_DIM = 128
DA_V_DIM = 256
REL_BUCKETS = 32
REL_MAX_DIST = 128
S5_GROUP = 16
S5_GROUPS = 64
S5_STATE = 64
D_FF = 5632
FFN_CONV = 3
EPS = 1e-6

LANES = 128
SUBLANES = 8
VMEM_LIMIT = 56 * 1024 * 1024

N_MAIN = 14336
COL_QM, COL_KM, COL_VM, COL_OM = 0, 1, 2, 3
COL_QD, COL_KD, COL_VD, COL_US = 4, 5, 6, 7
COL_GATE = 8
N_GATE_PAD = LANES

S5_SLABS = 8
S5_SLAB_STATES = 512

NEG_BIG = -1e30
LOG2E = math.log2(math.e)


def _cparams(sem):
    return pltpu.CompilerParams(dimension_semantics=sem, vmem_limit_bytes=VMEM_LIMIT)


def _sigmoid(x):
    return 0.5 * jnp.tanh(0.5 * x) + 0.5


def _gelu_tanh(x):
    c = math.sqrt(2.0 / math.pi)
    return 0.5 * x * (1.0 + jnp.tanh(c * (x + 0.044715 * (x * x * x))))


def _rms(x, gain):
    var = jnp.mean(x * x, axis=-1, keepdims=True)
    return x * lax.rsqrt(var + EPS) * gain


def _in_proj_kernel(x_ref, g_ref, wg_ref, w_ref, o_ref, og_ref, h_scr):
    @pl.when(pl.program_id(1) == 0)
    def _():
        h_scr[...] = _rms(x_ref[...], g_ref[...]).astype(BF16)
        og_ref[...] = jnp.dot(h_scr[...], wg_ref[...], preferred_element_type=F32)

    o_ref[...] = jnp.dot(h_scr[...], w_ref[...],
                         preferred_element_type=F32).astype(BF16)


def _in_proj(x2d, gain, w_gate, w_main, *, layer, tm, tn):
    m = x2d.shape[0]
    return pl.pallas_call(
        _in_proj_kernel,
        out_shape=(jax.ShapeDtypeStruct((m, N_MAIN), BF16),
                   jax.ShapeDtypeStruct((m, N_GATE_PAD), F32)),
        grid=(m // tm, N_MAIN // tn),
        in_specs=[pl.BlockSpec((tm, D_MODEL), lambda i, n: (i, 0)),
                  pl.BlockSpec((1, D_MODEL), lambda i, n: (0, 0)),
                  pl.BlockSpec((None, D_MODEL, N_GATE_PAD), lambda i, n: (layer, 0, 0)),
                  pl.BlockSpec((None, D_MODEL, tn), lambda i, n: (layer, 0, n))],
        out_specs=(pl.BlockSpec((tm, tn), lambda i, n: (i, n)),
                   pl.BlockSpec((tm, N_GATE_PAD), lambda i, n: (i, 0))),
        scratch_shapes=[pltpu.VMEM((tm, D_MODEL), BF16)],
        compiler_params=_cparams(("parallel", "arbitrary")),
        name="in_proj",
    )(x2d, gain, w_gate, w_main)


def _mlstm_chunk(r0, q_ref, k_ref, v_ref, o_ref, gt_ref, gb_ref, cw_ref, ng_ref, y_ref,
                 c_scr, n_scr, m_scr, qe_scr, ke_scr, *, t):
    hd = M_HEAD_DIM
    width = M_HEADS * hd
    rows = slice(r0, r0 + t)

    qe_scr[SUBLANES:SUBLANES + t, :] = q_ref[rows, :].astype(F32)
    ke_scr[SUBLANES:SUBLANES + t, :] = k_ref[rows, :].astype(F32)

    gates = gt_ref[rows, :] + gb_ref[...]
    log_f = jnp.minimum(gates, 0.0) - jnp.log1p(jnp.exp(-jnp.abs(gates)))
    row = lax.broadcasted_iota(jnp.int32, (t, t), 0)
    col = lax.broadcasted_iota(jnp.int32, (t, t), 1)
    causal = col <= row
    cum = jnp.dot(causal.astype(F32), log_f, preferred_element_type=F32,
                  precision=HIGHEST)
    gates_t = gates.T
    cum_t = cum.T

    for h in range(M_HEADS):
        sl = slice(h * hd, (h + 1) * hd)
        ksl = slice(width + h * hd, width + (h + 1) * hd)
        qc = jnp.zeros((t, hd), F32)
        kc = jnp.zeros((t, hd), F32)
        for j in range(M_CONV):
            off = SUBLANES - (M_CONV - 1) + j
            qc = qc + cw_ref[j:j + 1, sl] * qe_scr[off:off + t, sl]
            kc = kc + cw_ref[j:j + 1, ksl] * ke_scr[off:off + t, sl]
        qc = qc * _sigmoid(qc)
        kc = kc * _sigmoid(kc) * (hd ** -0.5)
        qb = qc.astype(BF16)
        kb = kc.astype(BF16)
        vb = v_ref[rows, sl]

        li_row = gates_t[h:h + 1, :]
        b_row = cum_t[M_HEADS + h:M_HEADS + h + 1, :]
        li_col = gates[:, h:h + 1]
        b_col = cum[:, M_HEADS + h:M_HEADS + h + 1]
        m_prev = m_scr[h:h + 1, 0:1]
        c_prev = c_scr[h]
        n_prev = n_scr[h:h + 1, :]

        dmat = jnp.where(causal, b_col - b_row + li_row, -jnp.inf)
        inter = b_col + m_prev
        m_t = jnp.maximum(inter, jnp.max(dmat, axis=-1, keepdims=True))
        s = lax.dot_general(qb, kb, (((1,), (1,)), ((), ())),
                            preferred_element_type=F32) * jnp.exp(dmat - m_t)
        w_inter = jnp.exp(inter - m_t)
        num = (jnp.dot(s.astype(BF16), vb, preferred_element_type=F32)
               + w_inter * jnp.dot(qb, c_prev.astype(BF16), preferred_element_type=F32))
        den = (jnp.sum(s, axis=-1, keepdims=True)
               + w_inter * jnp.sum(qc * n_prev, axis=-1, keepdims=True))
        hh = num / jnp.maximum(jnp.abs(den), jnp.exp(-m_t))

        g = cum[t - 1:t, M_HEADS + h:M_HEADS + h + 1]
        a_col = g - b_col + li_col
        m_new = jnp.maximum(g + m_prev, jnp.max(a_col, axis=0, keepdims=True))
        ws = jnp.exp(a_col - m_new)
        decay = jnp.exp(g + m_prev - m_new)
        kw = ws * kc
        c_scr[h] = decay * c_prev + lax.dot_general(
            kw.astype(BF16), vb, (((0,), (0,)), ((), ())), preferred_element_type=F32)
        n_scr[h:h + 1, :] = decay * n_prev + jnp.sum(kw, axis=0, keepdims=True)
        m_scr[h:h + 1, :] = jnp.broadcast_to(m_new, (1, LANES))

        hn = _rms(hh, ng_ref[:, sl])
        y_ref[rows, sl] = (_sigmoid(o_ref[rows, sl].astype(F32)) * hn).astype(BF16)

    qe_scr[0:SUBLANES, :] = qe_scr[t:t + SUBLANES, :]
    ke_scr[0:SUBLANES, :] = ke_scr[t:t + SUBLANES, :]


def _mix_kernel(q_ref, k_ref, v_ref, o_ref, gt_ref, gb_ref, cw_ref, ng_ref,
                u_ref, wb_ref, wc_ref, are_ref, aim_ref, d_ref, wg_ref,
                ya_ref, yc_ref,
                c_scr, n_scr, m_scr, qe_scr, ke_scr,
                xre_scr, xim_scr, bu_scr, uil_scr, yil_scr,
                *, chunk, steps, batch, steps_per_seq):
    i = pl.program_id(0)
    ns = S5_SLAB_STATES
    rows = steps * batch
    n_chunks = rows // chunk
    width = M_HEADS * M_HEAD_DIM

    @pl.when(i == 0)
    def _():
        xre_scr[...] = jnp.zeros_like(xre_scr)
        xim_scr[...] = jnp.zeros_like(xim_scr)

    @pl.when(i % steps_per_seq == 0)
    def _():
        c_scr[...] = jnp.zeros_like(c_scr)
        n_scr[...] = jnp.zeros_like(n_scr)
        m_scr[...] = jnp.zeros_like(m_scr)
        qe_scr[0:SUBLANES, :] = jnp.zeros((SUBLANES, width), F32)
        ke_scr[0:SUBLANES, :] = jnp.zeros((SUBLANES, width), F32)

    def mlstm_chunk(j):
        _mlstm_chunk(j * chunk, q_ref, k_ref, v_ref, o_ref, gt_ref, gb_ref, cw_ref, ng_ref,
                     ya_ref, c_scr, n_scr, m_scr, qe_scr, ke_scr, t=chunk)

    slab_of = [((j + 1) * S5_SLABS) // n_chunks - 1 for j in range(n_chunks)]

    for b in range(batch):
        ub = u_ref[b].astype(F32)
        for c in range(S5_SLABS):
            uil_scr.at[c][pl.ds(b, steps, stride=batch), :] = ub[:, c * LANES:(c + 1) * LANES]

    ys = []
    for s in range(S5_SLABS):
        us = uil_scr[s]
        buf = bu_scr.at[s % 2]
        buf[...] = jnp.dot(us.astype(BF16), wb_ref[s], preferred_element_type=F32)
        a_re = jnp.broadcast_to(are_ref[s], (batch, ns))
        a_im = jnp.broadcast_to(aim_ref[s], (batch, ns))

        x_re = xre_scr[s]
        x_im = xim_scr[s]
        for tt in range(steps):
            rs = slice(tt * batch, (tt + 1) * batch)
            n_re = a_re * x_re - a_im * x_im + buf[rs, 0:ns]
            n_im = a_re * x_im + a_im * x_re + buf[rs, ns:2 * ns]
            buf[rs, 0:ns] = n_re
            buf[rs, ns:2 * ns] = n_im
            x_re, x_im = n_re, n_im
        xre_scr[s] = x_re
        xim_scr[s] = x_im
        y = (jnp.dot(buf[...].astype(BF16), wc_ref[s], preferred_element_type=F32)
             + d_ref[:, s * LANES:(s + 1) * LANES] * us)
        ys.append(_gelu_tanh(y).astype(BF16))

        for j in range(n_chunks):
            if slab_of[j] == s:
                mlstm_chunk(j)

    yb = jnp.concatenate(ys, axis=1)
    half = BRANCH_WIDTH
    a = jnp.dot(yb, wg_ref[:, 0:half], preferred_element_type=F32)
    g = jnp.dot(yb, wg_ref[:, half:2 * half], preferred_element_type=F32)
    out = a * _sigmoid(g)
    for c in range(S5_SLABS):
        yil_scr[c] = out[:, c * LANES:(c + 1) * LANES]
    for b in range(batch):
        for c in range(S5_SLABS):
            yc_ref[b, :, c * LANES:(c + 1) * LANES] = (
                yil_scr.at[c][pl.ds(b, steps, stride=batch), :].astype(BF16))


def _mlstm_s5(proj, gates, gate_bias, conv_w, norm_g, wb, wc, a_re, a_im, d_skip, w_glu, *,
              layer, batch, seq, steps, chunk):
    rows = steps * batch
    chunk = min(chunk, rows)
    assert seq % rows == 0 and rows % chunk == 0 and S5_SLABS % (rows // chunk) == 0
    ns = S5_SLAB_STATES
    width = M_HEADS * M_HEAD_DIM
    const2 = lambda i: (0, 0)
    const3 = lambda i: (0, 0, 0)

    def col_spec(cb):
        return pl.BlockSpec((rows, width), lambda i: (i, cb))

    return pl.pallas_call(
        functools.partial(_mix_kernel, chunk=chunk, steps=steps, batch=batch,
                          steps_per_seq=seq // rows),
        out_shape=(jax.ShapeDtypeStruct((batch * seq, width), BF16),
                   jax.ShapeDtypeStruct((batch, seq, BRANCH_WIDTH), BF16)),
        grid=(seq // steps,),
        in_specs=[col_spec(COL_QM), col_spec(COL_KM), col_spec(COL_VM), col_spec(COL_OM),
                  pl.BlockSpec((rows, N_GATE_PAD), lambda i: (i, 0)),
                  pl.BlockSpec((1, N_GATE_PAD), const2),
                  pl.BlockSpec((M_CONV, 2 * width), const2),
                  pl.BlockSpec((1, width), const2),
                  pl.BlockSpec((batch, steps, BRANCH_WIDTH), lambda i: (0, i, COL_US)),
                  pl.BlockSpec((S5_SLABS, LANES, 2 * ns), const3),
                  pl.BlockSpec((S5_SLABS, 2 * ns, LANES), const3),
                  pl.BlockSpec((S5_SLABS, 1, ns), const3),
                  pl.BlockSpec((S5_SLABS, 1, ns), const3),
                  pl.BlockSpec((1, BRANCH_WIDTH), const2),
                  pl.BlockSpec((None, BRANCH_WIDTH, 2 * BRANCH_WIDTH),
                               lambda i: (layer, 0, 0))],
        out_specs=(pl.BlockSpec((rows, width), lambda i: (i, 0)),
                   pl.BlockSpec((batch, steps, BRANCH_WIDTH), lambda i: (0, i, 0))),
        scratch_shapes=[pltpu.VMEM((M_HEADS, M_HEAD_DIM, M_HEAD_DIM), F32),
                        pltpu.VMEM((M_HEADS, M_HEAD_DIM), F32),
                        pltpu.VMEM((M_HEADS, LANES), F32),
                        pltpu.VMEM((chunk + SUBLANES, width), F32),
                        pltpu.VMEM((chunk + SUBLANES, width), F32),
                        pltpu.VMEM((S5_SLABS, batch, ns), F32),
                        pltpu.VMEM((S5_SLABS, batch, ns), F32),
                        pltpu.VMEM((2, rows, 2 * ns), F32),
                        pltpu.VMEM((S5_SLABS, rows, LANES), F32),
                        pltpu.VMEM((S5_SLABS, rows, LANES), F32)],
        compiler_params=_cparams(("arbitrary",)),
        name="mlstm_s5",
    )(proj, proj, proj, proj, gates, gate_bias, conv_w, norm_g,
      proj.reshape(batch, seq, N_MAIN), wb, wc, a_re, a_im, d_skip, w_glu)


def _attn_kernel(sc_ref, q_ref, k_ref, v_ref, bias_ref, ng_ref, y_ref,
                 m_scr, l_scr, acc_scr, vt_scr, s_scr, *, blk, heads, out_scale):
    t = blk
    d = DA_HEAD_DIM
    w = DA_V_DIM
    h0 = pl.program_id(1) * heads
    i = pl.program_id(2)
    lam = sc_ref[0]

    qs = (q_ref[...].astype(F32) * (d ** -0.5 * LOG2E)).astype(BF16)

    @pl.when(i == 0)
    def _():
        vt_scr[...] = v_ref[...].astype(F32).T.astype(BF16)

    def block_step(j, bias_tile=None, first=False):
        start = pl.multiple_of(j * t, t)
        kb = k_ref[pl.ds(start, t), :]
        vtb = vt_scr[:, pl.ds(start, t)]
        far = bias_tile is None
        for ci in range(2 * heads):
            cols = slice(ci * d, (ci + 1) * d)
            s_scr[ci] = lax.dot_general(kb[:, cols], qs[:, cols], (((1,), (1,)), ((), ())),
                                        preferred_element_type=F32)
        for hh in range(heads):
            bias = sc_ref[1 + h0 + hh] if far else bias_ref[hh, bias_tile]
            for c in range(2):
                ci = 2 * hh + c
                s = s_scr[ci]
                if far:
                    m_blk = jnp.max(s, axis=0, keepdims=True) + bias
                else:
                    s = s + bias
                    m_blk = jnp.max(s, axis=0, keepdims=True)
                m_new = m_blk if first else jnp.maximum(m_scr[ci], m_blk)
                p = jnp.exp2(s - ((m_new - bias) if far else m_new))
                l_blk = jnp.sum(p, axis=0, keepdims=True)
                pv = jnp.dot(vtb[hh * w:(hh + 1) * w, :], p.astype(BF16),
                             preferred_element_type=F32)
                if first:
                    l_scr[ci] = l_blk
                    acc_scr[ci] = pv
                else:
                    alpha = jnp.exp2(m_scr[ci] - m_new)
                    l_scr[ci] = alpha * l_scr[ci] + l_blk
                    acc_scr[ci] = alpha * acc_scr[ci] + pv
                m_scr[ci] = m_new

    block_step(i, 0, first=True)

    @pl.when(i >= 1)
    def _():
        block_step(i - 1, 1)

    def far_body(j, carry):
        block_step(j)
        return carry

    lax.fori_loop(0, jnp.maximum(i - 1, 0), far_body, 0)

    for hh in range(heads):
        ca, cb = 2 * hh, 2 * hh + 1
        out_t = (acc_scr[ca] * (1.0 / l_scr[ca])
                 - lam * (acc_scr[cb] * (1.0 / l_scr[cb])))
        hs = slice(hh * w, (hh + 1) * w)
        y_ref[:, hs] = (_rms(out_t.T, ng_ref[:, hs]) * out_scale).astype(BF16)


def _diff_attn(scalars, proj, bias_tiles, norm_g, *, batch, seq, blk, heads, out_scale):
    nq = seq // blk
    kvw = heads * DA_V_DIM
    per_row = BRANCH_WIDTH // kvw
    return pl.pallas_call(
        functools.partial(_attn_kernel, blk=blk, heads=heads, out_scale=out_scale),
        out_shape=jax.ShapeDtypeStruct((batch * seq, DA_HEADS * DA_V_DIM), BF16),
        grid=(batch, DA_HEADS // heads, nq),
        in_specs=[pl.BlockSpec(memory_space=pltpu.SMEM),
                  pl.BlockSpec((blk, kvw), lambda b, h, i: (b * nq + i, COL_QD * per_row + h)),
                  pl.BlockSpec((seq, kvw), lambda b, h, i: (b, COL_KD * per_row + h)),
                  pl.BlockSpec((seq, kvw), lambda b, h, i: (b, COL_VD * per_row + h)),
                  pl.BlockSpec((heads, 2, blk, blk), lambda b, h, i: (h, 0, 0, 0)),
                  pl.BlockSpec((1, kvw), lambda b, h, i: (0, h))],
        out_specs=pl.BlockSpec((blk, kvw), lambda b, h, i: (b * nq + i, h)),
        scratch_shapes=[pltpu.VMEM((2 * heads, 1, blk), F32),
                        pltpu.VMEM((2 * heads, 1, blk), F32),
                        pltpu.VMEM((2 * heads, DA_V_DIM, blk), F32),
                        pltpu.VMEM((kvw, seq), BF16),
                        pltpu.VMEM((2 * heads, blk, blk), F32)],
        compiler_params=_cparams(("parallel", "parallel", "arbitrary")),
        name="diff_attn",
    )(scalars, proj, proj, proj, bias_tiles, norm_g)


def _merge_kernel(ya_ref, yb_ref, yc_ref, g0_ref, g1_ref, g2_ref, x_ref, wbr_ref,
                  wo_ref, ng_ref, o_ref):
    merged = None
    for n, (y_ref, g_ref) in enumerate(((ya_ref, g0_ref), (yb_ref, g1_ref),
                                        (yc_ref, g2_ref))):
        z = jnp.dot(y_ref[...], wbr_ref[n], preferred_element_type=F32)
        term = _sigmoid(g_ref[...].astype(F32)) * z
        merged = term if merged is None else merged + term
    mix = jnp.dot(merged.astype(BF16), wo_ref[...], preferred_element_type=F32)
    o_ref[...] = x_ref[...] + _rms(mix, ng_ref[...])


def _merge(y_a, y_b, y_c, proj, x2d, w_branch, w_out, norm_g, *, layer, tm):
    m = x2d.shape[0]
    row = lambda i: (i, 0)

    def gate_spec(n):
        return pl.BlockSpec((tm, D_MODEL), lambda i: (i, COL_GATE // 2 + n))

    return pl.pallas_call(
        _merge_kernel,
        out_shape=jax.ShapeDtypeStruct((m, D_MODEL), F32),
        grid=(m // tm,),
        in_specs=[pl.BlockSpec((tm, BRANCH_WIDTH), row),
                  pl.BlockSpec((tm, BRANCH_WIDTH), row),
                  pl.BlockSpec((tm, BRANCH_WIDTH), row),
                  gate_spec(0), gate_spec(1), gate_spec(2),
                  pl.BlockSpec((tm, D_MODEL), row),
                  pl.BlockSpec((None, N_BRANCH, BRANCH_WIDTH, D_MODEL),
                               lambda i: (layer, 0, 0, 0), pipeline_mode=pl.Buffered(1)),
                  pl.BlockSpec((None, D_MODEL, D_MODEL), lambda i: (layer, 0, 0),
                               pipeline_mode=pl.Buffered(1)),
                  pl.BlockSpec((1, D_MODEL), lambda i: (0, 0))],
        out_specs=pl.BlockSpec((tm, D_MODEL), row),
        compiler_params=_cparams(("parallel",)),
        name="merge",
    )(y_a, y_b, y_c, proj, proj, proj, x2d, w_branch, w_out, norm_g)


def _ffn_kernel(x_ref, g_ref, wa_ref, wv_ref, cwa_ref, cwv_ref, cba_ref,
                cbv_ref, wd_ref, ng_ref, o_ref, h_scr, tail_scr, *, tl):
    i = pl.program_id(1)
    f = pl.program_id(2)
    nf = pl.num_programs(2)
    hal = SUBLANES

    @pl.when(f == 0)
    def _():
        h_scr[...] = _rms(x_ref[0], g_ref[...]).astype(BF16)
        o_ref[0] = jnp.zeros((tl, D_MODEL), F32)

    @pl.when(i == 0)
    def _():
        tail_scr[f] = jnp.zeros(tail_scr.shape[1:], F32)

    hb = h_scr[...]

    def conv(part, w_ref, cw_ref, cb_ref):
        up = jnp.dot(hb, w_ref[...], preferred_element_type=F32)
        ext = jnp.concatenate([tail_scr[f, part], up], axis=0)
        tail_scr[f, part] = up[tl - hal:tl, :]
        out = cb_ref[...] + cw_ref[FFN_CONV - 1:FFN_CONV, :] * up
        for j in range(FFN_CONV - 1):
            off = hal - (FFN_CONV - 1) + j
            out = out + cw_ref[j:j + 1, :] * ext[off:off + tl, :]
        return out

    a = conv(0, wa_ref, cwa_ref, cba_ref)
    v = conv(1, wv_ref, cwv_ref, cbv_ref)
    act = (_gelu_tanh(a) * v).astype(BF16)
    o_ref[0] += jnp.dot(act, wd_ref[...], preferred_element_type=F32)

    @pl.when(f == nf - 1)
    def _():
        o_ref[0] = x_ref[0] + _rms(o_ref[0], ng_ref[...])


def _ffn(x3d, gain, w_up, conv_w, conv_b, w_down, norm_g, *, layer, tl, tf):
    batch, seq, _ = x3d.shape
    nfb = D_FF // tf
    hal = SUBLANES
    return pl.pallas_call(
        functools.partial(_ffn_kernel, tl=tl),
        out_shape=jax.ShapeDtypeStruct((batch, seq, D_MODEL), F32),
        grid=(batch, seq // tl, nfb),
        in_specs=[pl.BlockSpec((1, tl, D_MODEL), lambda b, i, f: (b, i, 0)),
                  pl.BlockSpec((1, D_MODEL), lambda b, i, f: (0, 0)),
                  pl.BlockSpec((None, D_MODEL, tf), lambda b, i, f: (layer, 0, f)),
                  pl.BlockSpec((None, D_MODEL, tf), lambda b, i, f: (layer, 0, nfb + f)),
                  pl.BlockSpec((FFN_CONV, tf), lambda b, i, f: (0, f)),
                  pl.BlockSpec((FFN_CONV, tf), lambda b, i, f: (0, nfb + f)),
                  pl.BlockSpec((1, tf), lambda b, i, f: (0, f)),
                  pl.BlockSpec((1, tf), lambda b, i, f: (0, nfb + f)),
                  pl.BlockSpec((None, tf, D_MODEL), lambda b, i, f: (layer, f, 0)),
                  pl.BlockSpec((1, D_MODEL), lambda b, i, f: (0, 0))],
        out_specs=pl.BlockSpec((1, tl, D_MODEL), lambda b, i, f: (b, i, 0)),
        scratch_shapes=[pltpu.VMEM((tl, D_MODEL), BF16),
                        pltpu.VMEM((nfb, 2, hal, tf), F32)],
        compiler_params=_cparams(("parallel", "arbitrary", "arbitrary")),
        name="conv_ffn",
    )(x3d, gain, w_up, w_up, conv_w, conv_w, conv_b, conv_b, w_down, norm_g)


def _t5_bucket(dist):
    n = jnp.maximum(dist, 0)
    max_exact = REL_BUCKETS // 2
    nf = jnp.maximum(n, 1).astype(F32)
    large = max_exact + (jnp.log(nf / max_exact) / math.log(REL_MAX_DIST / max_exact)
                         * (REL_BUCKETS - max_exact)).astype(jnp.int32)
    large = jnp.minimum(large, REL_BUCKETS - 1)
    return jnp.where(n < max_exact, n, large)


def _attn_bias_tiles(rel_bias, blk):
    qi = jnp.arange(blk, dtype=jnp.int32)[None, :]
    kj = jnp.arange(blk, dtype=jnp.int32)[:, None]
    table = rel_bias.astype(F32) * LOG2E
    last = table[REL_BUCKETS - 1][:, None, None]
    tiles = []
    for off in (0, blk):
        dist = qi - kj + off
        bucket = _t5_bucket(dist)
        bias = jnp.broadcast_to(last, (DA_HEADS, blk, blk))
        for b in range(REL_BUCKETS - 1):
            bias = jnp.where(bucket == b, table[b][:, None, None], bias)
        tiles.append(jnp.where(dist >= 0, bias, NEG_BIG))
    return jnp.stack(tiles, axis=1)


def _s5_params(lam_re, lam_im, log_dt, b_re, b_im, c_re, c_im):
    dt = jnp.exp(log_dt)[:, None]
    mag = jnp.exp(lam_re * dt)
    a_re = mag * jnp.cos(lam_im * dt)
    a_im = mag * jnp.sin(lam_im * dt)
    den = lam_re * lam_re + lam_im * lam_im
    z_re = ((a_re - 1.0) * lam_re + a_im * lam_im) / den
    z_im = (a_im * lam_re - (a_re - 1.0) * lam_im) / den
    bb_re = z_re[..., None] * b_re - z_im[..., None] * b_im
    bb_im = z_re[..., None] * b_im + z_im[..., None] * b_re
    gs = S5_GROUPS // S5_SLABS
    eye = jnp.eye(gs, dtype=F32)

    def in_blocks(bb):
        bb = bb.reshape(S5_SLABS, gs, S5_STATE, S5_GROUP)
        w = jnp.einsum('sgpc,gh->sgchp', bb, eye)
        return w.reshape(S5_SLABS, gs * S5_GROUP, gs * S5_STATE)

    def out_blocks(cc):
        cc = cc.reshape(S5_SLABS, gs, S5_GROUP, S5_STATE)
        w = jnp.einsum('sgcp,gh->sgphc', cc, eye)
        return w.reshape(S5_SLABS, gs * S5_STATE, gs * S5_GROUP)

    wb = jnp.concatenate([in_blocks(bb_re), in_blocks(bb_im)], axis=-1).astype(BF16)
    wc = jnp.concatenate([out_blocks(c_re), out_blocks(-c_im)], axis=-2).astype(BF16)
    a_re = a_re.reshape(S5_SLABS, 1, S5_SLAB_STATES)
    a_im = a_im.reshape(S5_SLABS, 1, S5_SLAB_STATES)
    return wb, wc, a_re, a_im


IN_PROJ_TM, IN_PROJ_TN = 1024, 2048
MLSTM_CHUNK = 128
ATTN_BLOCK = 512
ATTN_HEADS_PER_STEP = 2
S5_STEPS = 32
MERGE_TM = 256
FFN_TL, FFN_TF = 512, 512


def _layer(x2d, batch, seq, layer, p):
    n_if = 2 * M_HEADS
    proj, gates = _in_proj(x2d, p['norm_mix_pre'][layer][None, :], p['w_gate'],
                           p['w_main'], layer=layer, tm=min(IN_PROJ_TM, batch * seq),
                           tn=IN_PROJ_TN)

    gate_bias = jnp.pad(p['mlstm_b_if'][layer].reshape(1, n_if),
                        ((0, 0), (0, N_GATE_PAD - n_if)))
    wb, wc, a_re, a_im = _s5_params(
        p['s5_lambda_re'][layer], p['s5_lambda_im'][layer], p['s5_log_dt'][layer],
        p['s5_b_re'][layer], p['s5_b_im'][layer], p['s5_c_re'][layer], p['s5_c_im'][layer])
    y_a, y_c = _mlstm_s5(proj, gates, gate_bias, p['mlstm_conv'][layer],
                         p['mlstm_norm'][layer][None, :], wb, wc, a_re, a_im,
                         p['s5_d'][layer][None, :], p['w_glu'], layer=layer, batch=batch,
                         seq=seq, steps=min(S5_STEPS, seq), chunk=MLSTM_CHUNK)
    y_c = y_c.reshape(batch * seq, BRANCH_WIDTH)

    lambda_init = 0.8 - 0.6 * math.exp(-0.3 * layer)
    lam = p['diff_lambda'][layer]
    lam_full = (jnp.exp(jnp.sum(lam[0] * lam[1])) - jnp.exp(jnp.sum(lam[2] * lam[3]))
                + lambda_init)
    blk = min(ATTN_BLOCK, seq)
    scalars = jnp.concatenate(
        [lam_full[None], p['rel_bias'][REL_BUCKETS - 1, :] * LOG2E]).astype(F32)
    y_b = _diff_attn(scalars, proj, p['attn_bias_tiles'],
                     p['diff_norm'][layer][None, :], batch=batch, seq=seq, blk=blk,
                     heads=ATTN_HEADS_PER_STEP, out_scale=1.0 - lambda_init)

    x2d = _merge(y_a, y_b, y_c, proj, x2d, p['w_branch'], p['w_out'],
                 p['norm_mix_post'][layer][None, :], layer=layer, tm=MERGE_TM)

    x3d = _ffn(x2d.reshape(batch, seq, D_MODEL), p['norm_ffn_pre'][layer][None, :],
               p['w_up'], p['ffn_conv'][layer], p['ffn_conv_b'][layer][None, :],
               p['w_down'], p['norm_ffn_post'][layer][None, :], layer=layer,
               tl=min(FFN_TL, seq), tf=FFN_TF)
    return x3d.reshape(batch * seq, D_MODEL)


def kernel(x, norm_mix_pre, norm_mix_post, norm_ffn_pre, norm_ffn_post, w_in, mlstm_b_if, mlstm_conv, mlstm_norm, diff_lambda, diff_norm, rel_bias, s5_lambda_re, s5_lambda_im, s5_log_dt, s5_b_re, s5_b_im, s5_c_re, s5_c_im, s5_d, s5_w_glu, w_branch, w_out, w_up, ffn_conv, ffn_conv_b, w_down):
    n_if = 2 * M_HEADS
    split = 4 * M_HEADS * M_HEAD_DIM
    w_main = jnp.concatenate([w_in[:, :, :split], w_in[:, :, split + n_if:]],
                             axis=2).astype(BF16)
    w_gate = jnp.pad(w_in[:, :, split:split + n_if],
                     ((0, 0), (0, 0), (0, N_GATE_PAD - n_if))).astype(BF16)
    p = dict(norm_mix_pre=norm_mix_pre, norm_mix_post=norm_mix_post,
             norm_ffn_pre=norm_ffn_pre, norm_ffn_post=norm_ffn_post,
             w_main=w_main, w_gate=w_gate,
             mlstm_b_if=mlstm_b_if, mlstm_conv=mlstm_conv, mlstm_norm=mlstm_norm,
             diff_lambda=diff_lambda, diff_norm=diff_norm, rel_bias=rel_bias,
             s5_lambda_re=s5_lambda_re, s5_lambda_im=s5_lambda_im, s5_log_dt=s5_log_dt,
             s5_b_re=s5_b_re, s5_b_im=s5_b_im, s5_c_re=s5_c_re, s5_c_im=s5_c_im,
             s5_d=s5_d, w_glu=s5_w_glu.astype(BF16), w_branch=w_branch.astype(BF16),
             w_out=w_out.astype(BF16), w_up=w_up.astype(BF16),
             ffn_conv=ffn_conv, ffn_conv_b=ffn_conv_b, w_down=w_down.astype(BF16))
    batch, seq, _ = x.shape
    p['attn_bias_tiles'] = _attn_bias_tiles(rel_bias, min(ATTN_BLOCK, seq))
    x2d = x.reshape(batch * seq, D_MODEL)
    for layer in range(DEPTH):
        x2d = _layer(x2d, batch, seq, layer, p)
    return x2d.reshape(batch, seq, D_MODEL)
```

```python
import functools
import math

import jax
import jax.numpy as jnp
from jax import lax
from jax.experimental import pallas as pl
from jax.experimental.pallas import tpu as pltpu

F32 = jnp.float32
BF16 = jnp.bfloat16
HIGHEST = lax.Precision.HIGHEST

D_MODEL = 2048
DEPTH = 2
BRANCH_WIDTH = 1024
N_BRANCH = 3
M_HEADS = 4
M_HEAD_DIM = 256
M_CONV = 4
DA_HEADS = 4
DA_HEAD_DIM = 128
DA_V_DIM = 256
REL_BUCKETS = 32
REL_MAX_DIST = 128
S5_GROUP = 16
S5_GROUPS = 64
S5_STATE = 64
D_FF = 5632
FFN_CONV = 3
EPS = 1e-6

LANES = 128
SUBLANES = 8
VMEM_LIMIT = 56 * 1024 * 1024

N_MAIN = 14336
COL_QM, COL_KM, COL_VM, COL_OM = 0, 1, 2, 3
COL_QD, COL_KD, COL_VD, COL_US = 4, 5, 6, 7
COL_GATE = 8
N_GATE_PAD = LANES

S5_SLABS = 8
S5_SLAB_STATES = 512

NEG_BIG = -1e30
LOG2E = math.log2(math.e)


def _cparams(sem):
    return pltpu.CompilerParams(dimension_semantics=sem, vmem_limit_bytes=VMEM_LIMIT)


def _sigmoid(x):
    return 0.5 * jnp.tanh(0.5 * x) + 0.5


def _gelu_tanh(x):
    c = math.sqrt(2.0 / math.pi)
    return 0.5 * x * (1.0 + jnp.tanh(c * (x + 0.044715 * (x * x * x))))


def _rms(x, gain):
    var = jnp.mean(x * x, axis=-1, keepdims=True)
    return x * lax.rsqrt(var + EPS) * gain


def _in_proj_kernel(x_ref, g_ref, wg_ref, w_ref, o_ref, og_ref, h_scr):
    @pl.when(pl.program_id(1) == 0)
    def _():
        h_scr[...] = _rms(x_ref[...], g_ref[...]).astype(BF16)
        og_ref[...] = jnp.dot(h_scr[...], wg_ref[...], preferred_element_type=F32)

    o_ref[...] = jnp.dot(h_scr[...], w_ref[...],
                         preferred_element_type=F32).astype(BF16)


def _in_proj(x2d, gain, w_gate, w_main, *, layer, tm, tn):
    m = x2d.shape[0]
    return pl.pallas_call(
        _in_proj_kernel,
        out_shape=(jax.ShapeDtypeStruct((m, N_MAIN), BF16),
                   jax.ShapeDtypeStruct((m, N_GATE_PAD), F32)),
        grid=(m // tm, N_MAIN // tn),
        in_specs=[pl.BlockSpec((tm, D_MODEL), lambda i, n: (i, 0)),
                  pl.BlockSpec((1, D_MODEL), lambda i, n: (0, 0)),
                  pl.BlockSpec((None, D_MODEL, N_GATE_PAD), lambda i, n: (layer, 0, 0)),
                  pl.BlockSpec((None, D_MODEL, tn), lambda i, n: (layer, 0, n))],
        out_specs=(pl.BlockSpec((tm, tn), lambda i, n: (i, n)),
                   pl.BlockSpec((tm, N_GATE_PAD), lambda i, n: (i, 0))),
        scratch_shapes=[pltpu.VMEM((tm, D_MODEL), BF16)],
        compiler_params=_cparams(("parallel", "arbitrary")),
        name="in_proj",
    )(x2d, gain, w_gate, w_main)


def _mlstm_chunk(r0, q_ref, k_ref, v_ref, o_ref, gt_ref, gb_ref, cw_ref, ng_ref, y_ref,
                 c_scr, n_scr, m_scr, qe_scr, ke_scr, *, t):
    hd = M_HEAD_DIM
    width = M_HEADS * hd
    rows = slice(r0, r0 + t)

    qe_scr[SUBLANES:SUBLANES + t, :] = q_ref[rows, :].astype(F32)
    ke_scr[SUBLANES:SUBLANES + t, :] = k_ref[rows, :].astype(F32)

    gates = gt_ref[rows, :] + gb_ref[...]
    log_f = jnp.minimum(gates, 0.0) - jnp.log1p(jnp.exp(-jnp.abs(gates)))
    row = lax.broadcasted_iota(jnp.int32, (t, t), 0)
    col = lax.broadcasted_iota(jnp.int32, (t, t), 1)
    causal = col <= row
    cum = jnp.dot(causal.astype(F32), log_f, preferred_element_type=F32,
                  precision=HIGHEST)
    gates_t = gates.T
    cum_t = cum.T

    for h in range(M_HEADS):
        sl = slice(h * hd, (h + 1) * hd)
        ksl = slice(width + h * hd, width + (h + 1) * hd)
        qc = jnp.zeros((t, hd), F32)
        kc = jnp.zeros((t, hd), F32)
        for j in range(M_CONV):
            off = SUBLANES - (M_CONV - 1) + j
            qc = qc + cw_ref[j:j + 1, sl] * qe_scr[off:off + t, sl]
            kc = kc + cw_ref[j:j + 1, ksl] * ke_scr[off:off + t, sl]
        qc = qc * _sigmoid(qc)
        kc = kc * _sigmoid(kc) * (hd ** -0.5)
        qb = qc.astype(BF16)
        kb = kc.astype(BF16)
        vb = v_ref[rows, sl]

        li_row = gates_t[h:h + 1, :]
        b_row = cum_t[M_HEADS + h:M_HEADS + h + 1, :]
        li_col = gates[:, h:h + 1]
        b_col = cum[:, M_HEADS + h:M_HEADS + h + 1]
        m_prev = m_scr[h:h + 1, 0:1]
        c_prev = c_scr[h]
        n_prev = n_scr[h:h + 1, :]

        dmat = jnp.where(causal, b_col - b_row + li_row, -jnp.inf)
        inter = b_col + m_prev
        m_t = jnp.maximum(inter, jnp.max(dmat, axis=-1, keepdims=True))
        s = lax.dot_general(qb, kb, (((1,), (1,)), ((), ())),
                            preferred_element_type=F32) * jnp.exp(dmat - m_t)
        w_inter = jnp.exp(inter - m_t)
        num = (jnp.dot(s.astype(BF16), vb, preferred_element_type=F32)
               + w_inter * jnp.dot(qb, c_prev.astype(BF16), preferred_element_type=F32))
        den = (jnp.sum(s, axis=-1, keepdims=True)
               + w_inter * jnp.sum(qc * n_prev, axis=-1, keepdims=True))
        hh = num / jnp.maximum(jnp.abs(den), jnp.exp(-m_t))

        g = cum[t - 1:t, M_HEADS + h:M_HEADS + h + 1]
        a_col = g - b_col + li_col
        m_new = jnp.maximum(g + m_prev, jnp.max(a_col, axis=0, keepdims=True))
        ws = jnp.exp(a_col - m_new)
        decay = jnp.exp(g + m_prev - m_new)
        kw = ws * kc
        c_scr[h] = decay * c_prev + lax.dot_general(
            kw.astype(BF16), vb, (((0,), (0,)), ((), ())), preferred_element_type=F32)
        n_scr[h:h + 1, :] = decay * n_prev + jnp.sum(kw, axis=0, keepdims=True)
        m_scr[h:h + 1, :] = jnp.broadcast_to(m_new, (1, LANES))

        hn = _rms(hh, ng_ref[:, sl])
        y_ref[rows, sl] = (_sigmoid(o_ref[rows, sl].astype(F32)) * hn).astype(BF16)
        yield

    qe_scr[0:SUBLANES, :] = qe_scr[t:t + SUBLANES, :]
    ke_scr[0:SUBLANES, :] = ke_scr[t:t + SUBLANES, :]


def _mix_kernel(q_ref, k_ref, v_ref, o_ref, gt_ref, gb_ref, cw_ref, ng_ref,
                u_ref, wb_ref, wc_ref, are_ref, aim_ref, d_ref, wg_ref,
                ya_ref, yc_ref,
                c_scr, n_scr, m_scr, qe_scr, ke_scr,
                xre_scr, xim_scr, bu_scr, uil_scr, yil_scr,
                *, chunk, steps, batch, steps_per_seq):
    i = pl.program_id(0)
    ns = S5_SLAB_STATES
    rows = steps * batch
    n_chunks = rows // chunk
    width = M_HEADS * M_HEAD_DIM

    @pl.when(i == 0)
    def _():
        xre_scr[...] = jnp.zeros_like(xre_scr)
        xim_scr[...] = jnp.zeros_like(xim_scr)

    @pl.when(i % steps_per_seq == 0)
    def _():
        c_scr[...] = jnp.zeros_like(c_scr)
        n_scr[...] = jnp.zeros_like(n_scr)
        m_scr[...] = jnp.zeros_like(m_scr)
        qe_scr[0:SUBLANES, :] = jnp.zeros((SUBLANES, width), F32)
        ke_scr[0:SUBLANES, :] = jnp.zeros((SUBLANES, width), F32)

    def mlstm_heads():
        for j in range(n_chunks):
            yield from _mlstm_chunk(j * chunk, q_ref, k_ref, v_ref, o_ref, gt_ref, gb_ref,
                                    cw_ref, ng_ref, ya_ref, c_scr, n_scr, m_scr, qe_scr,
                                    ke_scr, t=chunk)

    heads = mlstm_heads()
    heads_per_slab = (n_chunks * M_HEADS) // S5_SLABS

    for b in range(batch):
        ub = u_ref[b].astype(F32)
        for c in range(S5_SLABS):
            uil_scr.at[c][pl.ds(b, steps, stride=batch), :] = ub[:, c * LANES:(c + 1) * LANES]

    ys = []
    for s in range(S5_SLABS):
        us = uil_scr[s]
        buf = bu_scr.at[s % 2]
        buf[...] = jnp.dot(us.astype(BF16), wb_ref[s], preferred_element_type=F32)
        a_re = jnp.broadcast_to(are_ref[s], (batch, ns))
        a_im = jnp.broadcast_to(aim_ref[s], (batch, ns))

        x_re = xre_scr[s]
        x_im = xim_scr[s]
        for tt in range(steps):
            rs = slice(tt * batch, (tt + 1) * batch)
            n_re = a_re * x_re - a_im * x_im + buf[rs, 0:ns]
            n_im = a_re * x_im + a_im * x_re + buf[rs, ns:2 * ns]
            buf[rs, 0:ns] = n_re
            buf[rs, ns:2 * ns] = n_im
            x_re, x_im = n_re, n_im
        xre_scr[s] = x_re
        xim_scr[s] = x_im
        y = (jnp.dot(buf[...].astype(BF16), wc_ref[s], preferred_element_type=F32)
             + d_ref[:, s * LANES:(s + 1) * LANES] * us)
        ys.append(_gelu_tanh(y).astype(BF16))

        for _ in range(heads_per_slab):
            next(heads, None)

    for _ in heads:
        pass

    yb = jnp.concatenate(ys, axis=1)
    half = BRANCH_WIDTH
    a = jnp.dot(yb, wg_ref[:, 0:half], preferred_element_type=F32)
    g = jnp.dot(yb, wg_ref[:, half:2 * half], preferred_element_type=F32)
    out = a * _sigmoid(g)
    for c in range(S5_SLABS):
        yil_scr[c] = out[:, c * LANES:(c + 1) * LANES]
    for b in range(batch):
        for c in range(S5_SLABS):
            yc_ref[b, :, c * LANES:(c + 1) * LANES] = (
                yil_scr.at[c][pl.ds(b, steps, stride=batch), :].astype(BF16))


def _mlstm_s5(proj, gates, gate_bias, conv_w, norm_g, wb, wc, a_re, a_im, d_skip, w_glu, *,
              layer, batch, seq, steps, chunk):
    rows = steps * batch
    chunk = min(chunk, rows)
    assert seq % rows == 0 and rows % chunk == 0
    ns = S5_SLAB_STATES
    width = M_HEADS * M_HEAD_DIM
    const2 = lambda i: (0, 0)
    const3 = lambda i: (0, 0, 0)

    def col_spec(cb):
        return pl.BlockSpec((rows, width), lambda i: (i, cb))

    return pl.pallas_call(
        functools.partial(_mix_kernel, chunk=chunk, steps=steps, batch=batch,
                          steps_per_seq=seq // rows),
        out_shape=(jax.ShapeDtypeStruct((batch * seq, width), BF16),
                   jax.ShapeDtypeStruct((batch, seq, BRANCH_WIDTH), BF16)),
        grid=(seq // steps,),
        in_specs=[col_spec(COL_QM), col_spec(COL_KM), col_spec(COL_VM), col_spec(COL_OM),
                  pl.BlockSpec((rows, N_GATE_PAD), lambda i: (i, 0)),
                  pl.BlockSpec((1, N_GATE_PAD), const2),
                  pl.BlockSpec((M_CONV, 2 * width), const2),
                  pl.BlockSpec((1, width), const2),
                  pl.BlockSpec((batch, steps, BRANCH_WIDTH), lambda i: (0, i, COL_US)),
                  pl.BlockSpec((S5_SLABS, LANES, 2 * ns), const3),
                  pl.BlockSpec((S5_SLABS, 2 * ns, LANES), const3),
                  pl.BlockSpec((S5_SLABS, 1, ns), const3),
                  pl.BlockSpec((S5_SLABS, 1, ns), const3),
                  pl.BlockSpec((1, BRANCH_WIDTH), const2),
                  pl.BlockSpec((None, BRANCH_WIDTH, 2 * BRANCH_WIDTH),
                               lambda i: (layer, 0, 0))],
        out_specs=(pl.BlockSpec((rows, width), lambda i: (i, 0)),
                   pl.BlockSpec((batch, steps, BRANCH_WIDTH), lambda i: (0, i, 0))),
        scratch_shapes=[pltpu.VMEM((M_HEADS, M_HEAD_DIM, M_HEAD_DIM), F32),
                        pltpu.VMEM((M_HEADS, M_HEAD_DIM), F32),
                        pltpu.VMEM((M_HEADS, LANES), F32),
                        pltpu.VMEM((chunk + SUBLANES, width), F32),
                        pltpu.VMEM((chunk + SUBLANES, width), F32),
                        pltpu.VMEM((S5_SLABS, batch, ns), F32),
                        pltpu.VMEM((S5_SLABS, batch, ns), F32),
                        pltpu.VMEM((2, rows, 2 * ns), F32),
                        pltpu.VMEM((S5_SLABS, rows, LANES), F32),
                        pltpu.VMEM((S5_SLABS, rows, LANES), F32)],
        compiler_params=_cparams(("arbitrary",)),
        name="mlstm_s5",
    )(proj, proj, proj, proj, gates, gate_bias, conv_w, norm_g,
      proj.reshape(batch, seq, N_MAIN), wb, wc, a_re, a_im, d_skip, w_glu)


def _attn_kernel(sc_ref, q_ref, k_ref, v_ref, bias_ref, ng_ref, y_ref,
                 m_scr, l_scr, acc_scr, vt_scr, s_scr, *, blk, heads, out_scale):
    t = blk
    d = DA_HEAD_DIM
    w = DA_V_DIM
    h0 = pl.program_id(1) * heads
    i = pl.program_id(2)
    lam = sc_ref[0]

    qs = (q_ref[...].astype(F32) * (d ** -0.5 * LOG2E)).astype(BF16)

    @pl.when(i == 0)
    def _():
        vt_scr[...] = v_ref[...].astype(F32).T.astype(BF16)

    def block_step(j, bias_tile=None, first=False):
        start = pl.multiple_of(j * t, t)
        kb = k_ref[pl.ds(start, t), :]
        vtb = vt_scr[:, pl.ds(start, t)]
        far = bias_tile is None
        for ci in range(2 * heads):
            cols = slice(ci * d, (ci + 1) * d)
            s_scr[ci] = lax.dot_general(kb[:, cols], qs[:, cols], (((1,), (1,)), ((), ())),
                                        preferred_element_type=F32)
        for hh in range(heads):
            bias = sc_ref[1 + h0 + hh] if far else bias_ref[hh, bias_tile]
            for c in range(2):
                ci = 2 * hh + c
                s = s_scr[ci]
                if far:
                    m_blk = jnp.max(s, axis=0, keepdims=True) + bias
                else:
                    s = s + bias
                    m_blk = jnp.max(s, axis=0, keepdims=True)
                m_new = m_blk if first else jnp.maximum(m_scr[ci], m_blk)
                p = jnp.exp2(s - ((m_new - bias) if far else m_new))
                l_blk = jnp.sum(p, axis=0, keepdims=True)
                pv = jnp.dot(vtb[hh * w:(hh + 1) * w, :], p.astype(BF16),
                             preferred_element_type=F32)
                if first:
                    l_scr[ci] = l_blk
                    acc_scr[ci] = pv
                else:
                    alpha = jnp.exp2(m_scr[ci] - m_new)
                    l_scr[ci] = alpha * l_scr[ci] + l_blk
                    acc_scr[ci] = alpha * acc_scr[ci] + pv
                m_scr[ci] = m_new

    block_step(i, 0, first=True)

    @pl.when(i >= 1)
    def _():
        block_step(i - 1, 1)

    def far_body(j, carry):
        block_step(j)
        return carry

    lax.fori_loop(0, jnp.maximum(i - 1, 0), far_body, 0)

    for hh in range(heads):
        ca, cb = 2 * hh, 2 * hh + 1
        out_t = (acc_scr[ca] * (1.0 / l_scr[ca])
                 - lam * (acc_scr[cb] * (1.0 / l_scr[cb])))
        hs = slice(hh * w, (hh + 1) * w)
        y_ref[:, hs] = (_rms(out_t.T, ng_ref[:, hs]) * out_scale).astype(BF16)


def _diff_attn(scalars, proj, bias_tiles, norm_g, *, batch, seq, blk, heads, out_scale):
    nq = seq // blk
    kvw = heads * DA_V_DIM
    per_row = BRANCH_WIDTH // kvw
    return pl.pallas_call(
        functools.partial(_attn_kernel, blk=blk, heads=heads, out_scale=out_scale),
        out_shape=jax.ShapeDtypeStruct((batch * seq, DA_HEADS * DA_V_DIM), BF16),
        grid=(batch, DA_HEADS // heads, nq),
        in_specs=[pl.BlockSpec(memory_space=pltpu.SMEM),
                  pl.BlockSpec((blk, kvw), lambda b, h, i: (b * nq + i, COL_QD * per_row + h)),
                  pl.BlockSpec((seq, kvw), lambda b, h, i: (b, COL_KD * per_row + h)),
                  pl.BlockSpec((seq, kvw), lambda b, h, i: (b, COL_VD * per_row + h)),
                  pl.BlockSpec((heads, 2, blk, blk), lambda b, h, i: (h, 0, 0, 0)),
                  pl.BlockSpec((1, kvw), lambda b, h, i: (0, h))],
        out_specs=pl.BlockSpec((blk, kvw), lambda b, h, i: (b * nq + i, h)),
        scratch_shapes=[pltpu.VMEM((2 * heads, 1, blk), F32),
                        pltpu.VMEM((2 * heads, 1, blk), F32),
                        pltpu.VMEM((2 * heads, DA_V_DIM, blk), F32),
                        pltpu.VMEM((kvw, seq), BF16),
                        pltpu.VMEM((2 * heads, blk, blk), F32)],
        compiler_params=_cparams(("parallel", "parallel", "arbitrary")),
        name="diff_attn",
    )(scalars, proj, proj, proj, bias_tiles, norm_g)


def _merge_kernel(ya_ref, yb_ref, yc_ref, g0_ref, g1_ref, g2_ref, x_ref, wbr_ref,
                  wo_ref, ng_ref, o_ref):
    merged = None
    for n, (y_ref, g_ref) in enumerate(((ya_ref, g0_ref), (yb_ref, g1_ref),
                                        (yc_ref, g2_ref))):
        z = jnp.dot(y_ref[...], wbr_ref[n], preferred_element_type=F32)
        term = _sigmoid(g_ref[...].astype(F32)) * z
        merged = term if merged is None else merged + term
    mix = jnp.dot(merged.astype(BF16), wo_ref[...], preferred_element_type=F32)
    o_ref[...] = x_ref[...] + _rms(mix, ng_ref[...])


def _merge(y_a, y_b, y_c, proj, x2d, w_branch, w_out, norm_g, *, layer, tm):
    m = x2d.shape[0]
    row = lambda i: (i, 0)

    def gate_spec(n):
        return pl.BlockSpec((tm, D_MODEL), lambda i: (i, COL_GATE // 2 + n))

    return pl.pallas_call(
        _merge_kernel,
        out_shape=jax.ShapeDtypeStruct((m, D_MODEL), F32),
        grid=(m // tm,),
        in_specs=[pl.BlockSpec((tm, BRANCH_WIDTH), row),
                  pl.BlockSpec((tm, BRANCH_WIDTH), row),
                  pl.BlockSpec((tm, BRANCH_WIDTH), row),
                  gate_spec(0), gate_spec(1), gate_spec(2),
                  pl.BlockSpec((tm, D_MODEL), row),
                  pl.BlockSpec((None, N_BRANCH, BRANCH_WIDTH, D_MODEL),
                               lambda i: (layer, 0, 0, 0), pipeline_mode=pl.Buffered(1)),
                  pl.BlockSpec((None, D_MODEL, D_MODEL), lambda i: (layer, 0, 0),
                               pipeline_mode=pl.Buffered(1)),
                  pl.BlockSpec((1, D_MODEL), lambda i: (0, 0))],
        out_specs=pl.BlockSpec((tm, D_MODEL), row),
        compiler_params=_cparams(("parallel",)),
        name="merge",
    )(y_a, y_b, y_c, proj, proj, proj, x2d, w_branch, w_out, norm_g)


def _ffn_kernel(x_ref, g_ref, wa_ref, wv_ref, cwa_ref, cwv_ref, cba_ref,
                cbv_ref, wd_ref, ng_ref, o_ref, h_scr, tail_scr, *, tl):
    i = pl.program_id(1)
    f = pl.program_id(2)
    nf = pl.num_programs(2)
    hal = SUBLANES

    @pl.when(f == 0)
    def _():
        h_scr[...] = _rms(x_ref[0], g_ref[...]).astype(BF16)
        o_ref[0] = jnp.zeros((tl, D_MODEL), F32)

    @pl.when(i == 0)
    def _():
        tail_scr[f] = jnp.zeros(tail_scr.shape[1:], F32)

    hb = h_scr[...]

    def conv(part, w_ref, cw_ref, cb_ref):
        up = jnp.dot(hb, w_ref[...], preferred_element_type=F32)
        ext = jnp.concatenate([tail_scr[f, part], up], axis=0)
        tail_scr[f, part] = up[tl - hal:tl, :]
        out = cb_ref[...] + cw_ref[FFN_CONV - 1:FFN_CONV, :] * up
        for j in range(FFN_CONV - 1):
            off = hal - (FFN_CONV - 1) + j
            out = out + cw_ref[j:j + 1, :] * ext[off:off + tl, :]
        return out

    a = conv(0, wa_ref, cwa_ref, cba_ref)
    v = conv(1, wv_ref, cwv_ref, cbv_ref)
    act = (_gelu_tanh(a) * v).astype(BF16)
    o_ref[0] += jnp.dot(act, wd_ref[...], preferred_element_type=F32)

    @pl.when(f == nf - 1)
    def _():
        o_ref[0] = x_ref[0] + _rms(o_ref[0], ng_ref[...])


def _ffn(x3d, gain, w_up, conv_w, conv_b, w_down, norm_g, *, layer, tl, tf):
    batch, seq, _ = x3d.shape
    nfb = D_FF // tf
    hal = SUBLANES
    return pl.pallas_call(
        functools.partial(_ffn_kernel, tl=tl),
        out_shape=jax.ShapeDtypeStruct((batch, seq, D_MODEL), F32),
        grid=(batch, seq // tl, nfb),
        in_specs=[pl.BlockSpec((1, tl, D_MODEL), lambda b, i, f: (b, i, 0)),
                  pl.BlockSpec((1, D_MODEL), lambda b, i, f: (0, 0)),
                  pl.BlockSpec((None, D_MODEL, tf), lambda b, i, f: (layer, 0, f)),
                  pl.BlockSpec((None, D_MODEL, tf), lambda b, i, f: (layer, 0, nfb + f)),
                  pl.BlockSpec((FFN_CONV, tf), lambda b, i, f: (0, f)),
                  pl.BlockSpec((FFN_CONV, tf), lambda b, i, f: (0, nfb + f)),
                  pl.BlockSpec((1, tf), lambda b, i, f: (0, f)),
                  pl.BlockSpec((1, tf), lambda b, i, f: (0, nfb + f)),
                  pl.BlockSpec((None, tf, D_MODEL), lambda b, i, f: (layer, f, 0)),
                  pl.BlockSpec((1, D_MODEL), lambda b, i, f: (0, 0))],
        out_specs=pl.BlockSpec((1, tl, D_MODEL), lambda b, i, f: (b, i, 0)),
        scratch_shapes=[pltpu.VMEM((tl, D_MODEL), BF16),
                        pltpu.VMEM((nfb, 2, hal, tf), F32)],
        compiler_params=_cparams(("parallel", "arbitrary", "arbitrary")),
        name="conv_ffn",
    )(x3d, gain, w_up, w_up, conv_w, conv_w, conv_b, conv_b, w_down, norm_g)


def _t5_bucket(dist):
    n = jnp.maximum(dist, 0)
    max_exact = REL_BUCKETS // 2
    nf = jnp.maximum(n, 1).astype(F32)
    large = max_exact + (jnp.log(nf / max_exact) / math.log(REL_MAX_DIST / max_exact)
                         * (REL_BUCKETS - max_exact)).astype(jnp.int32)
    large = jnp.minimum(large, REL_BUCKETS - 1)
    return jnp.where(n < max_exact, n, large)


def _attn_bias_tiles(rel_bias, blk):
    qi = jnp.arange(blk, dtype=jnp.int32)[None, :]
    kj = jnp.arange(blk, dtype=jnp.int32)[:, None]
    table = rel_bias.astype(F32) * LOG2E
    last = table[REL_BUCKETS - 1][:, None, None]
    tiles = []
    for off in (0, blk):
        dist = qi - kj + off
        bucket = _t5_bucket(dist)
        bias = jnp.broadcast_to(last, (DA_HEADS, blk, blk))
        for b in range(REL_BUCKETS - 1):
            bias = jnp.where(bucket == b, table[b][:, None, None], bias)
        tiles.append(jnp.where(dist >= 0, bias, NEG_BIG))
    return jnp.stack(tiles, axis=1)


def _s5_params(lam_re, lam_im, log_dt, b_re, b_im, c_re, c_im):
    dt = jnp.exp(log_dt)[:, None]
    mag = jnp.exp(lam_re * dt)
    a_re = mag * jnp.cos(lam_im * dt)
    a_im = mag * jnp.sin(lam_im * dt)
    den = lam_re * lam_re + lam_im * lam_im
    z_re = ((a_re - 1.0) * lam_re + a_im * lam_im) / den
    z_im = (a_im * lam_re - (a_re - 1.0) * lam_im) / den
    bb_re = z_re[..., None] * b_re - z_im[..., None] * b_im
    bb_im = z_re[..., None] * b_im + z_im[..., None] * b_re
    gs = S5_GROUPS // S5_SLABS
    eye = jnp.eye(gs, dtype=F32)

    def in_blocks(bb):
        bb = bb.reshape(S5_SLABS, gs, S5_STATE, S5_GROUP)
        w = jnp.einsum('sgpc,gh->sgchp', bb, eye)
        return w.reshape(S5_SLABS, gs * S5_GROUP, gs * S5_STATE)

    def out_blocks(cc):
        cc = cc.reshape(S5_SLABS, gs, S5_GROUP, S5_STATE)
        w = jnp.einsum('sgcp,gh->sgphc', cc, eye)
        return w.reshape(S5_SLABS, gs * S5_STATE, gs * S5_GROUP)

    wb = jnp.concatenate([in_blocks(bb_re), in_blocks(bb_im)], axis=-1).astype(BF16)
    wc = jnp.concatenate([out_blocks(c_re), out_blocks(-c_im)], axis=-2).astype(BF16)
    a_re = a_re.reshape(S5_SLABS, 1, S5_SLAB_STATES)
    a_im = a_im.reshape(S5_SLABS, 1, S5_SLAB_STATES)
    return wb, wc, a_re, a_im


IN_PROJ_TM, IN_PROJ_TN = 1024, 2048
MLSTM_CHUNK = 128
ATTN_BLOCK = 512
ATTN_HEADS_PER_STEP = 2
S5_STEPS = 32
MERGE_TM = 256
FFN_TL, FFN_TF = 512, 512


def _layer(x2d, batch, seq, layer, p):
    n_if = 2 * M_HEADS
    proj, gates = _in_proj(x2d, p['norm_mix_pre'][layer][None, :], p['w_gate'],
                           p['w_main'], layer=layer, tm=min(IN_PROJ_TM, batch * seq),
                           tn=IN_PROJ_TN)

    gate_bias = jnp.pad(p['mlstm_b_if'][layer].reshape(1, n_if),
                        ((0, 0), (0, N_GATE_PAD - n_if)))
    wb, wc, a_re, a_im = _s5_params(
        p['s5_lambda_re'][layer], p['s5_lambda_im'][layer], p['s5_log_dt'][layer],
        p['s5_b_re'][layer], p['s5_b_im'][layer], p['s5_c_re'][layer], p['s5_c_im'][layer])
    y_a, y_c = _mlstm_s5(proj, gates, gate_bias, p['mlstm_conv'][layer],
                         p['mlstm_norm'][layer][None, :], wb, wc, a_re, a_im,
                         p['s5_d'][layer][None, :], p['w_glu'], layer=layer, batch=batch,
                         seq=seq, steps=min(S5_STEPS, seq), chunk=MLSTM_CHUNK)
    y_c = y_c.reshape(batch * seq, BRANCH_WIDTH)

    lambda_init = 0.8 - 0.6 * math.exp(-0.3 * layer)
    lam = p['diff_lambda'][layer]
    lam_full = (jnp.exp(jnp.sum(lam[0] * lam[1])) - jnp.exp(jnp.sum(lam[2] * lam[3]))
                + lambda_init)
    blk = min(ATTN_BLOCK, seq)
    scalars = jnp.concatenate(
        [lam_full[None], p['rel_bias'][REL_BUCKETS - 1, :] * LOG2E]).astype(F32)
    y_b = _diff_attn(scalars, proj, p['attn_bias_tiles'],
                     p['diff_norm'][layer][None, :], batch=batch, seq=seq, blk=blk,
                     heads=ATTN_HEADS_PER_STEP, out_scale=1.0 - lambda_init)

    x2d = _merge(y_a, y_b, y_c, proj, x2d, p['w_branch'], p['w_out'],
                 p['norm_mix_post'][layer][None, :], layer=layer, tm=MERGE_TM)

    x3d = _ffn(x2d.reshape(batch, seq, D_MODEL), p['norm_ffn_pre'][layer][None, :],
               p['w_up'], p['ffn_conv'][layer], p['ffn_conv_b'][layer][None, :],
               p['w_down'], p['norm_ffn_post'][layer][None, :], layer=layer,
               tl=min(FFN_TL, seq), tf=FFN_TF)
    return x3d.reshape(batch * seq, D_MODEL)


def kernel(x, norm_mix_pre, norm_mix_post, norm_ffn_pre, norm_ffn_post, w_in, mlstm_b_if, mlstm_conv, mlstm_norm, diff_lambda, diff_norm, rel_bias, s5_lambda_re, s5_lambda_im, s5_log_dt, s5_b_re, s5_b_im, s5_c_re, s5_c_im, s5_d, s5_w_glu, w_branch, w_out, w_up, ffn_conv, ffn_conv_b, w_down):
    n_if = 2 * M_HEADS
    split = 4 * M_HEADS * M_HEAD_DIM
    w_main = jnp.concatenate([w_in[:, :, :split], w_in[:, :, split + n_if:]],
                             axis=2).astype(BF16)
    w_gate = jnp.pad(w_in[:, :, split:split + n_if],
                     ((0, 0), (0, 0), (0, N_GATE_PAD - n_if))).astype(BF16)
    p = dict(norm_mix_pre=norm_mix_pre, norm_mix_post=norm_mix_post,
             norm_ffn_pre=norm_ffn_pre, norm_ffn_post=norm_ffn_post,
             w_main=w_main, w_gate=w_gate,
             mlstm_b_if=mlstm_b_if, mlstm_conv=mlstm_conv, mlstm_norm=mlstm_norm,
             diff_lambda=diff_lambda, diff_norm=diff_norm, rel_bias=rel_bias,
             s5_lambda_re=s5_lambda_re, s5_lambda_im=s5_lambda_im, s5_log_dt=s5_log_dt,
             s5_b_re=s5_b_re, s5_b_im=s5_b_im, s5_c_re=s5_c_re, s5_c_im=s5_c_im,
             s5_d=s5_d, w_glu=s5_w_glu.astype(BF16), w_branch=w_branch.astype(BF16),
             w_out=w_out.astype(BF16), w_up=w_up.astype(BF16),
             ffn_conv=ffn_conv, ffn_conv_b=ffn_conv_b, w_down=w_down.astype(BF16))
    batch, seq, _ = x.shape
    p['attn_bias_tiles'] = _attn_bias_tiles(rel_bias, min(ATTN_BLOCK, seq))
    x2d = x.reshape(batch * seq, D_MODEL)
    for layer in range(DEPTH):
        x2d = _layer(x2d, batch, seq, layer, p)
    return x2d.reshape(batch, seq, D_MODEL)
```

```python
import functools
import math

import jax
import jax.numpy as jnp
from jax import lax
from jax.experimental import pallas as pl
from jax.experimental.pallas import tpu as pltpu

F32 = jnp.float32
BF16 = jnp.bfloat16
HIGHEST = lax.Precision.HIGHEST

D_MODEL = 2048
DEPTH = 2
BRANCH_WIDTH = 1024
N_BRANCH = 3
M_HEADS = 4
M_HEAD_DIM = 256
M_CONV = 4
DA_HEADS = 4
DA_HEAD_DIM = 128
DA_V_DIM = 256
REL_BUCKETS = 32
REL_MAX_DIST = 128
S5_GROUP = 16
S5_GROUPS = 64
S5_STATE = 64
D_FF = 5632
FFN_CONV = 3
EPS = 1e-6

LANES = 128
SUBLANES = 8
VMEM_LIMIT = 56 * 1024 * 1024

COL_QM, COL_KM, COL_VM, COL_OM = 0, 1, 2, 3
COL_QD, COL_KD, COL_VD, COL_US = 4, 5, 6, 7
COL_GATE = 8
N_MAIN = COL_GATE * BRANCH_WIDTH + N_BRANCH * D_MODEL
N_GATE_PAD = LANES

S5_SLABS = 8
S5_SLAB_STATES = (S5_GROUPS // S5_SLABS) * S5_STATE

NEG_BIG = -1e30
LOG2E = math.log2(math.e)


def _cparams(sem):
    return pltpu.CompilerParams(dimension_semantics=sem, vmem_limit_bytes=VMEM_LIMIT)


def _sigmoid(x):
    return 0.5 * jnp.tanh(0.5 * x) + 0.5


def _gelu_tanh(x):
    c = math.sqrt(2.0 / math.pi)
    return 0.5 * x * (1.0 + jnp.tanh(c * (x + 0.044715 * (x * x * x))))


def _rms(x, gain):
    var = jnp.mean(x * x, axis=-1, keepdims=True)
    return x * lax.rsqrt(var + EPS) * gain


def _in_proj_kernel(x_ref, g_ref, wg_ref, w_ref, o_ref, og_ref, h_scr):
    @pl.when(pl.program_id(1) == 0)
    def _():
        h_scr[...] = _rms(x_ref[...], g_ref[...]).astype(BF16)
        og_ref[...] = jnp.dot(h_scr[...], wg_ref[...], preferred_element_type=F32)

    o_ref[...] = jnp.dot(h_scr[...], w_ref[...],
                         preferred_element_type=F32).astype(BF16)


def _in_proj(x2d, gain, w_gate, w_main, *, layer, tm, tn):
    m = x2d.shape[0]
    return pl.pallas_call(
        _in_proj_kernel,
        out_shape=(jax.ShapeDtypeStruct((m, N_MAIN), BF16),
                   jax.ShapeDtypeStruct((m, N_GATE_PAD), F32)),
        grid=(m // tm, N_MAIN // tn),
        in_specs=[pl.BlockSpec((tm, D_MODEL), lambda i, n: (i, 0)),
                  pl.BlockSpec((1, D_MODEL), lambda i, n: (0, 0)),
                  pl.BlockSpec((None, D_MODEL, N_GATE_PAD), lambda i, n: (layer, 0, 0)),
                  pl.BlockSpec((None, D_MODEL, tn), lambda i, n: (layer, 0, n))],
        out_specs=(pl.BlockSpec((tm, tn), lambda i, n: (i, n)),
                   pl.BlockSpec((tm, N_GATE_PAD), lambda i, n: (i, 0))),
        scratch_shapes=[pltpu.VMEM((tm, D_MODEL), BF16)],
        compiler_params=_cparams(("parallel", "arbitrary")),
        name="in_proj",
    )(x2d, gain, w_gate, w_main)


def _mlstm_chunk(r0, q_ref, k_ref, v_ref, o_ref, gt_ref, gb_ref, cw_ref, ng_ref, y_ref,
                 c_scr, n_scr, m_scr, qe_scr, ke_scr, *, t):
    hd = M_HEAD_DIM
    width = M_HEADS * hd
    rows = slice(r0, r0 + t)

    qe_scr[SUBLANES:SUBLANES + t, :] = q_ref[rows, :].astype(F32)
    ke_scr[SUBLANES:SUBLANES + t, :] = k_ref[rows, :].astype(F32)

    gates = gt_ref[rows, :] + gb_ref[...]
    log_f = jnp.minimum(gates, 0.0) - jnp.log1p(jnp.exp(-jnp.abs(gates)))
    row = lax.broadcasted_iota(jnp.int32, (t, t), 0)
    col = lax.broadcasted_iota(jnp.int32, (t, t), 1)
    causal = col <= row
    cum = jnp.dot(causal.astype(F32), log_f, preferred_element_type=F32,
                  precision=HIGHEST)
    gates_t = gates.T
    cum_t = cum.T

    for h in range(M_HEADS):
        sl = slice(h * hd, (h + 1) * hd)
        ksl = slice(width + h * hd, width + (h + 1) * hd)
        qc = jnp.zeros((t, hd), F32)
        kc = jnp.zeros((t, hd), F32)
        for j in range(M_CONV):
            off = SUBLANES - (M_CONV - 1) + j
            qc = qc + cw_ref[j:j + 1, sl] * qe_scr[off:off + t, sl]
            kc = kc + cw_ref[j:j + 1, ksl] * ke_scr[off:off + t, sl]
        qc = qc * _sigmoid(qc)
        kc = kc * _sigmoid(kc) * (hd ** -0.5)
        qb = qc.astype(BF16)
        kb = kc.astype(BF16)
        vb = v_ref[rows, sl]

        li_row = gates_t[h:h + 1, :]
        b_row = cum_t[M_HEADS + h:M_HEADS + h + 1, :]
        li_col = gates[:, h:h + 1]
        b_col = cum[:, M_HEADS + h:M_HEADS + h + 1]
        m_prev = m_scr[h:h + 1, 0:1]
        c_prev = c_scr[h]
        n_prev = n_scr[h:h + 1, :]

        dmat = jnp.where(causal, b_col - b_row + li_row, -jnp.inf)
        inter = b_col + m_prev
        m_t = jnp.maximum(inter, jnp.max(dmat, axis=-1, keepdims=True))
        s = lax.dot_general(qb, kb, (((1,), (1,)), ((), ())),
                            preferred_element_type=F32) * jnp.exp(dmat - m_t)
        w_inter = jnp.exp(inter - m_t)
        num = (jnp.dot(s.astype(BF16), vb, preferred_element_type=F32)
               + w_inter * jnp.dot(qb, c_prev.astype(BF16), preferred_element_type=F32))
        den = (jnp.sum(s, axis=-1, keepdims=True)
               + w_inter * jnp.sum(qc * n_prev, axis=-1, keepdims=True))
        hh = num / jnp.maximum(jnp.abs(den), jnp.exp(-m_t))

        g = cum[t - 1:t, M_HEADS + h:M_HEADS + h + 1]
        a_col = g - b_col + li_col
        m_new = jnp.maximum(g + m_prev, jnp.max(a_col, axis=0, keepdims=True))
        ws = jnp.exp(a_col - m_new)
        decay = jnp.exp(g + m_prev - m_new)
        kw = ws * kc
        c_scr[h] = decay * c_prev + lax.dot_general(
            kw.astype(BF16), vb, (((0,), (0,)), ((), ())), preferred_element_type=F32)
        n_scr[h:h + 1, :] = decay * n_prev + jnp.sum(kw, axis=0, keepdims=True)
        m_scr[h:h + 1, :] = jnp.broadcast_to(m_new, (1, LANES))

        hn = _rms(hh, ng_ref[:, sl])
        y_ref[rows, sl] = (_sigmoid(o_ref[rows, sl].astype(F32)) * hn).astype(BF16)
        yield

    qe_scr[0:SUBLANES, :] = qe_scr[t:t + SUBLANES, :]
    ke_scr[0:SUBLANES, :] = ke_scr[t:t + SUBLANES, :]


def _mix_kernel(q_ref, k_ref, v_ref, o_ref, gt_ref, gb_ref, cw_ref, ng_ref,
                u_ref, wb_ref, wc_ref, are_ref, aim_ref, d_ref, wg_ref,
                ya_ref, yc_ref,
                c_scr, n_scr, m_scr, qe_scr, ke_scr,
                xre_scr, xim_scr, bu_scr, uil_scr, yil_scr,
                *, chunk, steps, batch, steps_per_seq):
    i = pl.program_id(0)
    ns = S5_SLAB_STATES
    rows = steps * batch
    n_chunks = rows // chunk
    width = M_HEADS * M_HEAD_DIM

    @pl.when(i == 0)
    def _():
        xre_scr[...] = jnp.zeros_like(xre_scr)
        xim_scr[...] = jnp.zeros_like(xim_scr)

    @pl.when(i % steps_per_seq == 0)
    def _():
        c_scr[...] = jnp.zeros_like(c_scr)
        n_scr[...] = jnp.zeros_like(n_scr)
        m_scr[...] = jnp.zeros_like(m_scr)
        qe_scr[0:SUBLANES, :] = jnp.zeros((SUBLANES, width), F32)
        ke_scr[0:SUBLANES, :] = jnp.zeros((SUBLANES, width), F32)

    def mlstm_heads():
        for j in range(n_chunks):
            yield from _mlstm_chunk(j * chunk, q_ref, k_ref, v_ref, o_ref, gt_ref, gb_ref,
                                    cw_ref, ng_ref, ya_ref, c_scr, n_scr, m_scr, qe_scr,
                                    ke_scr, t=chunk)

    heads = mlstm_heads()
    heads_per_slab = (n_chunks * M_HEADS) // S5_SLABS

    for b in range(batch):
        ub = u_ref[b].astype(F32)
        for c in range(S5_SLABS):
            uil_scr.at[c][pl.ds(b, steps, stride=batch), :] = ub[:, c * LANES:(c + 1) * LANES]

    ys = []
    for s in range(S5_SLABS):
        us = uil_scr[s]
        buf = bu_scr.at[s % 2]
        buf[...] = jnp.dot(us.astype(BF16), wb_ref[s], preferred_element_type=F32)
        a_re = jnp.broadcast_to(are_ref[s], (batch, ns))
        a_im = jnp.broadcast_to(aim_ref[s], (batch, ns))

        x_re = xre_scr[s]
        x_im = xim_scr[s]
        for tt in range(steps):
            if tt == steps // 2:
                for _ in range(heads_per_slab // 2):
                    next(heads, None)
            rs = slice(tt * batch, (tt + 1) * batch)
            n_re = a_re * x_re - a_im * x_im + buf[rs, 0:ns]
            n_im = a_re * x_im + a_im * x_re + buf[rs, ns:2 * ns]
            buf[rs, 0:ns] = n_re
            buf[rs, ns:2 * ns] = n_im
            x_re, x_im = n_re, n_im
        xre_scr[s] = x_re
        xim_scr[s] = x_im
        y = (jnp.dot(buf[...].astype(BF16), wc_ref[s], preferred_element_type=F32)
             + d_ref[:, s * LANES:(s + 1) * LANES] * us)
        ys.append(_gelu_tanh(y).astype(BF16))

        for _ in range(heads_per_slab - heads_per_slab // 2):
            next(heads, None)

    for _ in heads:
        pass

    yb = jnp.concatenate(ys, axis=1)
    half = BRANCH_WIDTH
    a = jnp.dot(yb, wg_ref[:, 0:half], preferred_element_type=F32)
    g = jnp.dot(yb, wg_ref[:, half:2 * half], preferred_element_type=F32)
    out = a * _sigmoid(g)
    for c in range(S5_SLABS):
        yil_scr[c] = out[:, c * LANES:(c + 1) * LANES]
    for b in range(batch):
        for c in range(S5_SLABS):
            yc_ref[b, :, c * LANES:(c + 1) * LANES] = (
                yil_scr.at[c][pl.ds(b, steps, stride=batch), :].astype(BF16))


def _mlstm_s5(proj, gates, gate_bias, conv_w, norm_g, wb, wc, a_re, a_im, d_skip, w_glu, *,
              layer, batch, seq, steps, chunk):
    rows = steps * batch
    chunk = min(chunk, rows)
    assert seq % rows == 0 and rows % chunk == 0
    ns = S5_SLAB_STATES
    width = M_HEADS * M_HEAD_DIM
    const2 = lambda i: (0, 0)
    const3 = lambda i: (0, 0, 0)

    def col_spec(cb):
        return pl.BlockSpec((rows, width), lambda i: (i, cb))

    return pl.pallas_call(
        functools.partial(_mix_kernel, chunk=chunk, steps=steps, batch=batch,
                          steps_per_seq=seq // rows),
        out_shape=(jax.ShapeDtypeStruct((batch * seq, width), BF16),
                   jax.ShapeDtypeStruct((batch, seq, BRANCH_WIDTH), BF16)),
        grid=(seq // steps,),
        in_specs=[col_spec(COL_QM), col_spec(COL_KM), col_spec(COL_VM), col_spec(COL_OM),
                  pl.BlockSpec((rows, N_GATE_PAD), lambda i: (i, 0)),
                  pl.BlockSpec((1, N_GATE_PAD), const2),
                  pl.BlockSpec((M_CONV, 2 * width), const2),
                  pl.BlockSpec((1, width), const2),
                  pl.BlockSpec((batch, steps, BRANCH_WIDTH), lambda i: (0, i, COL_US)),
                  pl.BlockSpec((S5_SLABS, LANES, 2 * ns), const3),
                  pl.BlockSpec((S5_SLABS, 2 * ns, LANES), const3),
                  pl.BlockSpec((S5_SLABS, 1, ns), const3),
                  pl.BlockSpec((S5_SLABS, 1, ns), const3),
                  pl.BlockSpec((1, BRANCH_WIDTH), const2),
                  pl.BlockSpec((None, BRANCH_WIDTH, 2 * BRANCH_WIDTH),
                               lambda i: (layer, 0, 0))],
        out_specs=(pl.BlockSpec((rows, width), lambda i: (i, 0)),
                   pl.BlockSpec((batch, steps, BRANCH_WIDTH), lambda i: (0, i, 0))),
        scratch_shapes=[pltpu.VMEM((M_HEADS, M_HEAD_DIM, M_HEAD_DIM), F32),
                        pltpu.VMEM((M_HEADS, M_HEAD_DIM), F32),
                        pltpu.VMEM((M_HEADS, LANES), F32),
                        pltpu.VMEM((chunk + SUBLANES, width), F32),
                        pltpu.VMEM((chunk + SUBLANES, width), F32),
                        pltpu.VMEM((S5_SLABS, batch, ns), F32),
                        pltpu.VMEM((S5_SLABS, batch, ns), F32),
                        pltpu.VMEM((2, rows, 2 * ns), F32),
                        pltpu.VMEM((S5_SLABS, rows, LANES), F32),
                        pltpu.VMEM((S5_SLABS, rows, LANES), F32)],
        compiler_params=_cparams(("arbitrary",)),
        name="mlstm_s5",
    )(proj, proj, proj, proj, gates, gate_bias, conv_w, norm_g,
      proj.reshape(batch, seq, N_MAIN), wb, wc, a_re, a_im, d_skip, w_glu)


def _attn_kernel(sc_ref, q_ref, k_ref, v_ref, bias_ref, ng_ref, y_ref,
                 m_scr, l_scr, acc_scr, vt_scr, s_scr, *, blk, heads, out_scale):
    t = blk
    d = DA_HEAD_DIM
    w = DA_V_DIM
    h0 = pl.program_id(1) * heads
    i = pl.program_id(2)
    lam = sc_ref[0]

    qs = (q_ref[...].astype(F32) * (d ** -0.5 * LOG2E)).astype(BF16)

    @pl.when(i == 0)
    def _():
        vt_scr[...] = v_ref[...].astype(F32).T.astype(BF16)

    def block_step(j, bias_tile=None, first=False):
        start = pl.multiple_of(j * t, t)
        kb = k_ref[pl.ds(start, t), :]
        vtb = vt_scr[:, pl.ds(start, t)]
        far = bias_tile is None
        for ci in range(2 * heads):
            cols = slice(ci * d, (ci + 1) * d)
            s_scr[ci] = lax.dot_general(kb[:, cols], qs[:, cols], (((1,), (1,)), ((), ())),
                                        preferred_element_type=F32)
        for hh in range(heads):
            bias = sc_ref[1 + h0 + hh] if far else bias_ref[hh, bias_tile]
            for c in range(2):
                ci = 2 * hh + c
                s = s_scr[ci]
                if far:
                    m_blk = jnp.max(s, axis=0, keepdims=True) + bias
                else:
                    s = s + bias
                    m_blk = jnp.max(s, axis=0, keepdims=True)
                m_new = m_blk if first else jnp.maximum(m_scr[ci], m_blk)
                p = jnp.exp2(s - ((m_new - bias) if far else m_new))
                l_blk = jnp.sum(p, axis=0, keepdims=True)
                pv = jnp.dot(vtb[hh * w:(hh + 1) * w, :], p.astype(BF16),
                             preferred_element_type=F32)
                if first:
                    l_scr[ci] = l_blk
                    acc_scr[ci] = pv
                else:
                    alpha = jnp.exp2(m_scr[ci] - m_new)
                    l_scr[ci] = alpha * l_scr[ci] + l_blk
                    acc_scr[ci] = alpha * acc_scr[ci] + pv
                m_scr[ci] = m_new

    block_step(i, 0, first=True)

    @pl.when(i >= 1)
    def _():
        block_step(i - 1, 1)

    def far_body(j, carry):
        block_step(j)
        return carry

    lax.fori_loop(0, jnp.maximum(i - 1, 0), far_body, 0)

    for hh in range(heads):
        ca, cb = 2 * hh, 2 * hh + 1
        out_t = (acc_scr[ca] * (1.0 / l_scr[ca])
                 - lam * (acc_scr[cb] * (1.0 / l_scr[cb])))
        hs = slice(hh * w, (hh + 1) * w)
        y_ref[:, hs] = (_rms(out_t.T, ng_ref[:, hs]) * out_scale).astype(BF16)


def _diff_attn(scalars, proj, bias_tiles, norm_g, *, batch, seq, blk, heads, out_scale):
    nq = seq // blk
    kvw = heads * DA_V_DIM
    per_row = BRANCH_WIDTH // kvw
    return pl.pallas_call(
        functools.partial(_attn_kernel, blk=blk, heads=heads, out_scale=out_scale),
        out_shape=jax.ShapeDtypeStruct((batch * seq, DA_HEADS * DA_V_DIM), BF16),
        grid=(batch, DA_HEADS // heads, nq),
        in_specs=[pl.BlockSpec(memory_space=pltpu.SMEM),
                  pl.BlockSpec((blk, kvw), lambda b, h, i: (b * nq + i, COL_QD * per_row + h)),
                  pl.BlockSpec((seq, kvw), lambda b, h, i: (b, COL_KD * per_row + h)),
                  pl.BlockSpec((seq, kvw), lambda b, h, i: (b, COL_VD * per_row + h)),
                  pl.BlockSpec((heads, 2, blk, blk), lambda b, h, i: (h, 0, 0, 0)),
                  pl.BlockSpec((1, kvw), lambda b, h, i: (0, h))],
        out_specs=pl.BlockSpec((blk, kvw), lambda b, h, i: (b * nq + i, h)),
        scratch_shapes=[pltpu.VMEM((2 * heads, 1, blk), F32),
                        pltpu.VMEM((2 * heads, 1, blk), F32),
                        pltpu.VMEM((2 * heads, DA_V_DIM, blk), F32),
                        pltpu.VMEM((kvw, seq), BF16),
                        pltpu.VMEM((2 * heads, blk, blk), F32)],
        compiler_params=_cparams(("parallel", "parallel", "arbitrary")),
        name="diff_attn",
    )(scalars, proj, proj, proj, bias_tiles, norm_g)


def _merge_kernel(ya_ref, yb_ref, yc_ref, g0_ref, g1_ref, g2_ref, x_ref, wbr_ref,
                  wo_ref, ng_ref, o_ref):
    merged = None
    for n, (y_ref, g_ref) in enumerate(((ya_ref, g0_ref), (yb_ref, g1_ref),
                                        (yc_ref, g2_ref))):
        z = jnp.dot(y_ref[...], wbr_ref[n], preferred_element_type=F32)
        term = _sigmoid(g_ref[...].astype(F32)) * z
        merged = term if merged is None else merged + term
    mix = jnp.dot(merged.astype(BF16), wo_ref[...], preferred_element_type=F32)
    o_ref[...] = x_ref[...] + _rms(mix, ng_ref[...])


def _merge(y_a, y_b, y_c, proj, x2d, w_branch, w_out, norm_g, *, layer, tm):
    m = x2d.shape[0]
    row = lambda i: (i, 0)

    def gate_spec(n):
        return pl.BlockSpec((tm, D_MODEL), lambda i: (i, COL_GATE // 2 + n))

    return pl.pallas_call(
        _merge_kernel,
        out_shape=jax.ShapeDtypeStruct((m, D_MODEL), F32),
        grid=(m // tm,),
        in_specs=[pl.BlockSpec((tm, BRANCH_WIDTH), row),
                  pl.BlockSpec((tm, BRANCH_WIDTH), row),
                  pl.BlockSpec((tm, BRANCH_WIDTH), row),
                  gate_spec(0), gate_spec(1), gate_spec(2),
                  pl.BlockSpec((tm, D_MODEL), row),
                  pl.BlockSpec((None, N_BRANCH, BRANCH_WIDTH, D_MODEL),
                               lambda i: (layer, 0, 0, 0), pipeline_mode=pl.Buffered(1)),
                  pl.BlockSpec((None, D_MODEL, D_MODEL), lambda i: (layer, 0, 0),
                               pipeline_mode=pl.Buffered(1)),
                  pl.BlockSpec((1, D_MODEL), lambda i: (0, 0))],
        out_specs=pl.BlockSpec((tm, D_MODEL), row),
        compiler_params=_cparams(("parallel",)),
        name="merge",
    )(y_a, y_b, y_c, proj, proj, proj, x2d, w_branch, w_out, norm_g)


def _ffn_kernel(x_ref, g_ref, wa_ref, wv_ref, cwa_ref, cwv_ref, cba_ref,
                cbv_ref, wd_ref, ng_ref, o_ref, h_scr, tail_scr, *, tl):
    i = pl.program_id(1)
    f = pl.program_id(2)
    nf = pl.num_programs(2)
    hal = SUBLANES

    @pl.when(f == 0)
    def _():
        h_scr[...] = _rms(x_ref[0], g_ref[...]).astype(BF16)
        o_ref[0] = jnp.zeros((tl, D_MODEL), F32)

    @pl.when(i == 0)
    def _():
        tail_scr[f] = jnp.zeros(tail_scr.shape[1:], F32)

    hb = h_scr[...]

    def conv(part, w_ref, cw_ref, cb_ref):
        up = jnp.dot(hb, w_ref[...], preferred_element_type=F32)
        ext = jnp.concatenate([tail_scr[f, part], up], axis=0)
        tail_scr[f, part] = up[tl - hal:tl, :]
        out = cb_ref[...] + cw_ref[FFN_CONV - 1:FFN_CONV, :] * up
        for j in range(FFN_CONV - 1):
            off = hal - (FFN_CONV - 1) + j
            out = out + cw_ref[j:j + 1, :] * ext[off:off + tl, :]
        return out

    a = conv(0, wa_ref, cwa_ref, cba_ref)
    v = conv(1, wv_ref, cwv_ref, cbv_ref)
    act = (_gelu_tanh(a) * v).astype(BF16)
    o_ref[0] += jnp.dot(act, wd_ref[...], preferred_element_type=F32)

    @pl.when(f == nf - 1)
    def _():
        o_ref[0] = x_ref[0] + _rms(o_ref[0], ng_ref[...])


def _ffn(x3d, gain, w_up, conv_w, conv_b, w_down, norm_g, *, layer, tl, tf):
    batch, seq, _ = x3d.shape
    nfb = D_FF // tf
    hal = SUBLANES
    return pl.pallas_call(
        functools.partial(_ffn_kernel, tl=tl),
        out_shape=jax.ShapeDtypeStruct((batch, seq, D_MODEL), F32),
        grid=(batch, seq // tl, nfb),
        in_specs=[pl.BlockSpec((1, tl, D_MODEL), lambda b, i, f: (b, i, 0)),
                  pl.BlockSpec((1, D_MODEL), lambda b, i, f: (0, 0)),
                  pl.BlockSpec((None, D_MODEL, tf), lambda b, i, f: (layer, 0, f)),
                  pl.BlockSpec((None, D_MODEL, tf), lambda b, i, f: (layer, 0, nfb + f)),
                  pl.BlockSpec((FFN_CONV, tf), lambda b, i, f: (0, f)),
                  pl.BlockSpec((FFN_CONV, tf), lambda b, i, f: (0, nfb + f)),
                  pl.BlockSpec((1, tf), lambda b, i, f: (0, f)),
                  pl.BlockSpec((1, tf), lambda b, i, f: (0, nfb + f)),
                  pl.BlockSpec((None, tf, D_MODEL), lambda b, i, f: (layer, f, 0)),
                  pl.BlockSpec((1, D_MODEL), lambda b, i, f: (0, 0))],
        out_specs=pl.BlockSpec((1, tl, D_MODEL), lambda b, i, f: (b, i, 0)),
        scratch_shapes=[pltpu.VMEM((tl, D_MODEL), BF16),
                        pltpu.VMEM((nfb, 2, hal, tf), F32)],
        compiler_params=_cparams(("parallel", "arbitrary", "arbitrary")),
        name="conv_ffn",
    )(x3d, gain, w_up, w_up, conv_w, conv_w, conv_b, conv_b, w_down, norm_g)


def _t5_bucket(dist):
    n = jnp.maximum(dist, 0)
    max_exact = REL_BUCKETS // 2
    nf = jnp.maximum(n, 1).astype(F32)
    large = max_exact + (jnp.log(nf / max_exact) / math.log(REL_MAX_DIST / max_exact)
                         * (REL_BUCKETS - max_exact)).astype(jnp.int32)
    large = jnp.minimum(large, REL_BUCKETS - 1)
    return jnp.where(n < max_exact, n, large)


def _attn_bias_tiles(rel_bias, blk):
    qi = jnp.arange(blk, dtype=jnp.int32)[None, :]
    kj = jnp.arange(blk, dtype=jnp.int32)[:, None]
    table = rel_bias.astype(F32) * LOG2E
    last = table[REL_BUCKETS - 1][:, None, None]
    tiles = []
    for off in (0, blk):
        dist = qi - kj + off
        bucket = _t5_bucket(dist)
        bias = jnp.broadcast_to(last, (DA_HEADS, blk, blk))
        for b in range(REL_BUCKETS - 1):
            bias = jnp.where(bucket == b, table[b][:, None, None], bias)
        tiles.append(jnp.where(dist >= 0, bias, NEG_BIG))
    return jnp.stack(tiles, axis=1)


def _s5_params(lam_re, lam_im, log_dt, b_re, b_im, c_re, c_im):
    dt = jnp.exp(log_dt)[:, None]
    mag = jnp.exp(lam_re * dt)
    a_re = mag * jnp.cos(lam_im * dt)
    a_im = mag * jnp.sin(lam_im * dt)
    den = lam_re * lam_re + lam_im * lam_im
    z_re = ((a_re - 1.0) * lam_re + a_im * lam_im) / den
    z_im = (a_im * lam_re - (a_re - 1.0) * lam_im) / den
    bb_re = z_re[..., None] * b_re - z_im[..., None] * b_im
    bb_im = z_re[..., None] * b_im + z_im[..., None] * b_re
    gs = S5_GROUPS // S5_SLABS
    eye = jnp.eye(gs, dtype=F32)

    def in_blocks(bb):
        bb = bb.reshape(S5_SLABS, gs, S5_STATE, S5_GROUP)
        w = jnp.einsum('sgpc,gh->sgchp', bb, eye)
        return w.reshape(S5_SLABS, gs * S5_GROUP, gs * S5_STATE)

    def out_blocks(cc):
        cc = cc.reshape(S5_SLABS, gs, S5_GROUP, S5_STATE)
        w = jnp.einsum('sgcp,gh->sgphc', cc, eye)
        return w.reshape(S5_SLABS, gs * S5_STATE, gs * S5_GROUP)

    wb = jnp.concatenate([in_blocks(bb_re), in_blocks(bb_im)], axis=-1).astype(BF16)
    wc = jnp.concatenate([out_blocks(c_re), out_blocks(-c_im)], axis=-2).astype(BF16)
    a_re = a_re.reshape(S5_SLABS, 1, S5_SLAB_STATES)
    a_im = a_im.reshape(S5_SLABS, 1, S5_SLAB_STATES)
    return wb, wc, a_re, a_im


IN_PROJ_TM, IN_PROJ_TN = 1024, 2048
MLSTM_CHUNK = 128
ATTN_BLOCK = 512
ATTN_HEADS_PER_STEP = 2
S5_STEPS = 32
MERGE_TM = 256
FFN_TL, FFN_TF = 512, 512


def _layer(x2d, batch, seq, layer, p):
    n_if = 2 * M_HEADS
    proj, gates = _in_proj(x2d, p['norm_mix_pre'][layer][None, :], p['w_gate'],
                           p['w_main'], layer=layer, tm=min(IN_PROJ_TM, batch * seq),
                           tn=IN_PROJ_TN)

    gate_bias = jnp.pad(p['mlstm_b_if'][layer].reshape(1, n_if),
                        ((0, 0), (0, N_GATE_PAD - n_if)))
    wb, wc, a_re, a_im = _s5_params(
        p['s5_lambda_re'][layer], p['s5_lambda_im'][layer], p['s5_log_dt'][layer],
        p['s5_b_re'][layer], p['s5_b_im'][layer], p['s5_c_re'][layer], p['s5_c_im'][layer])
    y_a, y_c = _mlstm_s5(proj, gates, gate_bias, p['mlstm_conv'][layer],
                         p['mlstm_norm'][layer][None, :], wb, wc, a_re, a_im,
                         p['s5_d'][layer][None, :], p['w_glu'], layer=layer, batch=batch,
                         seq=seq, steps=min(S5_STEPS, seq), chunk=MLSTM_CHUNK)
    y_c = y_c.reshape(batch * seq, BRANCH_WIDTH)

    lambda_init = 0.8 - 0.6 * math.exp(-0.3 * layer)
    lam = p['diff_lambda'][layer]
    lam_full = (jnp.exp(jnp.sum(lam[0] * lam[1])) - jnp.exp(jnp.sum(lam[2] * lam[3]))
                + lambda_init)
    blk = min(ATTN_BLOCK, seq)
    scalars = jnp.concatenate(
        [lam_full[None], p['rel_bias'][REL_BUCKETS - 1, :] * LOG2E]).astype(F32)
    y_b = _diff_attn(scalars, proj, p['attn_bias_tiles'],
                     p['diff_norm'][layer][None, :], batch=batch, seq=seq, blk=blk,
                     heads=ATTN_HEADS_PER_STEP, out_scale=1.0 - lambda_init)

    x2d = _merge(y_a, y_b, y_c, proj, x2d, p['w_branch'], p['w_out'],
                 p['norm_mix_post'][layer][None, :], layer=layer, tm=MERGE_TM)

    x3d = _ffn(x2d.reshape(batch, seq, D_MODEL), p['norm_ffn_pre'][layer][None, :],
               p['w_up'], p['ffn_conv'][layer], p['ffn_conv_b'][layer][None, :],
               p['w_down'], p['norm_ffn_post'][layer][None, :], layer=layer,
               tl=min(FFN_TL, seq), tf=FFN_TF)
    return x3d.reshape(batch * seq, D_MODEL)


def kernel(x, norm_mix_pre, norm_mix_post, norm_ffn_pre, norm_ffn_post, w_in, mlstm_b_if, mlstm_conv, mlstm_norm, diff_lambda, diff_norm, rel_bias, s5_lambda_re, s5_lambda_im, s5_log_dt, s5_b_re, s5_b_im, s5_c_re, s5_c_im, s5_d, s5_w_glu, w_branch, w_out, w_up, ffn_conv, ffn_conv_b, w_down):
    n_if = 2 * M_HEADS
    split = 4 * M_HEADS * M_HEAD_DIM
    w_main = jnp.concatenate([w_in[:, :, :split], w_in[:, :, split + n_if:]],
                             axis=2).astype(BF16)
    w_gate = jnp.pad(w_in[:, :, split:split + n_if],
                     ((0, 0), (0, 0), (0, N_GATE_PAD - n_if))).astype(BF16)
    p = dict(norm_mix_pre=norm_mix_pre, norm_mix_post=norm_mix_post,
             norm_ffn_pre=norm_ffn_pre, norm_ffn_post=norm_ffn_post,
             w_main=w_main, w_gate=w_gate,
             mlstm_b_if=mlstm_b_if, mlstm_conv=mlstm_conv, mlstm_norm=mlstm_norm,
             diff_lambda=diff_lambda, diff_norm=diff_norm, rel_bias=rel_bias,
             s5_lambda_re=s5_lambda_re, s5_lambda_im=s5_lambda_im, s5_log_dt=s5_log_dt,
             s5_b_re=s5_b_re, s5_b_im=s5_b_im, s5_c_re=s5_c_re, s5_c_im=s5_c_im,
             s5_d=s5_d, w_glu=s5_w_glu.astype(BF16), w_branch=w_branch.astype(BF16),
             w_out=w_out.astype(BF16), w_up=w_up.astype(BF16),
             ffn_conv=ffn_conv, ffn_conv_b=ffn_conv_b, w_down=w_down.astype(BF16))
    batch, seq, _ = x.shape
    p['attn_bias_tiles'] = _attn_bias_tiles(rel_bias, min(ATTN_BLOCK, seq))
    x2d = x.reshape(batch * seq, D_MODEL)
    for layer in range(DEPTH):
        x2d = _layer(x2d, batch, seq, layer, p)
    return x2d.reshape(batch, seq, D_MODEL)
```

```python
import functools
import math

import jax
import jax.numpy as jnp
from jax import lax
from jax.experimental import pallas as pl
from jax.experimental.pallas import tpu as pltpu

F32 = jnp.float32
BF16 = jnp.bfloat16
HIGHEST = lax.Precision.HIGHEST

D_MODEL = 2048
DEPTH = 2
BRANCH_WIDTH = 1024
N_BRANCH = 3
M_HEADS = 4
M_HEAD_DIM = 256
M_CONV = 4
DA_HEADS = 4
DA_HEAD_DIM = 128
DA_V_DIM = 256
REL_BUCKETS = 32
REL_MAX_DIST = 128
S5_GROUP = 16
S5_GROUPS = 64
S5_STATE = 64
D_FF = 5632
FFN_CONV = 3
EPS = 1e-6

LANES = 128
SUBLANES = 8
VMEM_LIMIT = 56 * 1024 * 1024

COL_QM, COL_KM, COL_VM, COL_OM = 0, 1, 2, 3
COL_QD, COL_KD, COL_VD, COL_US = 4, 5, 6, 7
COL_GATE = 8
N_MAIN = COL_GATE * BRANCH_WIDTH + N_BRANCH * D_MODEL
N_GATE_PAD = LANES

S5_SLABS = 8
S5_SLAB_STATES = (S5_GROUPS // S5_SLABS) * S5_STATE

NEG_BIG = -1e30
LOG2E = math.log2(math.e)


def _cparams(sem):
    return pltpu.CompilerParams(dimension_semantics=sem, vmem_limit_bytes=VMEM_LIMIT)


def _sigmoid(x):
    return 0.5 * jnp.tanh(0.5 * x) + 0.5


def _gelu_tanh(x):
    c = math.sqrt(2.0 / math.pi)
    return 0.5 * x * (1.0 + jnp.tanh(c * (x + 0.044715 * (x * x * x))))


def _rms(x, gain):
    var = jnp.mean(x * x, axis=-1, keepdims=True)
    return x * lax.rsqrt(var + EPS) * gain


def _in_proj_kernel(x_ref, g_ref, wg_ref, w_ref, o_ref, og_ref, h_scr):
    @pl.when(pl.program_id(1) == 0)
    def _():
        h_scr[...] = _rms(x_ref[...], g_ref[...]).astype(BF16)
        og_ref[...] = jnp.dot(h_scr[...], wg_ref[...], preferred_element_type=F32)

    o_ref[...] = jnp.dot(h_scr[...], w_ref[...],
                         preferred_element_type=F32).astype(BF16)


def _in_proj(x2d, gain, w_gate, w_main, *, layer, tm, tn):
    m = x2d.shape[0]
    return pl.pallas_call(
        _in_proj_kernel,
        out_shape=(jax.ShapeDtypeStruct((m, N_MAIN), BF16),
                   jax.ShapeDtypeStruct((m, N_GATE_PAD), F32)),
        grid=(m // tm, N_MAIN // tn),
        in_specs=[pl.BlockSpec((tm, D_MODEL), lambda i, n: (i, 0)),
                  pl.BlockSpec((1, D_MODEL), lambda i, n: (0, 0)),
                  pl.BlockSpec((None, D_MODEL, N_GATE_PAD), lambda i, n: (layer, 0, 0)),
                  pl.BlockSpec((None, D_MODEL, tn), lambda i, n: (layer, 0, n))],
        out_specs=(pl.BlockSpec((tm, tn), lambda i, n: (i, n)),
                   pl.BlockSpec((tm, N_GATE_PAD), lambda i, n: (i, 0))),
        scratch_shapes=[pltpu.VMEM((tm, D_MODEL), BF16)],
        compiler_params=_cparams(("parallel", "arbitrary")),
        name="in_proj",
    )(x2d, gain, w_gate, w_main)


def _mlstm_chunk(r0, q_ref, k_ref, v_ref, o_ref, gt_ref, gb_ref, cw_ref, ng_ref, y_ref,
                 c_scr, n_scr, m_scr, qe_scr, ke_scr, *, t):
    hd = M_HEAD_DIM
    width = M_HEADS * hd
    rows = slice(r0, r0 + t)

    qe_scr[SUBLANES:SUBLANES + t, :] = q_ref[rows, :].astype(F32)
    ke_scr[SUBLANES:SUBLANES + t, :] = k_ref[rows, :].astype(F32)

    gates = gt_ref[rows, :] + gb_ref[...]
    log_f = jnp.minimum(gates, 0.0) - jnp.log1p(jnp.exp(-jnp.abs(gates)))
    row = lax.broadcasted_iota(jnp.int32, (t, t), 0)
    col = lax.broadcasted_iota(jnp.int32, (t, t), 1)
    causal = col <= row
    cum = jnp.dot(causal.astype(F32), log_f, preferred_element_type=F32,
                  precision=HIGHEST)
    gates_t = gates.T
    cum_t = cum.T

    for h in range(M_HEADS):
        sl = slice(h * hd, (h + 1) * hd)
        ksl = slice(width + h * hd, width + (h + 1) * hd)
        qc = jnp.zeros((t, hd), F32)
        kc = jnp.zeros((t, hd), F32)
        for j in range(M_CONV):
            off = SUBLANES - (M_CONV - 1) + j
            qc = qc + cw_ref[j:j + 1, sl] * qe_scr[off:off + t, sl]
            kc = kc + cw_ref[j:j + 1, ksl] * ke_scr[off:off + t, sl]
        qc = qc * _sigmoid(qc)
        kc = kc * _sigmoid(kc) * (hd ** -0.5)
        qb = qc.astype(BF16)
        kb = kc.astype(BF16)
        vb = v_ref[rows, sl]

        li_row = gates_t[h:h + 1, :]
        b_row = cum_t[M_HEADS + h:M_HEADS + h + 1, :]
        li_col = gates[:, h:h + 1]
        b_col = cum[:, M_HEADS + h:M_HEADS + h + 1]
        m_prev = m_scr[h:h + 1, 0:1]
        c_prev = c_scr[h]
        n_prev = n_scr[h:h + 1, :]

        dmat = jnp.where(causal, b_col - b_row + li_row, -jnp.inf)
        inter = b_col + m_prev
        m_t = jnp.maximum(inter, jnp.max(dmat, axis=-1, keepdims=True))
        s = lax.dot_general(qb, kb, (((1,), (1,)), ((), ())),
                            preferred_element_type=F32) * jnp.exp(dmat - m_t)
        w_inter = jnp.exp(inter - m_t)
        num = (jnp.dot(s.astype(BF16), vb, preferred_element_type=F32)
               + w_inter * jnp.dot(qb, c_prev.astype(BF16), preferred_element_type=F32))
        den = (jnp.sum(s, axis=-1, keepdims=True)
               + w_inter * jnp.sum(qc * n_prev, axis=-1, keepdims=True))
        hh = num / jnp.maximum(jnp.abs(den), jnp.exp(-m_t))

        g = cum[t - 1:t, M_HEADS + h:M_HEADS + h + 1]
        a_col = g - b_col + li_col
        m_new = jnp.maximum(g + m_prev, jnp.max(a_col, axis=0, keepdims=True))
        ws = jnp.exp(a_col - m_new)
        decay = jnp.exp(g + m_prev - m_new)
        kw = ws * kc
        c_scr[h] = decay * c_prev + lax.dot_general(
            kw.astype(BF16), vb, (((0,), (0,)), ((), ())), preferred_element_type=F32)
        n_scr[h:h + 1, :] = decay * n_prev + jnp.sum(kw, axis=0, keepdims=True)
        m_scr[h:h + 1, :] = jnp.broadcast_to(m_new, (1, LANES))

        hn = _rms(hh, ng_ref[:, sl])
        y_ref[rows, sl] = (_sigmoid(o_ref[rows, sl].astype(F32)) * hn).astype(BF16)
        yield

    qe_scr[0:SUBLANES, :] = qe_scr[t:t + SUBLANES, :]
    ke_scr[0:SUBLANES, :] = ke_scr[t:t + SUBLANES, :]


def _mix_kernel(q_ref, k_ref, v_ref, o_ref, gt_ref, gb_ref, cw_ref, ng_ref,
                u_ref, wb_ref, wc_ref, are_ref, aim_ref, d_ref, wg_ref,
                ya_ref, yc_ref,
                c_scr, n_scr, m_scr, qe_scr, ke_scr,
                xre_scr, xim_scr, bu_scr, uil_scr, yil_scr,
                *, chunk, steps, batch, steps_per_seq):
    i = pl.program_id(0)
    ns = S5_SLAB_STATES
    rows = steps * batch
    n_chunks = rows // chunk
    width = M_HEADS * M_HEAD_DIM

    @pl.when(i == 0)
    def _():
        xre_scr[...] = jnp.zeros_like(xre_scr)
        xim_scr[...] = jnp.zeros_like(xim_scr)

    @pl.when(i % steps_per_seq == 0)
    def _():
        c_scr[...] = jnp.zeros_like(c_scr)
        n_scr[...] = jnp.zeros_like(n_scr)
        m_scr[...] = jnp.zeros_like(m_scr)
        qe_scr[0:SUBLANES, :] = jnp.zeros((SUBLANES, width), F32)
        ke_scr[0:SUBLANES, :] = jnp.zeros((SUBLANES, width), F32)

    def mlstm_heads():
        for j in range(n_chunks):
            yield from _mlstm_chunk(j * chunk, q_ref, k_ref, v_ref, o_ref, gt_ref, gb_ref,
                                    cw_ref, ng_ref, ya_ref, c_scr, n_scr, m_scr, qe_scr,
                                    ke_scr, t=chunk)

    heads = mlstm_heads()
    heads_per_slab = (n_chunks * M_HEADS) // S5_SLABS

    for b in range(batch):
        ub = u_ref[b].astype(F32)
        for c in range(S5_SLABS):
            uil_scr.at[c][pl.ds(b, steps, stride=batch), :] = ub[:, c * LANES:(c + 1) * LANES]

    ys = []
    for s in range(S5_SLABS):
        us = uil_scr[s]
        buf = bu_scr.at[s % 2]
        buf[...] = jnp.dot(us.astype(BF16), wb_ref[s], preferred_element_type=F32)
        a_re = jnp.broadcast_to(are_ref[s], (batch, ns))
        a_im = jnp.broadcast_to(aim_ref[s], (batch, ns))

        x_re = xre_scr[s]
        x_im = xim_scr[s]
        for tt in range(steps):
            if tt == steps // 2:
                for _ in range(heads_per_slab // 2):
                    next(heads, None)
            rs = slice(tt * batch, (tt + 1) * batch)
            n_re = a_re * x_re - a_im * x_im + buf[rs, 0:ns]
            n_im = a_re * x_im + a_im * x_re + buf[rs, ns:2 * ns]
            buf[rs, 0:ns] = n_re
            buf[rs, ns:2 * ns] = n_im
            x_re, x_im = n_re, n_im
        xre_scr[s] = x_re
        xim_scr[s] = x_im
        y = (jnp.dot(buf[...].astype(BF16), wc_ref[s], preferred_element_type=F32)
             + d_ref[:, s * LANES:(s + 1) * LANES] * us)
        ys.append(_gelu_tanh(y).astype(BF16))

        for _ in range(heads_per_slab - heads_per_slab // 2):
            next(heads, None)

    for _ in heads:
        pass

    yb = jnp.concatenate(ys, axis=1)
    half = BRANCH_WIDTH
    a = jnp.dot(yb, wg_ref[:, 0:half], preferred_element_type=F32)
    g = jnp.dot(yb, wg_ref[:, half:2 * half], preferred_element_type=F32)
    out = a * _sigmoid(g)
    for c in range(S5_SLABS):
        yil_scr[c] = out[:, c * LANES:(c + 1) * LANES]
    for b in range(batch):
        for c in range(S5_SLABS):
            yc_ref[b, :, c * LANES:(c + 1) * LANES] = (
                yil_scr.at[c][pl.ds(b, steps, stride=batch), :].astype(BF16))


def _mlstm_s5(proj, gates, gate_bias, conv_w, norm_g, wb, wc, a_re, a_im, d_skip, w_glu, *,
              layer, batch, seq, steps, chunk):
    rows = steps * batch
    chunk = min(chunk, rows)
    assert seq % rows == 0 and rows % chunk == 0
    ns = S5_SLAB_STATES
    width = M_HEADS * M_HEAD_DIM
    const2 = lambda i: (0, 0)
    const3 = lambda i: (0, 0, 0)

    def col_spec(cb):
        return pl.BlockSpec((rows, width), lambda i: (i, cb))

    return pl.pallas_call(
        functools.partial(_mix_kernel, chunk=chunk, steps=steps, batch=batch,
                          steps_per_seq=seq // rows),
        out_shape=(jax.ShapeDtypeStruct((batch * seq, width), BF16),
                   jax.ShapeDtypeStruct((batch, seq, BRANCH_WIDTH), BF16)),
        grid=(seq // steps,),
        in_specs=[col_spec(COL_QM), col_spec(COL_KM), col_spec(COL_VM), col_spec(COL_OM),
                  pl.BlockSpec((rows, N_GATE_PAD), lambda i: (i, 0)),
                  pl.BlockSpec((1, N_GATE_PAD), const2),
                  pl.BlockSpec((M_CONV, 2 * width), const2),
                  pl.BlockSpec((1, width), const2),
                  pl.BlockSpec((batch, steps, BRANCH_WIDTH), lambda i: (0, i, COL_US)),
                  pl.BlockSpec((S5_SLABS, LANES, 2 * ns), const3),
                  pl.BlockSpec((S5_SLABS, 2 * ns, LANES), const3),
                  pl.BlockSpec((S5_SLABS, 1, ns), const3),
                  pl.BlockSpec((S5_SLABS, 1, ns), const3),
                  pl.BlockSpec((1, BRANCH_WIDTH), const2),
                  pl.BlockSpec((None, BRANCH_WIDTH, 2 * BRANCH_WIDTH),
                               lambda i: (layer, 0, 0))],
        out_specs=(pl.BlockSpec((rows, width), lambda i: (i, 0)),
                   pl.BlockSpec((batch, steps, BRANCH_WIDTH), lambda i: (0, i, 0))),
        scratch_shapes=[pltpu.VMEM((M_HEADS, M_HEAD_DIM, M_HEAD_DIM), F32),
                        pltpu.VMEM((M_HEADS, M_HEAD_DIM), F32),
                        pltpu.VMEM((M_HEADS, LANES), F32),
                        pltpu.VMEM((chunk + SUBLANES, width), F32),
                        pltpu.VMEM((chunk + SUBLANES, width), F32),
                        pltpu.VMEM((S5_SLABS, batch, ns), F32),
                        pltpu.VMEM((S5_SLABS, batch, ns), F32),
                        pltpu.VMEM((2, rows, 2 * ns), F32),
                        pltpu.VMEM((S5_SLABS, rows, LANES), F32),
                        pltpu.VMEM((S5_SLABS, rows, LANES), F32)],
        compiler_params=_cparams(("arbitrary",)),
        name="mlstm_s5",
    )(proj, proj, proj, proj, gates, gate_bias, conv_w, norm_g,
      proj.reshape(batch, seq, N_MAIN), wb, wc, a_re, a_im, d_skip, w_glu)


def _attn_kernel(sc_ref, q_ref, k_ref, v_ref, bias_ref, ng_ref, y_ref,
                 m_scr, l_scr, acc_scr, vt_scr, s_scr, *, blk, heads, out_scale):
    t = blk
    d = DA_HEAD_DIM
    w = DA_V_DIM
    h0 = pl.program_id(1) * heads
    i = pl.program_id(2)
    lam = sc_ref[0]

    qs = (q_ref[...].astype(F32) * (d ** -0.5 * LOG2E)).astype(BF16)

    @pl.when(i == 0)
    def _():
        vt_scr[...] = v_ref[...].astype(F32).T.astype(BF16)

    def block_step(j, bias_tile=None, first=False):
        start = pl.multiple_of(j * t, t)
        kb = k_ref[pl.ds(start, t), :]
        vtb = vt_scr[:, pl.ds(start, t)]
        far = bias_tile is None
        for ci in range(2 * heads):
            cols = slice(ci * d, (ci + 1) * d)
            s_scr[ci] = lax.dot_general(kb[:, cols], qs[:, cols], (((1,), (1,)), ((), ())),
                                        preferred_element_type=F32)
        for hh in range(heads):
            bias = sc_ref[1 + h0 + hh] if far else bias_ref[hh, bias_tile]
            for c in range(2):
                ci = 2 * hh + c
                s = s_scr[ci]
                if far:
                    m_blk = jnp.max(s, axis=0, keepdims=True) + bias
                else:
                    s = s + bias
                    m_blk = jnp.max(s, axis=0, keepdims=True)
                m_new = m_blk if first else jnp.maximum(m_scr[ci], m_blk)
                p = jnp.exp2(s - ((m_new - bias) if far else m_new))
                l_blk = jnp.sum(p, axis=0, keepdims=True)
                pv = jnp.dot(vtb[hh * w:(hh + 1) * w, :], p.astype(BF16),
                             preferred_element_type=F32)
                if first:
                    l_scr[ci] = l_blk
                    acc_scr[ci] = pv
                else:
                    alpha = jnp.exp2(m_scr[ci] - m_new)
                    l_scr[ci] = alpha * l_scr[ci] + l_blk
                    acc_scr[ci] = alpha * acc_scr[ci] + pv
                m_scr[ci] = m_new

    block_step(i, 0, first=True)

    @pl.when(i >= 1)
    def _():
        block_step(i - 1, 1)

    def far_body(j, carry):
        block_step(j)
        return carry

    lax.fori_loop(0, jnp.maximum(i - 1, 0), far_body, 0)

    for hh in range(heads):
        ca, cb = 2 * hh, 2 * hh + 1
        out_t = (acc_scr[ca] * (1.0 / l_scr[ca])
                 - lam * (acc_scr[cb] * (1.0 / l_scr[cb])))
        hs = slice(hh * w, (hh + 1) * w)
        y_ref[:, hs] = (_rms(out_t.T, ng_ref[:, hs]) * out_scale).astype(BF16)


def _diff_attn(scalars, proj, bias_tiles, norm_g, *, batch, seq, blk, heads, out_scale):
    nq = seq // blk
    kvw = heads * DA_V_DIM
    per_row = BRANCH_WIDTH // kvw
    return pl.pallas_call(
        functools.partial(_attn_kernel, blk=blk, heads=heads, out_scale=out_scale),
        out_shape=jax.ShapeDtypeStruct((batch * seq, DA_HEADS * DA_V_DIM), BF16),
        grid=(batch, DA_HEADS // heads, nq),
        in_specs=[pl.BlockSpec(memory_space=pltpu.SMEM),
                  pl.BlockSpec((blk, kvw), lambda b, h, i: (b * nq + i, COL_QD * per_row + h)),
                  pl.BlockSpec((seq, kvw), lambda b, h, i: (b, COL_KD * per_row + h)),
                  pl.BlockSpec((seq, kvw), lambda b, h, i: (b, COL_VD * per_row + h)),
                  pl.BlockSpec((heads, 2, blk, blk), lambda b, h, i: (h, 0, 0, 0)),
                  pl.BlockSpec((1, kvw), lambda b, h, i: (0, h))],
        out_specs=pl.BlockSpec((blk, kvw), lambda b, h, i: (b * nq + i, h)),
        scratch_shapes=[pltpu.VMEM((2 * heads, 1, blk), F32),
                        pltpu.VMEM((2 * heads, 1, blk), F32),
                        pltpu.VMEM((2 * heads, DA_V_DIM, blk), F32),
                        pltpu.VMEM((kvw, seq), BF16),
                        pltpu.VMEM((2 * heads, blk, blk), F32)],
        compiler_params=_cparams(("parallel", "parallel", "arbitrary")),
        name="diff_attn",
    )(scalars, proj, proj, proj, bias_tiles, norm_g)


def _merge_kernel(ya_ref, yb_ref, yc_ref, g0_ref, g1_ref, g2_ref, x_ref, wbr_ref,
                  wo_ref, ng_ref, o_ref):
    merged = None
    for n, (y_ref, g_ref) in enumerate(((ya_ref, g0_ref), (yb_ref, g1_ref),
                                        (yc_ref, g2_ref))):
        z = jnp.dot(y_ref[...], wbr_ref[n], preferred_element_type=F32)
        term = _sigmoid(g_ref[...].astype(F32)) * z
        merged = term if merged is None else merged + term
    mix = jnp.dot(merged.astype(BF16), wo_ref[...], preferred_element_type=F32)
    o_ref[...] = x_ref[...] + _rms(mix, ng_ref[...])


def _merge(y_a, y_b, y_c, proj, x2d, w_branch, w_out, norm_g, *, layer, tm):
    m = x2d.shape[0]
    row = lambda i: (i, 0)

    def gate_spec(n):
        return pl.BlockSpec((tm, D_MODEL), lambda i: (i, COL_GATE // 2 + n))

    return pl.pallas_call(
        _merge_kernel,
        out_shape=jax.ShapeDtypeStruct((m, D_MODEL), F32),
        grid=(m // tm,),
        in_specs=[pl.BlockSpec((tm, BRANCH_WIDTH), row),
                  pl.BlockSpec((tm, BRANCH_WIDTH), row),
                  pl.BlockSpec((tm, BRANCH_WIDTH), row),
                  gate_spec(0), gate_spec(1), gate_spec(2),
                  pl.BlockSpec((tm, D_MODEL), row),
                  pl.BlockSpec((None, N_BRANCH, BRANCH_WIDTH, D_MODEL),
                               lambda i: (layer, 0, 0, 0), pipeline_mode=pl.Buffered(1)),
                  pl.BlockSpec((None, D_MODEL, D_MODEL), lambda i: (layer, 0, 0),
                               pipeline_mode=pl.Buffered(1)),
                  pl.BlockSpec((1, D_MODEL), lambda i: (0, 0))],
        out_specs=pl.BlockSpec((tm, D_MODEL), row),
        compiler_params=_cparams(("parallel",)),
        name="merge",
    )(y_a, y_b, y_c, proj, proj, proj, x2d, w_branch, w_out, norm_g)


def _ffn_kernel(x_ref, g_ref, wa_ref, wv_ref, cwa_ref, cwv_ref, cba_ref,
                cbv_ref, wd_ref, ng_ref, o_ref, h_scr, tail_scr, *, tl):
    i = pl.program_id(1)
    f = pl.program_id(2)
    nf = pl.num_programs(2)
    hal = SUBLANES

    @pl.when(f == 0)
    def _():
        h_scr[...] = _rms(x_ref[0], g_ref[...]).astype(BF16)
        o_ref[0] = jnp.zeros((tl, D_MODEL), F32)

    @pl.when(i == 0)
    def _():
        tail_scr[f] = jnp.zeros(tail_scr.shape[1:], F32)

    hb = h_scr[...]

    def conv(part, w_ref, cw_ref, cb_ref):
        up = jnp.dot(hb, w_ref[...], preferred_element_type=F32)
        ext = jnp.concatenate([tail_scr[f, part], up], axis=0)
        tail_scr[f, part] = up[tl - hal:tl, :]
        out = cb_ref[...] + cw_ref[FFN_CONV - 1:FFN_CONV, :] * up
        for j in range(FFN_CONV - 1):
            off = hal - (FFN_CONV - 1) + j
            out = out + cw_ref[j:j + 1, :] * ext[off:off + tl, :]
        return out

    a = conv(0, wa_ref, cwa_ref, cba_ref)
    v = conv(1, wv_ref, cwv_ref, cbv_ref)
    act = (_gelu_tanh(a) * v).astype(BF16)
    o_ref[0] += jnp.dot(act, wd_ref[...], preferred_element_type=F32)

    @pl.when(f == nf - 1)
    def _():
        o_ref[0] = x_ref[0] + _rms(o_ref[0], ng_ref[...])


def _ffn(x3d, gain, w_up, conv_w, conv_b, w_down, norm_g, *, layer, tl, tf):
    batch, seq, _ = x3d.shape
    nfb = D_FF // tf
    hal = SUBLANES
    return pl.pallas_call(
        functools.partial(_ffn_kernel, tl=tl),
        out_shape=jax.ShapeDtypeStruct((batch, seq, D_MODEL), F32),
        grid=(batch, seq // tl, nfb),
        in_specs=[pl.BlockSpec((1, tl, D_MODEL), lambda b, i, f: (b, i, 0)),
                  pl.BlockSpec((1, D_MODEL), lambda b, i, f: (0, 0)),
                  pl.BlockSpec((None, D_MODEL, tf), lambda b, i, f: (layer, 0, f)),
                  pl.BlockSpec((None, D_MODEL, tf), lambda b, i, f: (layer, 0, nfb + f)),
                  pl.BlockSpec((FFN_CONV, tf), lambda b, i, f: (0, f)),
                  pl.BlockSpec((FFN_CONV, tf), lambda b, i, f: (0, nfb + f)),
                  pl.BlockSpec((1, tf), lambda b, i, f: (0, f)),
                  pl.BlockSpec((1, tf), lambda b, i, f: (0, nfb + f)),
                  pl.BlockSpec((None, tf, D_MODEL), lambda b, i, f: (layer, f, 0)),
                  pl.BlockSpec((1, D_MODEL), lambda b, i, f: (0, 0))],
        out_specs=pl.BlockSpec((1, tl, D_MODEL), lambda b, i, f: (b, i, 0)),
        scratch_shapes=[pltpu.VMEM((tl, D_MODEL), BF16),
                        pltpu.VMEM((nfb, 2, hal, tf), F32)],
        compiler_params=_cparams(("parallel", "arbitrary", "arbitrary")),
        name="conv_ffn",
    )(x3d, gain, w_up, w_up, conv_w, conv_w, conv_b, conv_b, w_down, norm_g)


def _t5_bucket(dist):
    n = jnp.maximum(dist, 0)
    max_exact = REL_BUCKETS // 2
    nf = jnp.maximum(n, 1).astype(F32)
    large = max_exact + (jnp.log(nf / max_exact) / math.log(REL_MAX_DIST / max_exact)
                         * (REL_BUCKETS - max_exact)).astype(jnp.int32)
    large = jnp.minimum(large, REL_BUCKETS - 1)
    return jnp.where(n < max_exact, n, large)


def _toeplitz(v, t):
    h = v.shape[0]
    period = 2 * t + 1
    v = jnp.pad(v, ((0, 0), (0, 1)))
    rows = jnp.tile(v, (1, t))[:, :t * (period - 1)].reshape(h, t, period - 1)
    return rows[:, :, t:]


def _attn_bias_tiles(rel_bias, blk):
    dist = jnp.arange(-blk, 2 * blk, dtype=jnp.int32)
    table = rel_bias.astype(F32) * LOG2E
    bucket = _t5_bucket(dist)
    bias = jnp.broadcast_to(table[REL_BUCKETS - 1][:, None], (DA_HEADS, 3 * blk))
    for b in range(REL_BUCKETS - 1):
        bias = jnp.where(bucket[None, :] == b, table[b][:, None], bias)
    bias = jnp.where(dist[None, :] >= 0, bias, NEG_BIG)
    return jnp.stack([_toeplitz(bias[:, 0:2 * blk], blk),
                      _toeplitz(bias[:, blk:3 * blk], blk)], axis=1)


def _s5_params(lam_re, lam_im, log_dt, b_re, b_im, c_re, c_im):
    dt = jnp.exp(log_dt)[:, None]
    mag = jnp.exp(lam_re * dt)
    a_re = mag * jnp.cos(lam_im * dt)
    a_im = mag * jnp.sin(lam_im * dt)
    den = lam_re * lam_re + lam_im * lam_im
    z_re = ((a_re - 1.0) * lam_re + a_im * lam_im) / den
    z_im = (a_im * lam_re - (a_re - 1.0) * lam_im) / den
    bb_re = z_re[..., None] * b_re - z_im[..., None] * b_im
    bb_im = z_re[..., None] * b_im + z_im[..., None] * b_re
    gs = S5_GROUPS // S5_SLABS
    eye = jnp.eye(gs, dtype=F32)

    def in_blocks(bb):
        bb = bb.reshape(S5_SLABS, gs, S5_STATE, S5_GROUP)
        w = jnp.einsum('sgpc,gh->sgchp', bb, eye)
        return w.reshape(S5_SLABS, gs * S5_GROUP, gs * S5_STATE)

    def out_blocks(cc):
        cc = cc.reshape(S5_SLABS, gs, S5_GROUP, S5_STATE)
        w = jnp.einsum('sgcp,gh->sgphc', cc, eye)
        return w.reshape(S5_SLABS, gs * S5_STATE, gs * S5_GROUP)

    wb = jnp.concatenate([in_blocks(bb_re), in_blocks(bb_im)], axis=-1).astype(BF16)
    wc = jnp.concatenate([out_blocks(c_re), out_blocks(-c_im)], axis=-2).astype(BF16)
    a_re = a_re.reshape(S5_SLABS, 1, S5_SLAB_STATES)
    a_im = a_im.reshape(S5_SLABS, 1, S5_SLAB_STATES)
    return wb, wc, a_re, a_im


IN_PROJ_TM, IN_PROJ_TN = 1024, 2048
MLSTM_CHUNK = 128
ATTN_BLOCK = 512
ATTN_HEADS_PER_STEP = 2
S5_STEPS = 32
MERGE_TM = 256
FFN_TL, FFN_TF = 512, 512


def _layer(x2d, batch, seq, layer, p):
    n_if = 2 * M_HEADS
    proj, gates = _in_proj(x2d, p['norm_mix_pre'][layer][None, :], p['w_gate'],
                           p['w_main'], layer=layer, tm=min(IN_PROJ_TM, batch * seq),
                           tn=IN_PROJ_TN)

    gate_bias = jnp.pad(p['mlstm_b_if'][layer].reshape(1, n_if),
                        ((0, 0), (0, N_GATE_PAD - n_if)))
    wb, wc, a_re, a_im = _s5_params(
        p['s5_lambda_re'][layer], p['s5_lambda_im'][layer], p['s5_log_dt'][layer],
        p['s5_b_re'][layer], p['s5_b_im'][layer], p['s5_c_re'][layer], p['s5_c_im'][layer])
    y_a, y_c = _mlstm_s5(proj, gates, gate_bias, p['mlstm_conv'][layer],
                         p['mlstm_norm'][layer][None, :], wb, wc, a_re, a_im,
                         p['s5_d'][layer][None, :], p['w_glu'], layer=layer, batch=batch,
                         seq=seq, steps=min(S5_STEPS, seq), chunk=MLSTM_CHUNK)
    y_c = y_c.reshape(batch * seq, BRANCH_WIDTH)

    lambda_init = 0.8 - 0.6 * math.exp(-0.3 * layer)
    lam = p['diff_lambda'][layer]
    lam_full = (jnp.exp(jnp.sum(lam[0] * lam[1])) - jnp.exp(jnp.sum(lam[2] * lam[3]))
                + lambda_init)
    blk = min(ATTN_BLOCK, seq)
    scalars = jnp.concatenate(
        [lam_full[None], p['rel_bias'][REL_BUCKETS - 1, :] * LOG2E]).astype(F32)
    y_b = _diff_attn(scalars, proj, p['attn_bias_tiles'],
                     p['diff_norm'][layer][None, :], batch=batch, seq=seq, blk=blk,
                     heads=ATTN_HEADS_PER_STEP, out_scale=1.0 - lambda_init)

    x2d = _merge(y_a, y_b, y_c, proj, x2d, p['w_branch'], p['w_out'],
                 p['norm_mix_post'][layer][None, :], layer=layer, tm=MERGE_TM)

    x3d = _ffn(x2d.reshape(batch, seq, D_MODEL), p['norm_ffn_pre'][layer][None, :],
               p['w_up'], p['ffn_conv'][layer], p['ffn_conv_b'][layer][None, :],
               p['w_down'], p['norm_ffn_post'][layer][None, :], layer=layer,
               tl=min(FFN_TL, seq), tf=FFN_TF)
    return x3d.reshape(batch * seq, D_MODEL)


def kernel(x, norm_mix_pre, norm_mix_post, norm_ffn_pre, norm_ffn_post, w_in, mlstm_b_if, mlstm_conv, mlstm_norm, diff_lambda, diff_norm, rel_bias, s5_lambda_re, s5_lambda_im, s5_log_dt, s5_b_re, s5_b_im, s5_c_re, s5_c_im, s5_d, s5_w_glu, w_branch, w_out, w_up, ffn_conv, ffn_conv_b, w_down):
    n_if = 2 * M_HEADS
    split = 4 * M_HEADS * M_HEAD_DIM
    w_main = jnp.concatenate([w_in[:, :, :split], w_in[:, :, split + n_if:]],
                             axis=2).astype(BF16)
    w_gate = jnp.pad(w_in[:, :, split:split + n_if],
                     ((0, 0), (0, 0), (0, N_GATE_PAD - n_if))).astype(BF16)
    p = dict(norm_mix_pre=norm_mix_pre, norm_mix_post=norm_mix_post,
             norm_ffn_pre=norm_ffn_pre, norm_ffn_post=norm_ffn_post,
             w_main=w_main, w_gate=w_gate,
             mlstm_b_if=mlstm_b_if, mlstm_conv=mlstm_conv, mlstm_norm=mlstm_norm,
             diff_lambda=diff_lambda, diff_norm=diff_norm, rel_bias=rel_bias,
             s5_lambda_re=s5_lambda_re, s5_lambda_im=s5_lambda_im, s5_log_dt=s5_log_dt,
             s5_b_re=s5_b_re, s5_b_im=s5_b_im, s5_c_re=s5_c_re, s5_c_im=s5_c_im,
             s5_d=s5_d, w_glu=s5_w_glu.astype(BF16), w_branch=w_branch.astype(BF16),
             w_out=w_out.astype(BF16), w_up=w_up.astype(BF16),
             ffn_conv=ffn_conv, ffn_conv_b=ffn_conv_b, w_down=w_down.astype(BF16))
    batch, seq, _ = x.shape
    p['attn_bias_tiles'] = _attn_bias_tiles(rel_bias, min(ATTN_BLOCK, seq))
    x2d = x.reshape(batch * seq, D_MODEL)
    for layer in range(DEPTH):
        x2d = _layer(x2d, batch, seq, layer, p)
    return x2d.reshape(batch, seq, D_MODEL)
```

```python
import functools
import math

import jax
import jax.numpy as jnp
from jax import lax
from jax.experimental import pallas as pl
from jax.experimental.pallas import tpu as pltpu

F32 = jnp.float32
BF16 = jnp.bfloat16
HIGHEST = lax.Precision.HIGHEST

D_MODEL = 2048
DEPTH = 2
BRANCH_WIDTH = 1024
N_BRANCH = 3
M_HEADS = 4
M_HEAD_DIM = 256
M_CONV = 4
DA_HEADS = 4
DA_HEAD_DIM = 128
DA_V_DIM = 256
REL_BUCKETS = 32
REL_MAX_DIST = 128
S5_GROUP = 16
S5_GROUPS = 64
S5_STATE = 64
D_FF = 5632
FFN_CONV = 3
EPS = 1e-6

LANES = 128
SUBLANES = 8
VMEM_LIMIT = 56 * 1024 * 1024

COL_QM, COL_KM, COL_VM, COL_OM = 0, 1, 2, 3
COL_QD, COL_KD, COL_VD, COL_US = 4, 5, 6, 7
COL_GATE = 8
N_MAIN = COL_GATE * BRANCH_WIDTH + N_BRANCH * D_MODEL
N_GATE_PAD = LANES

S5_SLABS = 8
S5_SLAB_STATES = (S5_GROUPS // S5_SLABS) * S5_STATE

NEG_BIG = -1e30
LOG2E = math.log2(math.e)


def _cparams(sem):
    return pltpu.CompilerParams(dimension_semantics=sem, vmem_limit_bytes=VMEM_LIMIT)


def _sigmoid(x):
    return 0.5 * jnp.tanh(0.5 * x) + 0.5


def _gelu_tanh(x):
    c = math.sqrt(2.0 / math.pi)
    return 0.5 * x * (1.0 + jnp.tanh(c * (x + 0.044715 * (x * x * x))))


def _rms(x, gain):
    var = jnp.mean(x * x, axis=-1, keepdims=True)
    return x * lax.rsqrt(var + EPS) * gain


def _in_proj_kernel(x_ref, g_ref, wg_ref, w_ref, o_ref, og_ref, h_scr):
    @pl.when(pl.program_id(1) == 0)
    def _():
        h_scr[...] = _rms(x_ref[...], g_ref[...]).astype(BF16)
        og_ref[...] = jnp.dot(h_scr[...], wg_ref[...], preferred_element_type=F32)

    o_ref[...] = jnp.dot(h_scr[...], w_ref[...],
                         preferred_element_type=F32).astype(BF16)


def _in_proj(x2d, gain, w_gate, w_main, *, layer, tm, tn):
    m = x2d.shape[0]
    return pl.pallas_call(
        _in_proj_kernel,
        out_shape=(jax.ShapeDtypeStruct((m, N_MAIN), BF16),
                   jax.ShapeDtypeStruct((m, N_GATE_PAD), F32)),
        grid=(m // tm, N_MAIN // tn),
        in_specs=[pl.BlockSpec((tm, D_MODEL), lambda i, n: (i, 0)),
                  pl.BlockSpec((1, D_MODEL), lambda i, n: (0, 0)),
                  pl.BlockSpec((None, D_MODEL, N_GATE_PAD), lambda i, n: (layer, 0, 0)),
                  pl.BlockSpec((None, D_MODEL, tn), lambda i, n: (layer, 0, n))],
        out_specs=(pl.BlockSpec((tm, tn), lambda i, n: (i, n)),
                   pl.BlockSpec((tm, N_GATE_PAD), lambda i, n: (i, 0))),
        scratch_shapes=[pltpu.VMEM((tm, D_MODEL), BF16)],
        compiler_params=_cparams(("parallel", "arbitrary")),
        name="in_proj",
    )(x2d, gain, w_gate, w_main)


def _mlstm_chunk(r0, q_ref, k_ref, v_ref, o_ref, gt_ref, gb_ref, cw_ref, sh_ref, ng_ref,
                 y_ref, c_scr, n_scr, m_scr, qe_scr, ke_scr, *, t):
    hd = M_HEAD_DIM
    width = M_HEADS * hd
    rows = slice(r0, r0 + t)

    qe_scr[t:2 * t, :] = q_ref[rows, :]
    ke_scr[t:2 * t, :] = k_ref[rows, :]
    q_sh = jnp.dot(sh_ref[...], qe_scr[...], preferred_element_type=F32)
    k_sh = jnp.dot(sh_ref[...], ke_scr[...], preferred_element_type=F32)

    gates = gt_ref[rows, :] + gb_ref[...]
    log_f = jnp.minimum(gates, 0.0) - jnp.log1p(jnp.exp(-jnp.abs(gates)))
    row = lax.broadcasted_iota(jnp.int32, (t, t), 0)
    col = lax.broadcasted_iota(jnp.int32, (t, t), 1)
    causal = col <= row
    cum = jnp.dot(causal.astype(F32), log_f, preferred_element_type=F32,
                  precision=HIGHEST)
    gates_t = gates.T
    cum_t = cum.T

    for h in range(M_HEADS):
        sl = slice(h * hd, (h + 1) * hd)
        ksl = slice(width + h * hd, width + (h + 1) * hd)
        last = M_CONV - 1
        qc = cw_ref[last:M_CONV, sl] * q_ref[rows, sl].astype(F32)
        kc = cw_ref[last:M_CONV, ksl] * k_ref[rows, sl].astype(F32)
        for j in range(last):
            qc = qc + cw_ref[j:j + 1, sl] * q_sh[j * t:(j + 1) * t, sl]
            kc = kc + cw_ref[j:j + 1, ksl] * k_sh[j * t:(j + 1) * t, sl]
        qc = qc * _sigmoid(qc)
        kc = kc * _sigmoid(kc) * (hd ** -0.5)
        qb = qc.astype(BF16)
        kb = kc.astype(BF16)
        vb = v_ref[rows, sl]

        li_row = gates_t[h:h + 1, :]
        b_row = cum_t[M_HEADS + h:M_HEADS + h + 1, :]
        li_col = gates[:, h:h + 1]
        b_col = cum[:, M_HEADS + h:M_HEADS + h + 1]
        m_prev = m_scr[h:h + 1, 0:1]
        c_prev = c_scr[h]
        n_prev = n_scr[h:h + 1, :]

        dmat = jnp.where(causal, b_col - b_row + li_row, -jnp.inf)
        inter = b_col + m_prev
        m_t = jnp.maximum(inter, jnp.max(dmat, axis=-1, keepdims=True))
        s = lax.dot_general(qb, kb, (((1,), (1,)), ((), ())),
                            preferred_element_type=F32) * jnp.exp(dmat - m_t)
        w_inter = jnp.exp(inter - m_t)
        num = (jnp.dot(s.astype(BF16), vb, preferred_element_type=F32)
               + w_inter * jnp.dot(qb, c_prev.astype(BF16), preferred_element_type=F32))
        den = (jnp.sum(s, axis=-1, keepdims=True)
               + w_inter * jnp.sum(qc * n_prev, axis=-1, keepdims=True))
        hh = num / jnp.maximum(jnp.abs(den), jnp.exp(-m_t))

        g = cum[t - 1:t, M_HEADS + h:M_HEADS + h + 1]
        a_col = g - b_col + li_col
        m_new = jnp.maximum(g + m_prev, jnp.max(a_col, axis=0, keepdims=True))
        ws = jnp.exp(a_col - m_new)
        decay = jnp.exp(g + m_prev - m_new)
        kw = ws * kc
        c_scr[h] = decay * c_prev + lax.dot_general(
            kw.astype(BF16), vb, (((0,), (0,)), ((), ())), preferred_element_type=F32)
        n_scr[h:h + 1, :] = decay * n_prev + jnp.sum(kw, axis=0, keepdims=True)
        m_scr[h:h + 1, :] = jnp.broadcast_to(m_new, (1, LANES))

        hn = _rms(hh, ng_ref[:, sl])
        y_ref[rows, sl] = (_sigmoid(o_ref[rows, sl].astype(F32)) * hn).astype(BF16)
        yield

    qe_scr[0:t, :] = q_ref[rows, :]
    ke_scr[0:t, :] = k_ref[rows, :]


def _mix_kernel(q_ref, k_ref, v_ref, o_ref, gt_ref, gb_ref, cw_ref, sh_ref, ng_ref,
                u_ref, wb_ref, wc_ref, are_ref, aim_ref, d_ref, wg_ref,
                ya_ref, yc_ref,
                c_scr, n_scr, m_scr, qe_scr, ke_scr,
                xre_scr, xim_scr, bu_scr, uil_scr, yil_scr,
                *, chunk, steps, batch, steps_per_seq):
    i = pl.program_id(0)
    ns = S5_SLAB_STATES
    rows = steps * batch
    n_chunks = rows // chunk
    width = M_HEADS * M_HEAD_DIM

    @pl.when(i == 0)
    def _():
        xre_scr[...] = jnp.zeros_like(xre_scr)
        xim_scr[...] = jnp.zeros_like(xim_scr)

    @pl.when(i % steps_per_seq == 0)
    def _():
        c_scr[...] = jnp.zeros_like(c_scr)
        n_scr[...] = jnp.zeros_like(n_scr)
        m_scr[...] = jnp.zeros_like(m_scr)
        qe_scr[0:chunk, :] = jnp.zeros((chunk, width), BF16)
        ke_scr[0:chunk, :] = jnp.zeros((chunk, width), BF16)

    def mlstm_heads():
        for j in range(n_chunks):
            yield from _mlstm_chunk(j * chunk, q_ref, k_ref, v_ref, o_ref, gt_ref, gb_ref,
                                    cw_ref, sh_ref, ng_ref, ya_ref, c_scr, n_scr, m_scr,
                                    qe_scr, ke_scr, t=chunk)

    heads = mlstm_heads()
    heads_per_slab = (n_chunks * M_HEADS) // S5_SLABS

    for b in range(batch):
        ub = u_ref[b].astype(F32)
        for c in range(S5_SLABS):
            uil_scr.at[c][pl.ds(b, steps, stride=batch), :] = ub[:, c * LANES:(c + 1) * LANES]

    ys = []
    for s in range(S5_SLABS):
        us = uil_scr[s]
        buf = bu_scr.at[s % 2]
        buf[...] = jnp.dot(us.astype(BF16), wb_ref[s], preferred_element_type=F32)
        a_re = jnp.broadcast_to(are_ref[s], (batch, ns))
        a_im = jnp.broadcast_to(aim_ref[s], (batch, ns))

        x_re = xre_scr[s]
        x_im = xim_scr[s]
        for tt in range(steps):
            if tt == steps // 2:
                for _ in range(heads_per_slab // 2):
                    next(heads, None)
            rs = slice(tt * batch, (tt + 1) * batch)
            n_re = a_re * x_re - a_im * x_im + buf[rs, 0:ns]
            n_im = a_re * x_im + a_im * x_re + buf[rs, ns:2 * ns]
            buf[rs, 0:ns] = n_re
            buf[rs, ns:2 * ns] = n_im
            x_re, x_im = n_re, n_im
        xre_scr[s] = x_re
        xim_scr[s] = x_im
        y = (jnp.dot(buf[...].astype(BF16), wc_ref[s], preferred_element_type=F32)
             + d_ref[:, s * LANES:(s + 1) * LANES] * us)
        ys.append(_gelu_tanh(y).astype(BF16))

        for _ in range(heads_per_slab - heads_per_slab // 2):
            next(heads, None)

    for _ in heads:
        pass

    yb = jnp.concatenate(ys, axis=1)
    half = BRANCH_WIDTH
    a = jnp.dot(yb, wg_ref[:, 0:half], preferred_element_type=F32)
    g = jnp.dot(yb, wg_ref[:, half:2 * half], preferred_element_type=F32)
    out = a * _sigmoid(g)
    for c in range(S5_SLABS):
        yil_scr[c] = out[:, c * LANES:(c + 1) * LANES]
    for b in range(batch):
        for c in range(S5_SLABS):
            yc_ref[b, :, c * LANES:(c + 1) * LANES] = (
                yil_scr.at[c][pl.ds(b, steps, stride=batch), :].astype(BF16))


def _mlstm_s5(proj, gates, gate_bias, conv_w, norm_g, wb, wc, a_re, a_im, d_skip, w_glu, *,
              layer, batch, seq, steps, chunk):
    rows = steps * batch
    chunk = min(chunk, rows)
    assert seq % rows == 0 and rows % chunk == 0
    ns = S5_SLAB_STATES
    width = M_HEADS * M_HEAD_DIM
    const2 = lambda i: (0, 0)
    const3 = lambda i: (0, 0, 0)

    def col_spec(cb):
        return pl.BlockSpec((rows, width), lambda i: (i, cb))

    r = jnp.arange((M_CONV - 1) * chunk, dtype=jnp.int32)[:, None]
    c = jnp.arange(2 * chunk, dtype=jnp.int32)[None, :]
    shift = (c == chunk + r % chunk - (M_CONV - 1) + r // chunk).astype(BF16)

    return pl.pallas_call(
        functools.partial(_mix_kernel, chunk=chunk, steps=steps, batch=batch,
                          steps_per_seq=seq // rows),
        out_shape=(jax.ShapeDtypeStruct((batch * seq, width), BF16),
                   jax.ShapeDtypeStruct((batch, seq, BRANCH_WIDTH), BF16)),
        grid=(seq // steps,),
        in_specs=[col_spec(COL_QM), col_spec(COL_KM), col_spec(COL_VM), col_spec(COL_OM),
                  pl.BlockSpec((rows, N_GATE_PAD), lambda i: (i, 0)),
                  pl.BlockSpec((1, N_GATE_PAD), const2),
                  pl.BlockSpec((M_CONV, 2 * width), const2),
                  pl.BlockSpec(((M_CONV - 1) * chunk, 2 * chunk), const2),
                  pl.BlockSpec((1, width), const2),
                  pl.BlockSpec((batch, steps, BRANCH_WIDTH), lambda i: (0, i, COL_US)),
                  pl.BlockSpec((S5_SLABS, LANES, 2 * ns), const3),
                  pl.BlockSpec((S5_SLABS, 2 * ns, LANES), const3),
                  pl.BlockSpec((S5_SLABS, 1, ns), const3),
                  pl.BlockSpec((S5_SLABS, 1, ns), const3),
                  pl.BlockSpec((1, BRANCH_WIDTH), const2),
                  pl.BlockSpec((None, BRANCH_WIDTH, 2 * BRANCH_WIDTH),
                               lambda i: (layer, 0, 0))],
        out_specs=(pl.BlockSpec((rows, width), lambda i: (i, 0)),
                   pl.BlockSpec((batch, steps, BRANCH_WIDTH), lambda i: (0, i, 0))),
        scratch_shapes=[pltpu.VMEM((M_HEADS, M_HEAD_DIM, M_HEAD_DIM), F32),
                        pltpu.VMEM((M_HEADS, M_HEAD_DIM), F32),
                        pltpu.VMEM((M_HEADS, LANES), F32),
                        pltpu.VMEM((2 * chunk, width), BF16),
                        pltpu.VMEM((2 * chunk, width), BF16),
                        pltpu.VMEM((S5_SLABS, batch, ns), F32),
                        pltpu.VMEM((S5_SLABS, batch, ns), F32),
                        pltpu.VMEM((2, rows, 2 * ns), F32),
                        pltpu.VMEM((S5_SLABS, rows, LANES), F32),
                        pltpu.VMEM((S5_SLABS, rows, LANES), F32)],
        compiler_params=_cparams(("arbitrary",)),
        name="mlstm_s5",
    )(proj, proj, proj, proj, gates, gate_bias, conv_w, shift, norm_g,
      proj.reshape(batch, seq, N_MAIN), wb, wc, a_re, a_im, d_skip, w_glu)


def _attn_kernel(sc_ref, q_ref, k_ref, v_ref, bias_ref, ng_ref, y_ref,
                 m_scr, l_scr, acc_scr, vt_scr, s_scr, *, blk, heads, out_scale):
    t = blk
    d = DA_HEAD_DIM
    w = DA_V_DIM
    h0 = pl.program_id(1) * heads
    i = pl.program_id(2)
    lam = sc_ref[0]

    qs = (q_ref[...].astype(F32) * (d ** -0.5 * LOG2E)).astype(BF16)

    @pl.when(i == 0)
    def _():
        vt_scr[...] = v_ref[...].astype(F32).T.astype(BF16)

    def block_step(j, bias_tile=None, first=False):
        start = pl.multiple_of(j * t, t)
        kb = k_ref[pl.ds(start, t), :]
        vtb = vt_scr[:, pl.ds(start, t)]
        far = bias_tile is None
        for ci in range(2 * heads):
            cols = slice(ci * d, (ci + 1) * d)
            s_scr[ci] = lax.dot_general(kb[:, cols], qs[:, cols], (((1,), (1,)), ((), ())),
                                        preferred_element_type=F32)
        for hh in range(heads):
            bias = sc_ref[1 + h0 + hh] if far else bias_ref[hh, bias_tile]
            for c in range(2):
                ci = 2 * hh + c
                s = s_scr[ci]
                if far:
                    m_blk = jnp.max(s, axis=0, keepdims=True) + bias
                else:
                    s = s + bias
                    m_blk = jnp.max(s, axis=0, keepdims=True)
                m_new = m_blk if first else jnp.maximum(m_scr[ci], m_blk)
                p = jnp.exp2(s - ((m_new - bias) if far else m_new))
                l_blk = jnp.sum(p, axis=0, keepdims=True)
                pv = jnp.dot(vtb[hh * w:(hh + 1) * w, :], p.astype(BF16),
                             preferred_element_type=F32)
                if first:
                    l_scr[ci] = l_blk
                    acc_scr[ci] = pv
                else:
                    alpha = jnp.exp2(m_scr[ci] - m_new)
                    l_scr[ci] = alpha * l_scr[ci] + l_blk
                    acc_scr[ci] = alpha * acc_scr[ci] + pv
                m_scr[ci] = m_new

    block_step(i, 0, first=True)

    @pl.when(i >= 1)
    def _():
        block_step(i - 1, 1)

    def far_body(j, carry):
        block_step(j)
        return carry

    lax.fori_loop(0, jnp.maximum(i - 1, 0), far_body, 0)

    for hh in range(heads):
        ca, cb = 2 * hh, 2 * hh + 1
        out_t = (acc_scr[ca] * (1.0 / l_scr[ca])
                 - lam * (acc_scr[cb] * (1.0 / l_scr[cb])))
        hs = slice(hh * w, (hh + 1) * w)
        y_ref[:, hs] = (_rms(out_t.T, ng_ref[:, hs]) * out_scale).astype(BF16)


def _diff_attn(scalars, proj, bias_tiles, norm_g, *, batch, seq, blk, heads, out_scale):
    nq = seq // blk
    kvw = heads * DA_V_DIM
    per_row = BRANCH_WIDTH // kvw
    return pl.pallas_call(
        functools.partial(_attn_kernel, blk=blk, heads=heads, out_scale=out_scale),
        out_shape=jax.ShapeDtypeStruct((batch * seq, DA_HEADS * DA_V_DIM), BF16),
        grid=(batch, DA_HEADS // heads, nq),
        in_specs=[pl.BlockSpec(memory_space=pltpu.SMEM),
                  pl.BlockSpec((blk, kvw), lambda b, h, i: (b * nq + i, COL_QD * per_row + h)),
                  pl.BlockSpec((seq, kvw), lambda b, h, i: (b, COL_KD * per_row + h)),
                  pl.BlockSpec((seq, kvw), lambda b, h, i: (b, COL_VD * per_row + h)),
                  pl.BlockSpec((heads, 2, blk, blk), lambda b, h, i: (h, 0, 0, 0)),
                  pl.BlockSpec((1, kvw), lambda b, h, i: (0, h))],
        out_specs=pl.BlockSpec((blk, kvw), lambda b, h, i: (b * nq + i, h)),
        scratch_shapes=[pltpu.VMEM((2 * heads, 1, blk), F32),
                        pltpu.VMEM((2 * heads, 1, blk), F32),
                        pltpu.VMEM((2 * heads, DA_V_DIM, blk), F32),
                        pltpu.VMEM((kvw, seq), BF16),
                        pltpu.VMEM((2 * heads, blk, blk), F32)],
        compiler_params=_cparams(("parallel", "parallel", "arbitrary")),
        name="diff_attn",
    )(scalars, proj, proj, proj, bias_tiles, norm_g)


def _merge_kernel(ya_ref, yb_ref, yc_ref, g0_ref, g1_ref, g2_ref, x_ref, wbr_ref,
                  wo_ref, ng_ref, o_ref):
    merged = None
    for n, (y_ref, g_ref) in enumerate(((ya_ref, g0_ref), (yb_ref, g1_ref),
                                        (yc_ref, g2_ref))):
        z = jnp.dot(y_ref[...], wbr_ref[n], preferred_element_type=F32)
        term = _sigmoid(g_ref[...].astype(F32)) * z
        merged = term if merged is None else merged + term
    mix = jnp.dot(merged.astype(BF16), wo_ref[...], preferred_element_type=F32)
    o_ref[...] = x_ref[...] + _rms(mix, ng_ref[...])


def _merge(y_a, y_b, y_c, proj, x2d, w_branch, w_out, norm_g, *, layer, tm):
    m = x2d.shape[0]
    row = lambda i: (i, 0)

    def gate_spec(n):
        return pl.BlockSpec((tm, D_MODEL), lambda i: (i, COL_GATE // 2 + n))

    return pl.pallas_call(
        _merge_kernel,
        out_shape=jax.ShapeDtypeStruct((m, D_MODEL), F32),
        grid=(m // tm,),
        in_specs=[pl.BlockSpec((tm, BRANCH_WIDTH), row),
                  pl.BlockSpec((tm, BRANCH_WIDTH), row),
                  pl.BlockSpec((tm, BRANCH_WIDTH), row),
                  gate_spec(0), gate_spec(1), gate_spec(2),
                  pl.BlockSpec((tm, D_MODEL), row),
                  pl.BlockSpec((None, N_BRANCH, BRANCH_WIDTH, D_MODEL),
                               lambda i: (layer, 0, 0, 0), pipeline_mode=pl.Buffered(1)),
                  pl.BlockSpec((None, D_MODEL, D_MODEL), lambda i: (layer, 0, 0),
                               pipeline_mode=pl.Buffered(1)),
                  pl.BlockSpec((1, D_MODEL), lambda i: (0, 0))],
        out_specs=pl.BlockSpec((tm, D_MODEL), row),
        compiler_params=_cparams(("parallel",)),
        name="merge",
    )(y_a, y_b, y_c, proj, proj, proj, x2d, w_branch, w_out, norm_g)


def _ffn_kernel(x_ref, g_ref, wa_ref, wv_ref, cwa_ref, cwv_ref, cba_ref,
                cbv_ref, wd_ref, ng_ref, o_ref, h_scr, tail_scr, *, tl):
    i = pl.program_id(1)
    f = pl.program_id(2)
    nf = pl.num_programs(2)
    hal = SUBLANES

    @pl.when(f == 0)
    def _():
        h_scr[...] = _rms(x_ref[0], g_ref[...]).astype(BF16)
        o_ref[0] = jnp.zeros((tl, D_MODEL), F32)

    @pl.when(i == 0)
    def _():
        tail_scr[f] = jnp.zeros(tail_scr.shape[1:], F32)

    hb = h_scr[...]

    def conv(part, w_ref, cw_ref, cb_ref):
        up = jnp.dot(hb, w_ref[...], preferred_element_type=F32)
        ext = jnp.concatenate([tail_scr[f, part], up], axis=0)
        tail_scr[f, part] = up[tl - hal:tl, :]
        out = cb_ref[...] + cw_ref[FFN_CONV - 1:FFN_CONV, :] * up
        for j in range(FFN_CONV - 1):
            off = hal - (FFN_CONV - 1) + j
            out = out + cw_ref[j:j + 1, :] * ext[off:off + tl, :]
        return out

    a = conv(0, wa_ref, cwa_ref, cba_ref)
    v = conv(1, wv_ref, cwv_ref, cbv_ref)
    act = (_gelu_tanh(a) * v).astype(BF16)
    o_ref[0] += jnp.dot(act, wd_ref[...], preferred_element_type=F32)

    @pl.when(f == nf - 1)
    def _():
        o_ref[0] = x_ref[0] + _rms(o_ref[0], ng_ref[...])


def _ffn(x3d, gain, w_up, conv_w, conv_b, w_down, norm_g, *, layer, tl, tf):
    batch, seq, _ = x3d.shape
    nfb = D_FF // tf
    hal = SUBLANES
    return pl.pallas_call(
        functools.partial(_ffn_kernel, tl=tl),
        out_shape=jax.ShapeDtypeStruct((batch, seq, D_MODEL), F32),
        grid=(batch, seq // tl, nfb),
        in_specs=[pl.BlockSpec((1, tl, D_MODEL), lambda b, i, f: (b, i, 0)),
                  pl.BlockSpec((1, D_MODEL), lambda b, i, f: (0, 0)),
                  pl.BlockSpec((None, D_MODEL, tf), lambda b, i, f: (layer, 0, f)),
                  pl.BlockSpec((None, D_MODEL, tf), lambda b, i, f: (layer, 0, nfb + f)),
                  pl.BlockSpec((FFN_CONV, tf), lambda b, i, f: (0, f)),
                  pl.BlockSpec((FFN_CONV, tf), lambda b, i, f: (0, nfb + f)),
                  pl.BlockSpec((1, tf), lambda b, i, f: (0, f)),
                  pl.BlockSpec((1, tf), lambda b, i, f: (0, nfb + f)),
                  pl.BlockSpec((None, tf, D_MODEL), lambda b, i, f: (layer, f, 0)),
                  pl.BlockSpec((1, D_MODEL), lambda b, i, f: (0, 0))],
        out_specs=pl.BlockSpec((1, tl, D_MODEL), lambda b, i, f: (b, i, 0)),
        scratch_shapes=[pltpu.VMEM((tl, D_MODEL), BF16),
                        pltpu.VMEM((nfb, 2, hal, tf), F32)],
        compiler_params=_cparams(("parallel", "arbitrary", "arbitrary")),
        name="conv_ffn",
    )(x3d, gain, w_up, w_up, conv_w, conv_w, conv_b, conv_b, w_down, norm_g)


def _t5_bucket(dist):
    n = jnp.maximum(dist, 0)
    max_exact = REL_BUCKETS // 2
    nf = jnp.maximum(n, 1).astype(F32)
    large = max_exact + (jnp.log(nf / max_exact) / math.log(REL_MAX_DIST / max_exact)
                         * (REL_BUCKETS - max_exact)).astype(jnp.int32)
    large = jnp.minimum(large, REL_BUCKETS - 1)
    return jnp.where(n < max_exact, n, large)


def _toeplitz(v, t):
    h = v.shape[0]
    period = 2 * t + 1
    v = jnp.pad(v, ((0, 0), (0, 1)))
    rows = jnp.tile(v, (1, t))[:, :t * (period - 1)].reshape(h, t, period - 1)
    return rows[:, :, t:]


def _attn_bias_tiles(rel_bias, blk):
    dist = jnp.arange(-blk, 2 * blk, dtype=jnp.int32)
    table = rel_bias.astype(F32) * LOG2E
    bucket = _t5_bucket(dist)
    bias = jnp.broadcast_to(table[REL_BUCKETS - 1][:, None], (DA_HEADS, 3 * blk))
    for b in range(REL_BUCKETS - 1):
        bias = jnp.where(bucket[None, :] == b, table[b][:, None], bias)
    bias = jnp.where(dist[None, :] >= 0, bias, NEG_BIG)
    return jnp.stack([_toeplitz(bias[:, 0:2 * blk], blk),
                      _toeplitz(bias[:, blk:3 * blk], blk)], axis=1)


def _s5_params(lam_re, lam_im, log_dt, b_re, b_im, c_re, c_im):
    dt = jnp.exp(log_dt)[:, None]
    mag = jnp.exp(lam_re * dt)
    a_re = mag * jnp.cos(lam_im * dt)
    a_im = mag * jnp.sin(lam_im * dt)
    den = lam_re * lam_re + lam_im * lam_im
    z_re = ((a_re - 1.0) * lam_re + a_im * lam_im) / den
    z_im = (a_im * lam_re - (a_re - 1.0) * lam_im) / den
    bb_re = z_re[..., None] * b_re - z_im[..., None] * b_im
    bb_im = z_re[..., None] * b_im + z_im[..., None] * b_re
    gs = S5_GROUPS // S5_SLABS
    eye = jnp.eye(gs, dtype=F32)

    def in_blocks(bb):
        bb = bb.reshape(S5_SLABS, gs, S5_STATE, S5_GROUP)
        w = jnp.einsum('sgpc,gh->sgchp', bb, eye)
        return w.reshape(S5_SLABS, gs * S5_GROUP, gs * S5_STATE)

    def out_blocks(cc):
        cc = cc.reshape(S5_SLABS, gs, S5_GROUP, S5_STATE)
        w = jnp.einsum('sgcp,gh->sgphc', cc, eye)
        return w.reshape(S5_SLABS, gs * S5_STATE, gs * S5_GROUP)

    wb = jnp.concatenate([in_blocks(bb_re), in_blocks(bb_im)], axis=-1).astype(BF16)
    wc = jnp.concatenate([out_blocks(c_re), out_blocks(-c_im)], axis=-2).astype(BF16)
    a_re = a_re.reshape(S5_SLABS, 1, S5_SLAB_STATES)
    a_im = a_im.reshape(S5_SLABS, 1, S5_SLAB_STATES)
    return wb, wc, a_re, a_im


IN_PROJ_TM, IN_PROJ_TN = 1024, 2048
MLSTM_CHUNK = 128
ATTN_BLOCK = 512
ATTN_HEADS_PER_STEP = 2
S5_STEPS = 32
MERGE_TM = 256
FFN_TL, FFN_TF = 512, 512


def _layer(x2d, batch, seq, layer, p):
    n_if = 2 * M_HEADS
    proj, gates = _in_proj(x2d, p['norm_mix_pre'][layer][None, :], p['w_gate'],
                           p['w_main'], layer=layer, tm=min(IN_PROJ_TM, batch * seq),
                           tn=IN_PROJ_TN)

    gate_bias = jnp.pad(p['mlstm_b_if'][layer].reshape(1, n_if),
                        ((0, 0), (0, N_GATE_PAD - n_if)))
    wb, wc, a_re, a_im = _s5_params(
        p['s5_lambda_re'][layer], p['s5_lambda_im'][layer], p['s5_log_dt'][layer],
        p['s5_b_re'][layer], p['s5_b_im'][layer], p['s5_c_re'][layer], p['s5_c_im'][layer])
    y_a, y_c = _mlstm_s5(proj, gates, gate_bias, p['mlstm_conv'][layer],
                         p['mlstm_norm'][layer][None, :], wb, wc, a_re, a_im,
                         p['s5_d'][layer][None, :], p['w_glu'], layer=layer, batch=batch,
                         seq=seq, steps=min(S5_STEPS, seq), chunk=MLSTM_CHUNK)
    y_c = y_c.reshape(batch * seq, BRANCH_WIDTH)

    lambda_init = 0.8 - 0.6 * math.exp(-0.3 * layer)
    lam = p['diff_lambda'][layer]
    lam_full = (jnp.exp(jnp.sum(lam[0] * lam[1])) - jnp.exp(jnp.sum(lam[2] * lam[3]))
                + lambda_init)
    blk = min(ATTN_BLOCK, seq)
    scalars = jnp.concatenate(
        [lam_full[None], p['rel_bias'][REL_BUCKETS - 1, :] * LOG2E]).astype(F32)
    y_b = _diff_attn(scalars, proj, p['attn_bias_tiles'],
                     p['diff_norm'][layer][None, :], batch=batch, seq=seq, blk=blk,
                     heads=ATTN_HEADS_PER_STEP, out_scale=1.0 - lambda_init)

    x2d = _merge(y_a, y_b, y_c, proj, x2d, p['w_branch'], p['w_out'],
                 p['norm_mix_post'][layer][None, :], layer=layer, tm=MERGE_TM)

    x3d = _ffn(x2d.reshape(batch, seq, D_MODEL), p['norm_ffn_pre'][layer][None, :],
               p['w_up'], p['ffn_conv'][layer], p['ffn_conv_b'][layer][None, :],
               p['w_down'], p['norm_ffn_post'][layer][None, :], layer=layer,
               tl=min(FFN_TL, seq), tf=FFN_TF)
    return x3d.reshape(batch * seq, D_MODEL)


def kernel(x, norm_mix_pre, norm_mix_post, norm_ffn_pre, norm_ffn_post, w_in, mlstm_b_if, mlstm_conv, mlstm_norm, diff_lambda, diff_norm, rel_bias, s5_lambda_re, s5_lambda_im, s5_log_dt, s5_b_re, s5_b_im, s5_c_re, s5_c_im, s5_d, s5_w_glu, w_branch, w_out, w_up, ffn_conv, ffn_conv_b, w_down):
    n_if = 2 * M_HEADS
    split = 4 * M_HEADS * M_HEAD_DIM
    w_main = jnp.concatenate([w_in[:, :, :split], w_in[:, :, split + n_if:]],
                             axis=2).astype(BF16)
    w_gate = jnp.pad(w_in[:, :, split:split + n_if],
                     ((0, 0), (0, 0), (0, N_GATE_PAD - n_if))).astype(BF16)
    p = dict(norm_mix_pre=norm_mix_pre, norm_mix_post=norm_mix_post,
             norm_ffn_pre=norm_ffn_pre, norm_ffn_post=norm_ffn_post,
             w_main=w_main, w_gate=w_gate,
             mlstm_b_if=mlstm_b_if, mlstm_conv=mlstm_conv, mlstm_norm=mlstm_norm,
             diff_lambda=diff_lambda, diff_norm=diff_norm, rel_bias=rel_bias,
             s5_lambda_re=s5_lambda_re, s5_lambda_im=s5_lambda_im, s5_log_dt=s5_log_dt,
             s5_b_re=s5_b_re, s5_b_im=s5_b_im, s5_c_re=s5_c_re, s5_c_im=s5_c_im,
             s5_d=s5_d, w_glu=s5_w_glu.astype(BF16), w_branch=w_branch.astype(BF16),
             w_out=w_out.astype(BF16), w_up=w_up.astype(BF16),
             ffn_conv=ffn_conv, ffn_conv_b=ffn_conv_b, w_down=w_down.astype(BF16))
    batch, seq, _ = x.shape
    p['attn_bias_tiles'] = _attn_bias_tiles(rel_bias, min(ATTN_BLOCK, seq))
    x2d = x.reshape(batch * seq, D_MODEL)
    for layer in range(DEPTH):
        x2d = _layer(x2d, batch, seq, layer, p)
    return x2d.reshape(batch, seq, D_MODEL)
```
